```python
import math
import jax, jax.numpy as jnp
from jax import lax
import numpy as np

D_MODEL = 1024
BATCH = 32
SEQ = 256
DEPTH = 2
DEC_BATCH = 2
DEC_SEQ = 4096
PAST_LEN = 256

GRID_W = 64
D_A = 512
CONV_K = 31
CONV_PAD = (CONV_K - 1) // 2
H_B = 4
DK_B = 128
DV_B = 128
D_B = H_B * DV_B
CHUNK = 32
H_C = 4
DH_C = 64
D_C = H_C * 2 * DH_C
QBLK = 128
ROPE_BASE = 10000.0
D_FF = ((8 * D_MODEL + 3 * 256 - 1) // (3 * 256)) * 256
SPLIT_SIZES = (2 * D_A, D_B, D_B, D_B, D_B, D_B, D_C, D_C, D_C, 3 * D_MODEL)
IN_COLS = 2 * D_A + 5 * D_B + 3 * D_C + 3 * D_MODEL
N_BRANCH = 3

kernel_name = 'hybrid_diffusion_conv_hgrn2_diffattn_step'


def rmsnorm(x, g, eps=1e-6):
    xf = x.astype(jnp.float32)
    y = xf * lax.rsqrt(jnp.mean(xf * xf, axis=-1, keepdims=True) + eps)
    return (y * g.astype(jnp.float32)).astype(x.dtype)


def layernorm(x, g, b, eps=1e-5):
    xf = x.astype(jnp.float32)
    mu = jnp.mean(xf, axis=-1, keepdims=True)
    xc = xf - mu
    y = xc * lax.rsqrt(jnp.mean(xc * xc, axis=-1, keepdims=True) + eps)
    return (y * g.astype(jnp.float32) + b.astype(jnp.float32)).astype(x.dtype)


def axial_rope(n):
    rows = n // GRID_W
    row = jnp.broadcast_to(jnp.arange(rows)[:, None], (rows, GRID_W)).reshape(-1).astype(jnp.float32)
    col = jnp.broadcast_to(jnp.arange(GRID_W)[None, :], (rows, GRID_W)).reshape(-1).astype(jnp.float32)
    half = DH_C // 2
    inv = ROPE_BASE ** (-jnp.arange(0, half, 2, dtype=jnp.float32) / half)
    ang = jnp.concatenate([row[:, None] * inv, col[:, None] * inv], axis=-1)
    return jnp.cos(ang), jnp.sin(ang)


def apply_rope(x, cos, sin):
    c = cos[None, :, None, None, :].astype(x.dtype)
    s = sin[None, :, None, None, :].astype(x.dtype)
    x1 = x[..., 0::2]
    x2 = x[..., 1::2]
    return jnp.stack([x1 * c - x2 * s, x1 * s + x2 * c], axis=-1).reshape(x.shape)


def gla_chunk_scan(q, k, v, logf, s0):
    B, N, H, _ = q.shape
    DV = v.shape[-1]
    n = N // CHUNK

    def blocks(t):
        return t.reshape(B, n, CHUNK, H, t.shape[-1]).transpose(1, 0, 3, 2, 4)

    causal = jnp.tril(jnp.ones((CHUNK, CHUNK), dtype=bool))[:, :, None]

    def step(S, inp):
        qc, kc, vc, gc = inp
        b = jnp.cumsum(gc, axis=2)
        o_inter = jnp.einsum('bhtd,bhde->bhte', qc * jnp.exp(b), S)
        diff = b[:, :, :, None, :] - b[:, :, None, :, :]
        decay = jnp.where(causal, jnp.exp(jnp.where(causal, diff, 0.0)), 0.0)
        a = jnp.einsum('bhtsd,bhsd->bhts', qc[:, :, :, None, :] * decay, kc)
        o_intra = jnp.einsum('bhts,bhse->bhte', a, vc)
        b_last = b[:, :, -1:, :]
        S = jnp.exp(b_last[:, :, 0, :])[..., None] * S + jnp.einsum('bhsd,bhse->bhde', kc * jnp.exp(b_last - b), vc)
        return S, o_inter + o_intra

    s_fin, o = lax.scan(step, s0, (blocks(q), blocks(k), blocks(v), blocks(logf)))
    o = o.transpose(1, 0, 3, 2, 4).reshape(B, N, H, DV)
    return o, s_fin


def diff_attention(q, k, v, lam):
    B, N, H, _, DH = q.shape
    nb = N // QBLK
    qb = jnp.moveaxis(q.reshape(B, nb, QBLK, H, 2, DH), 1, 0)
    scale = DH ** -0.5

    def one_block(qblk):
        s = jnp.einsum('bqhcd,bkhcd->bhcqk', qblk, k, preferred_element_type=jnp.float32) * scale
        p = jax.nn.softmax(s, axis=-1)
        w = (p[:, :, 0] - lam * p[:, :, 1]).astype(v.dtype)
        return jnp.einsum('bhqk,bkhe->bqhe', w, v)

    o = lax.map(one_block, qb)
    return jnp.moveaxis(o, 0, 1).reshape(B, N, H, v.shape[-1])


def trunk_layer(x, cvec, l, rope, ctx, params):
    (w_mod, b_mod, norm1, norm2, w_in, conv_w, conv_b, conv_ln_g, conv_ln_b, hgrn_lb, hgrn_norm,
     q_norm, k_norm, lambda_qk, subln, w_branch, w_out, w_ffn_in, w_ffn_out) = params
    B, N, _ = x.shape
    f32 = jnp.float32

    mod = jnp.dot(jax.nn.silu(cvec), w_mod[l]) + b_mod[l]
    sh1, sc1, g1, sh2, sc2, g2 = jnp.split(mod[:, None, :], 6, axis=-1)

    h = rmsnorm(x, norm1[l]) * (1 + sc1) + sh1
    u = h @ w_in[l]
    idx = np.cumsum(SPLIT_SIZES)[:-1].tolist()
    a_glu, hq, hi, hf_f, hf_b, hg, aq, ak, av, gates = jnp.split(u, idx, axis=-1)

    a = a_glu[..., :D_A] * jax.nn.sigmoid(a_glu[..., D_A:])
    a = lax.conv_general_dilated(a, conv_w[l][:, None, :], (1,), [(CONV_PAD, CONV_PAD)],
                                 dimension_numbers=('NWC', 'WIO', 'NWC'), feature_group_count=D_A) + conv_b[l]
    a = jax.nn.silu(layernorm(a, conv_ln_g[l], conv_ln_b[l]))
    y_a = a @ w_branch[l, :D_A]

    sm = jax.nn.softmax(hgrn_lb.astype(f32), axis=0)
    lb = (jnp.cumsum(sm, axis=0)[l] - sm[0]).reshape(2, 1, 1, H_B, DK_B)
    z = jnp.stack([hf_f, hf_b], axis=0).astype(f32).reshape(2, B, N, H_B, DK_B)
    fgate = lb + (1.0 - lb) * jax.nn.sigmoid(z)
    kgate = 1.0 - fgate
    logf = jnp.log(fgate)
    qb_ = jax.nn.silu(hq.astype(f32)).reshape(B, N, H_B, DK_B)
    vb_ = hi.astype(f32).reshape(B, N, H_B, DV_B)
    if ctx is None:
        s0 = jnp.zeros((B, 2, H_B, DK_B, DV_B), f32)
    else:
        s0 = jnp.broadcast_to(ctx[2].astype(f32), (B, 2, H_B, DK_B, DV_B))
    o_f, s_f = gla_chunk_scan(qb_, kgate[0], vb_, logf[0], s0[:, 0])
    fl = lambda t: jnp.flip(t, axis=1)
    o_b, s_b = gla_chunk_scan(fl(qb_), fl(kgate[1]), fl(vb_), fl(logf[1]), s0[:, 1])
    o_hg = rmsnorm(o_f + fl(o_b), hgrn_norm[l]) * jax.nn.silu(hg.astype(f32).reshape(B, N, H_B, DV_B))
    y_b = o_hg.reshape(B, N, D_B).astype(x.dtype) @ w_branch[l, D_A:D_A + D_B]
    new_s = jnp.stack([s_f, s_b], axis=1).astype(x.dtype)

    q = rmsnorm(aq.reshape(B, N, H_C, 2, DH_C), q_norm[l])
    k = rmsnorm(ak.reshape(B, N, H_C, 2, DH_C), k_norm[l])
    v = av.reshape(B, N, H_C, 2 * DH_C)
    if rope is not None:
        q = apply_rope(q, rope[0], rope[1])
        k = apply_rope(k, rope[0], rope[1])
    if ctx is None:
        k_all, v_all = k, v
    else:
        k_all = jnp.concatenate([ctx[0].astype(k.dtype), k], axis=1)
        v_all = jnp.concatenate([ctx[1].astype(v.dtype), v], axis=1)
    lam_init = 0.8 - 0.6 * math.exp(-0.3 * l)
    lq = lambda_qk[l].astype(f32)
    lam = jnp.exp(jnp.sum(lq[0] * lq[1])) - jnp.exp(jnp.sum(lq[2] * lq[3])) + lam_init
    o_c = diff_attention(q, k_all, v_all, lam)
    o_c = rmsnorm(o_c, subln[l]) * (1.0 - lam_init)
    y_c = o_c.reshape(B, N, D_C) @ w_branch[l, D_A + D_B:]

    s = jax.nn.sigmoid(gates.reshape(B, N, N_BRANCH, D_MODEL))
    m = s[:, :, 0] * y_a + s[:, :, 1] * y_b + s[:, :, 2] * y_c
    x = x + g1 * (m @ w_out[l])

    h2 = rmsnorm(x, norm2[l]) * (1 + sc2) + sh2
    gu = h2 @ w_ffn_in[l]
    ff = jax.nn.silu(gu[..., :D_FF]) * gu[..., D_FF:]
    x = x + g2 * (ff @ w_ffn_out[l])
    return x, k, v, new_s


def setup_inputs(seed: int = 0) -> dict:
    key = jax.random.key(seed)
    ks = jax.random.split(key, 32)
    nrm = lambda i, shape, sc: jax.random.normal(ks[i], shape, jnp.float32) * sc
    D = D_MODEL
    return {
        'x_prompt': nrm(0, (BATCH, SEQ, D), 1.0),
        'x_sample': nrm(1, (DEC_BATCH, DEC_SEQ, D), 1.0),
        'cache_k': nrm(2, (DEC_BATCH, DEPTH, PAST_LEN, H_C, 2, DH_C), 1.0),
        'cache_v': nrm(3, (DEC_BATCH, DEPTH, PAST_LEN, H_C, 2 * DH_C), 1.0),
        'state_hgrn': nrm(4, (DEC_BATCH, DEPTH, 2, H_B, DK_B, DV_B), 0.5),
        'c': nrm(5, (DEC_BATCH, D), 1.0),
        'c_ctx': nrm(6, (D,), 1.0),
        'w_mod': nrm(7, (DEPTH, D, 6 * D), 0.5 * D ** -0.5),
        'b_mod': nrm(8, (DEPTH, 6 * D), 0.02),
        'norm1': 1.0 + nrm(9, (DEPTH, D), 0.05),
        'norm2': 1.0 + nrm(10, (DEPTH, D), 0.05),
        'w_in': nrm(11, (DEPTH, D, IN_COLS), D ** -0.5),
        'conv_w': nrm(12, (DEPTH, CONV_K, D_A), CONV_K ** -0.5),
        'conv_b': nrm(13, (DEPTH, D_A), 0.02),
        'conv_ln_g': 1.0 + nrm(14, (DEPTH, D_A), 0.05),
        'conv_ln_b': nrm(15, (DEPTH, D_A), 0.02),
        'hgrn_lb': nrm(16, (DEPTH, 2, D_B), 0.1),
        'hgrn_norm': 1.0 + nrm(17, (DEPTH, DV_B), 0.05),
        'q_norm': 1.0 + nrm(18, (DEPTH, DH_C), 0.05),
        'k_norm': 1.0 + nrm(19, (DEPTH, DH_C), 0.05),
        'lambda_qk': nrm(20, (DEPTH, 4, DH_C), 0.1),
        'subln': 1.0 + nrm(21, (DEPTH, 2 * DH_C), 0.05),
        'w_branch': nrm(22, (DEPTH, D_A + D_B + D_C, D), 512 ** -0.5),
        'w_out': nrm(23, (DEPTH, D, D), D ** -0.5),
        'w_ffn_in': nrm(24, (DEPTH, D, 2 * D_FF), D ** -0.5),
        'w_ffn_out': nrm(25, (DEPTH, D_FF, D), D_FF ** -0.5),
    }


def reference(x_prompt, x_sample, cache_k, cache_v, state_hgrn, c, c_ctx, w_mod, b_mod, norm1, norm2, w_in,
              conv_w, conv_b, conv_ln_g, conv_ln_b, hgrn_lb, hgrn_norm, q_norm, k_norm, lambda_qk, subln,
              w_branch, w_out, w_ffn_in, w_ffn_out):
    params = (w_mod, b_mod, norm1, norm2, w_in, conv_w, conv_b, conv_ln_g, conv_ln_b, hgrn_lb, hgrn_norm,
              q_norm, k_norm, lambda_qk, subln, w_branch, w_out, w_ffn_in, w_ffn_out)

    xp = x_prompt
    cvec_ctx = c_ctx[None, :]
    ks, vs, ss = [], [], []
    for l in range(DEPTH):
        xp, k_l, v_l, s_l = trunk_layer(xp, cvec_ctx, l, None, None, params)
        ks.append(k_l)
        vs.append(v_l)
        ss.append(s_l)
    new_cache_k = jnp.stack(ks, axis=1)
    new_cache_v = jnp.stack(vs, axis=1)
    new_state_hgrn = jnp.stack(ss, axis=1)

    xs = x_sample
    rope = axial_rope(xs.shape[1])
    for l in range(DEPTH):
        ctx = (cache_k[:, l], cache_v[:, l], state_hgrn[:, l])
        xs, _, _, _ = trunk_layer(xs, c, l, rope, ctx, params)

    return (xp, xs, new_cache_k, new_cache_v, new_state_hgrn)
```

```python
import functools
import math

import numpy as np
import jax
import jax.numpy as jnp
from jax import lax
from jax.experimental import pallas as pl
from jax.experimental.pallas import tpu as pltpu

F32 = jnp.float32
BF16 = jnp.bfloat16

D = 1024
BATCH, SEQ = 32, 256
DEPTH = 2
DEC_BATCH, DEC_SEQ = 2, 4096
PAST = 256
GRID_W = 64
D_A, CONV_K = 512, 31
H_B, DK_B = 4, 128
D_B = 512
H_C, DH_C = 4, 64
D_C = 512
ROPE_BASE = 10000.0
D_FF = 2816
IN_COLS = 8192

TM = 256
CTX_TILES = BATCH * SEQ // TM
SMP_TILES = DEC_SEQ // TM
T = BATCH * SEQ + DEC_BATCH * DEC_SEQ
NT = T // TM
CB = 512
HALO = 16
TQ = 512
VMEM_LIMIT = 56 * 1024 * 1024

C_A, C_AG, C_HQ, C_HI, C_HFF, C_HFB, C_HG, C_AQ, C_AK, C_AV, C_GT = 0, 1, 2, 3, 4, 5, 6, 7, 8, 9, 10


def _cparams(sem):
    return pltpu.CompilerParams(dimension_semantics=sem, vmem_limit_bytes=VMEM_LIMIT)


def _mod_row(i):
    return jnp.where(i < CTX_TILES, 0, 1 + (i - CTX_TILES) // SMP_TILES)


def _sigmoid(x):
    return 1.0 / (1.0 + jnp.exp(-x))


def _silu(x):
    return x * _sigmoid(x)


def _dot(a, b):
    return jnp.dot(a, b, preferred_element_type=F32)


def _dot_nt(a, b):
    return lax.dot_general(a, b, (((1,), (1,)), ((), ())), preferred_element_type=F32)


def _dot_tn(a, b):
    return lax.dot_general(a, b, (((0,), (0,)), ((), ())), preferred_element_type=F32)


def _split_bf16(x):
    hi = x.astype(BF16)
    lo = (x - hi.astype(F32)).astype(BF16)
    return hi, lo


MOD_TN = 1536


def _mod_kernel(cv_ref, w_ref, b_ref, o_ref):
    a = _silu(cv_ref[...])
    a_hi, a_lo = _split_bf16(a)
    w_hi, w_lo = _split_bf16(w_ref[0])
    acc = _dot(a_hi, w_hi) + _dot(a_lo, w_hi) + _dot(a_hi, w_lo)
    o_ref[0] = acc + b_ref[0]


def _modulation(cv8, w_mod, b_mod):
    return pl.pallas_call(
        _mod_kernel,
        grid=(DEPTH, 6 * D // MOD_TN),
        in_specs=[
            pl.BlockSpec((8, D), lambda l, j: (0, 0)),
            pl.BlockSpec((1, D, MOD_TN), lambda l, j: (l, 0, j)),
            pl.BlockSpec((1, 1, MOD_TN), lambda l, j: (l, 0, j)),
        ],
        out_specs=pl.BlockSpec((1, 8, MOD_TN), lambda l, j: (l, 0, j)),
        out_shape=jax.ShapeDtypeStruct((DEPTH, 8, 6 * D), F32),
        compiler_params=_cparams(("arbitrary", "arbitrary")),
        name="adaln_mod",
    )(cv8, w_mod, b_mod.reshape(DEPTH, 1, 6 * D))


def _mod_spec(l, which):
    return pl.BlockSpec((1, 1, D), lambda i: ((l * 6 + which) * 8 + _mod_row(i), 0, 0))


def _rms_mod(x, g, sc, sh):
    y = x * lax.rsqrt(jnp.mean(x * x, axis=-1, keepdims=True) + 1e-6) * g
    return y * (1.0 + sc) + sh


def _in_kernel(x_ref, g_ref, sc_ref, sh_ref, w_ref, o_ref):
    h = _rms_mod(x_ref[...], g_ref[...], sc_ref[0], sh_ref[0])
    o_ref[...] = _dot(h.astype(BF16), w_ref[...])


def _in_proj(x, modt, norm1_l, w_in_l, l):
    return pl.pallas_call(
        _in_kernel,
        grid=(NT,),
        in_specs=[
            pl.BlockSpec((TM, D), lambda i: (i, 0)),
            pl.BlockSpec((1, D), lambda i: (0, 0)),
            _mod_spec(l, 1),
            _mod_spec(l, 0),
            pl.BlockSpec((D, IN_COLS), lambda i: (0, 0), pipeline_mode=pl.Buffered(1)),
        ],
        out_specs=pl.BlockSpec((TM, IN_COLS), lambda i: (i, 0)),
        out_shape=jax.ShapeDtypeStruct((T, IN_COLS), F32),
        compiler_params=_cparams(("arbitrary",)),
        name="in_proj",
    )(x, norm1_l.reshape(1, D), modt, modt, w_in_l)


def _conv_kernel(ac_ref, gc_ref, ap_ref, gp_ref, an_ref, gn_ref, w_ref, b_ref, lg_ref, lb_ref, o_ref, pad_ref):
    i = pl.program_id(0)
    in_smp = i >= CTX_TILES
    pos = i % SMP_TILES
    has_prev = jnp.logical_and(in_smp, pos != 0)
    has_next = jnp.logical_and(in_smp, pos != SMP_TILES - 1)
    pad_ref[HALO:HALO + TM, :] = ac_ref[...] * _sigmoid(gc_ref[...])
    pad_ref[0:HALO, :] = jnp.where(has_prev, ap_ref[...] * _sigmoid(gp_ref[...]), 0.0)
    pad_ref[HALO + TM:, :] = jnp.where(has_next, an_ref[...] * _sigmoid(gn_ref[...]), 0.0)
    off = HALO - CONV_K // 2
    acc = jnp.zeros((TM, D_A), F32)
    for k in range(CONV_K):
        acc = acc + pad_ref[off + k:off + k + TM, :] * w_ref[k:k + 1, :]
    acc = acc + b_ref[...]
    mu = jnp.mean(acc, axis=-1, keepdims=True)
    xc = acc - mu
    y = xc * lax.rsqrt(jnp.mean(xc * xc, axis=-1, keepdims=True) + 1e-5)
    y = y * lg_ref[...] + lb_ref[...]
    o_ref[...] = _silu(y).astype(BF16)


def _conv_branch(u, conv_w_l, conv_b_l, ln_g_l, ln_b_l):
    r = TM // HALO
    last = T // HALO - 1
    cur = lambda c: pl.BlockSpec((TM, CB), lambda i: (i, c))
    prev = lambda c: pl.BlockSpec((HALO, CB), lambda i: (jnp.maximum(i * r - 1, 0), c))
    nxt = lambda c: pl.BlockSpec((HALO, CB), lambda i: (jnp.minimum((i + 1) * r, last), c))
    vec = pl.BlockSpec((1, D_A), lambda i: (0, 0))
    return pl.pallas_call(
        _conv_kernel,
        grid=(NT,),
        in_specs=[cur(C_A), cur(C_AG), prev(C_A), prev(C_AG), nxt(C_A), nxt(C_AG),
                  pl.BlockSpec((CONV_K, D_A), lambda i: (0, 0)), vec, vec, vec],
        out_specs=pl.BlockSpec((TM, D_A), lambda i: (i, 0)),
        out_shape=jax.ShapeDtypeStruct((T, D_A), BF16),
        scratch_shapes=[pltpu.VMEM((TM + 2 * HALO, D_A), F32)],
        compiler_params=_cparams(("arbitrary",)),
        name="conv_branch",
    )(u, u, u, u, u, u, conv_w_l, conv_b_l.reshape(1, D_A), ln_g_l.reshape(1, D_A), ln_b_l.reshape(1, D_A))


N_LEVELS = 8


def _level_map(reverse):
    t = np.arange(TM)[:, None]
    s = np.arange(TM)[None, :]
    x = t ^ s
    lv = np.zeros((TM, TM), np.int32)
    for bit in range(N_LEVELS):
        lv[(x >> bit) == 1] = bit + 1
    ok = (t < s) if reverse else (t > s)
    return np.where(ok, lv, 0).astype(np.int32)


def _shift_rows(x, k):
    return pltpu.roll(x, k % TM, 0)


def _cumsum_rows(x, row, reverse):
    sh = 1
    while sh < TM:
        if reverse:
            x = x + jnp.where(row < TM - sh, _shift_rows(x, -sh), 0.0)
        else:
            x = x + jnp.where(row >= sh, _shift_rows(x, sh), 0.0)
        sh *= 2
    return x


def _anchors(cum, row, reverse):
    out = []
    z = cum
    for bit in range(N_LEVELS):
        h = 1 << bit
        hi = (row & h) != 0
        if reverse:
            out.append(jnp.where(hi, z, _shift_rows(z, -h)))
            z = jnp.where(hi, _shift_rows(z, h), z)
        else:
            out.append(jnp.where(hi, _shift_rows(z, h), z))
            z = jnp.where(hi, z, _shift_rows(z, -h))
    return out


def _hgrn_kernel(reverse, l, *refs):
    if reverse:
        (q_ref, v_ref, f_ref, lbp_ref, s0_ref, lv_ref, o_ref, ns_ref, st_ref) = refs
    else:
        (q_ref, v_ref, f_ref, g_ref, ob_ref, lbp_ref, s0_ref, lv_ref, nrm_ref, o_ref, ns_ref, st_ref) = refs
    g = pl.program_id(0)
    i = NT - 1 - g if reverse else g
    is_ctx = i < CTX_TILES
    pos = i % SMP_TILES
    seq_start = jnp.logical_and(i >= CTX_TILES, pos == (SMP_TILES - 1 if reverse else 0))

    @pl.when(is_ctx)
    def _():
        st_ref[...] = jnp.zeros_like(st_ref)

    @pl.when(seq_start)
    def _():
        for h in range(H_B):
            st_ref[h] = s0_ref[0, h].T

    d = 1 if reverse else 0
    rows = [lbp_ref[k * 2 + d:k * 2 + d + 1, :] for k in range(DEPTH)]
    mx = functools.reduce(jnp.maximum, rows)
    ex = [jnp.exp(r - mx) for r in rows]
    den = functools.reduce(jnp.add, ex)
    lb_all = jnp.zeros_like(mx)
    for k in range(1, l + 1):
        lb_all = lb_all + ex[k] / den

    row = lax.broadcasted_iota(jnp.int32, (TM, DK_B), 0)
    lv = lv_ref[...]
    for h in range(H_B):
        sl = slice(h * DK_B, (h + 1) * DK_B)
        lb = lb_all[:, sl]
        fg = lb + (1.0 - lb) * _sigmoid(f_ref[:, sl])
        kg = 1.0 - fg
        q = _silu(q_ref[:, sl])
        v = v_ref[:, sl]
        vb = v.astype(BF16)
        cum = _cumsum_rows(jnp.log(fg), row, reverse)
        tot = cum[0:1, :] if reverse else cum[TM - 1:TM, :]
        st = st_ref[h]
        o = _dot_nt((q * jnp.exp(cum)).astype(BF16), st.astype(BF16))
        a = jnp.zeros((TM, TM), F32)
        for bit, e in enumerate(_anchors(cum, row, reverse)):
            fac = jnp.exp(-jnp.abs(cum - e))
            a_l = _dot_nt((q * fac).astype(BF16), (kg * fac).astype(BF16))
            a = jnp.where(lv == bit + 1, a_l, a)
        o = o + _dot(a.astype(BF16), vb)
        o = o + jnp.sum(q * kg, axis=-1, keepdims=True) * v
        kbar = (kg * jnp.exp(tot - cum)).astype(BF16)
        st_ref[h] = jnp.exp(tot) * st + _dot_tn(vb, kbar)
        if reverse:
            o_ref[:, sl] = o
        else:
            o = o + ob_ref[:, sl]
            y = o * lax.rsqrt(jnp.mean(o * o, axis=-1, keepdims=True) + 1e-6) * nrm_ref[...]
            o_ref[:, sl] = (y * _silu(g_ref[:, sl])).astype(BF16)

    @pl.when(is_ctx)
    def _():
        for h in range(H_B):
            ns_ref[0, h] = st_ref[h].T


def _hgrn_pass(reverse, l, u, hgrn_lb, s0, lvmap, ob=None, hgrn_norm_l=None):
    tile = (lambda g: NT - 1 - g) if reverse else (lambda g: g)
    col = lambda c: pl.BlockSpec((TM, CB), lambda g: (tile(g), c))
    s0_spec = pl.BlockSpec((1, H_B, DK_B, DK_B),
                           lambda g: (jnp.clip((tile(g) - CTX_TILES) // SMP_TILES, 0, DEC_BATCH - 1), 0, 0, 0))
    lbp_spec = pl.BlockSpec((DEPTH * 2, D_B), lambda g: (0, 0))
    lv_spec = pl.BlockSpec((TM, TM), lambda g: (0, 0))
    ns_spec = pl.BlockSpec((1, H_B, DK_B, DK_B), lambda g: (jnp.minimum(tile(g), CTX_TILES - 1), 0, 0, 0))
    ns_shape = jax.ShapeDtypeStruct((BATCH, H_B, DK_B, DK_B), F32)
    f_col = C_HFB if reverse else C_HFF
    if reverse:
        in_specs = [col(C_HQ), col(C_HI), col(f_col), lbp_spec, s0_spec, lv_spec]
        args = (u, u, u, hgrn_lb, s0, lvmap)
        o_shape = jax.ShapeDtypeStruct((T, D_B), F32)
    else:
        in_specs = [col(C_HQ), col(C_HI), col(f_col), col(C_HG), pl.BlockSpec((TM, D_B), lambda g: (g, 0)),
                    lbp_spec, s0_spec, lv_spec, pl.BlockSpec((1, DK_B), lambda g: (0, 0))]
        args = (u, u, u, u, ob, hgrn_lb, s0, lvmap, hgrn_norm_l.reshape(1, DK_B))
        o_shape = jax.ShapeDtypeStruct((T, D_B), BF16)
    return pl.pallas_call(
        functools.partial(_hgrn_kernel, reverse, l),
        grid=(NT,),
        in_specs=in_specs,
        out_specs=[pl.BlockSpec((TM, D_B), lambda g: (tile(g), 0)), ns_spec],
        out_shape=[o_shape, ns_shape],
        scratch_shapes=[pltpu.VMEM((H_B, DK_B, DK_B), F32)],
        compiler_params=_cparams(("arbitrary",)),
        name="hgrn_bwd" if reverse else "hgrn_fwd",
    )(*args)


def _qk_kernel(aq_ref, ak_ref, av_ref, qn_ref, kn_ref, cos_ref, sin_ref, bd_ref, qb_ref, kb_ref, vb_ref, kf_ref):
    bd = bd_ref[...]
    lane = lax.broadcasted_iota(jnp.int32, (TM, 128), 1)
    even = (lane & 1) == 0
    cos = cos_ref[...]
    sin = sin_ref[...]

    def seg_norm(x, g):
        hi, lo = _split_bf16(x * x)
        ss = _dot(hi, bd) + _dot(lo, bd)
        return x * lax.rsqrt(ss * (1.0 / DH_C) + 1e-6) * g

    def rope(x):
        parts = []
        for s in range(D_C // 128):
            xs = x[:, s * 128:(s + 1) * 128]
            sw = jnp.where(even, pltpu.roll(xs, 127, 1), pltpu.roll(xs, 1, 1))
            parts.append(xs * cos + sw * sin)
        return jnp.concatenate(parts, axis=1)

    q = seg_norm(aq_ref[...], qn_ref[...])
    k = seg_norm(ak_ref[...], kn_ref[...])
    kf_ref[...] = k
    qb_ref[...] = (rope(q) * (DH_C ** -0.5)).astype(BF16)
    kb_ref[...] = rope(k).astype(BF16)
    vb_ref[...] = av_ref[...].astype(BF16)


def _rope_tables():
    rows = DEC_SEQ // GRID_W
    row = jnp.broadcast_to(jnp.arange(rows)[:, None], (rows, GRID_W)).reshape(-1).astype(F32)
    colp = jnp.broadcast_to(jnp.arange(GRID_W)[None, :], (rows, GRID_W)).reshape(-1).astype(F32)
    half = DH_C // 2
    inv = ROPE_BASE ** (-jnp.arange(0, half, 2, dtype=F32) / half)
    ang = jnp.concatenate([row[:, None] * inv, colp[:, None] * inv], axis=-1)
    cos = jnp.repeat(jnp.cos(ang), 2, axis=-1)
    sin = jnp.repeat(jnp.sin(ang), 2, axis=-1) * jnp.tile(jnp.array([-1.0, 1.0], F32), half)
    cos = jnp.concatenate([jnp.ones((TM, DH_C), F32), cos], axis=0)
    sin = jnp.concatenate([jnp.zeros((TM, DH_C), F32), sin], axis=0)
    return jnp.tile(cos, (1, 2)), jnp.tile(sin, (1, 2))


def _qk_prep(u, q_norm_l, k_norm_l, cos_t, sin_t, bd):
    col = lambda c: pl.BlockSpec((TM, CB), lambda i: (i, c))
    vec = pl.BlockSpec((1, D_C), lambda i: (0, 0))
    tab = pl.BlockSpec((TM, 128), lambda i: (jnp.where(i < CTX_TILES, 0, 1 + i % SMP_TILES), 0))
    out = pl.BlockSpec((TM, D_C), lambda i: (i, 0))
    return pl.pallas_call(
        _qk_kernel,
        grid=(NT,),
        in_specs=[col(C_AQ), col(C_AK), col(C_AV), vec, vec, tab, tab,
                  pl.BlockSpec((D_C, D_C), lambda i: (0, 0))],
        out_specs=[out, out, out, out],
        out_shape=[jax.ShapeDtypeStruct((T, D_C), BF16)] * 3 + [jax.ShapeDtypeStruct((T, D_C), F32)],
        compiler_params=_cparams(("arbitrary",)),
        name="qk_prep",
    )(u, u, u, jnp.tile(q_norm_l, D_C // DH_C).reshape(1, D_C), jnp.tile(k_norm_l, D_C // DH_C).reshape(1, D_C),
      cos_t, sin_t, bd)


def _attn_kernel(lam_init, n_new, has_cache, *refs):
    if has_cache:
        q_ref, kc_ref, vc_ref, kn_ref, vn_ref, lam_ref, sub_ref, o_ref = refs
    else:
        q_ref, kn_ref, vn_ref, lam_ref, sub_ref, o_ref = refs
    tq = q_ref.shape[0]
    q = q_ref[...]
    lane = lax.broadcasted_iota(jnp.int32, q.shape, 1)
    qs = (jnp.where(lane < DH_C, q, jnp.zeros_like(q)), jnp.where(lane >= DH_C, q, jnp.zeros_like(q)))

    def step(k, v, carry):
        new = []
        for c in range(2):
            m, lsum, acc = carry[c]
            s = _dot_nt(qs[c], k)
            mn = jnp.maximum(m, jnp.max(s, axis=-1, keepdims=True))
            alpha = jnp.exp(m - mn)
            p = jnp.exp(s - mn)
            lsum = alpha * lsum + jnp.sum(p, axis=-1, keepdims=True)
            acc = alpha * acc + _dot(p.astype(BF16), v)
            new.append((mn, lsum, acc))
        return tuple(new)

    init = tuple((jnp.full((tq, 1), -1e30, F32), jnp.zeros((tq, 1), F32), jnp.zeros((tq, 2 * DH_C), F32))
                 for _ in range(2))
    carry = init
    if has_cache:
        carry = step(kc_ref[...], vc_ref[...], carry)
    if n_new == 1:
        carry = step(kn_ref[...], vn_ref[...], carry)
    else:
        def body(j, carry):
            r = pl.multiple_of(j * TM, TM)
            return step(kn_ref[pl.ds(r, TM), :], vn_ref[pl.ds(r, TM), :], carry)
        carry = lax.fori_loop(0, n_new, body, carry)

    lq = lam_ref[...]
    lam = (jnp.exp(jnp.sum(lq[0:1] * lq[1:2], axis=-1, keepdims=True))
           - jnp.exp(jnp.sum(lq[2:3] * lq[3:4], axis=-1, keepdims=True)) + lam_init)
    (_, l0, a0), (_, l1, a1) = carry
    o = a0 / l0 - lam * (a1 / l1)
    y = o * lax.rsqrt(jnp.mean(o * o, axis=-1, keepdims=True) + 1e-6) * sub_ref[...]
    o_ref[...] = (y * (1.0 - lam_init)).astype(BF16)


def _attention(l, qb, kb, vb, ck, cv, lambda_qk_l, subln_l):
    lam_init = 0.8 - 0.6 * math.exp(-0.3 * l)
    lam_spec2 = pl.BlockSpec((4, DH_C), lambda b, h: (0, 0))
    sub_spec2 = pl.BlockSpec((1, 2 * DH_C), lambda b, h: (0, 0))
    sub = subln_l.reshape(1, 2 * DH_C)
    blk = lambda b, h: (b, h)
    o_ctx = pl.pallas_call(
        functools.partial(_attn_kernel, lam_init, 1, False),
        grid=(BATCH, H_C),
        in_specs=[pl.BlockSpec((SEQ, 128), blk)] * 3 + [lam_spec2, sub_spec2],
        out_specs=pl.BlockSpec((SEQ, 128), blk),
        out_shape=jax.ShapeDtypeStruct((BATCH * SEQ, D_C), BF16),
        compiler_params=_cparams(("arbitrary", "arbitrary")),
        name="attn_ctx",
    )(qb, kb, vb, lambda_qk_l, sub)

    nq = DEC_SEQ // TQ
    base_q = BATCH * SEQ // TQ
    base_kv = BATCH * SEQ // DEC_SEQ
    lam_spec3 = pl.BlockSpec((4, DH_C), lambda b, h, j: (0, 0))
    sub_spec3 = pl.BlockSpec((1, 2 * DH_C), lambda b, h, j: (0, 0))
    q_spec = pl.BlockSpec((TQ, 128), lambda b, h, j: (base_q + b * nq + j, h))
    kv_spec = pl.BlockSpec((DEC_SEQ, 128), lambda b, h, j: (base_kv + b, h))
    c_spec = pl.BlockSpec((PAST, 128), lambda b, h, j: (b, h))
    o_smp = pl.pallas_call(
        functools.partial(_attn_kernel, lam_init, DEC_SEQ // TM, True),
        grid=(DEC_BATCH, H_C, nq),
        in_specs=[q_spec, c_spec, c_spec, kv_spec, kv_spec, lam_spec3, sub_spec3],
        out_specs=pl.BlockSpec((TQ, 128), lambda b, h, j: (b * nq + j, h)),
        out_shape=jax.ShapeDtypeStruct((DEC_BATCH * DEC_SEQ, D_C), BF16),
        compiler_params=_cparams(("arbitrary", "arbitrary", "arbitrary")),
        name="attn_smp",
    )(qb, ck, cv, kb, vb, lambda_qk_l, sub)
    return jnp.concatenate([o_ctx, o_smp], axis=0)


def _merge_kernel(a_ref, b_ref, c_ref, ga_ref, gb_ref, gc_ref, x_ref, g1_ref, wb_ref, wo_ref, o_ref):
    ya = _dot(a_ref[...], wb_ref[0:D_A, :])
    yb = _dot(b_ref[...], wb_ref[D_A:D_A + D_B, :])
    yc = _dot(c_ref[...], wb_ref[D_A + D_B:, :])
    m = _sigmoid(ga_ref[...]) * ya + _sigmoid(gb_ref[...]) * yb + _sigmoid(gc_ref[...]) * yc
    o_ref[...] = x_ref[...] + g1_ref[0] * _dot(m.astype(BF16), wo_ref[...])


def _merge(ya, yb, yc, u, x, modt, w_branch_l, w_out_l, l):
    row = lambda w: pl.BlockSpec((TM, w), lambda i: (i, 0))
    gate0 = C_GT * CB // D
    gate = lambda k: pl.BlockSpec((TM, D), lambda i: (i, gate0 + k))
    return pl.pallas_call(
        _merge_kernel,
        grid=(NT,),
        in_specs=[row(D_A), row(D_B), row(D_C), gate(0), gate(1), gate(2),
                  row(D), _mod_spec(l, 2),
                  pl.BlockSpec((D_A + D_B + D_C, D), lambda i: (0, 0)),
                  pl.BlockSpec((D, D), lambda i: (0, 0))],
        out_specs=row(D),
        out_shape=jax.ShapeDtypeStruct((T, D), F32),
        compiler_params=_cparams(("arbitrary",)),
        name="merge_out",
    )(ya, yb, yc, u, u, u, x, modt, w_branch_l, w_out_l)


def _ffn_kernel(x_ref, g_ref, sc_ref, sh_ref, g2_ref, wi_ref, wo_ref, o_ref):
    x = x_ref[...]
    h = _rms_mod(x, g_ref[...], sc_ref[0], sh_ref[0]).astype(BF16)
    gu = _dot(h, wi_ref[...])
    ff = _silu(gu[:, :D_FF]) * gu[:, D_FF:]
    o_ref[...] = x + g2_ref[0] * _dot(ff.astype(BF16), wo_ref[...])


def _ffn(x, modt, norm2_l, w_ffn_in_l, w_ffn_out_l, l):
    return pl.pallas_call(
        _ffn_kernel,
        grid=(NT,),
        in_specs=[pl.BlockSpec((TM, D), lambda i: (i, 0)),
                  pl.BlockSpec((1, D), lambda i: (0, 0)),
                  _mod_spec(l, 4), _mod_spec(l, 3), _mod_spec(l, 5),
                  pl.BlockSpec((D, 2 * D_FF), lambda i: (0, 0), pipeline_mode=pl.Buffered(1)),
                  pl.BlockSpec((D_FF, D), lambda i: (0, 0), pipeline_mode=pl.Buffered(1))],
        out_specs=pl.BlockSpec((TM, D), lambda i: (i, 0)),
        out_shape=jax.ShapeDtypeStruct((T, D), F32),
        compiler_params=_cparams(("arbitrary",)),
        name="ffn",
    )(x, norm2_l.reshape(1, D), modt, modt, modt, w_ffn_in_l, w_ffn_out_l)


def _layer(l, x, modt, cache_k, cache_v, state_hgrn, tabs, p):
    (norm1, norm2, w_in, conv_w, conv_b, conv_ln_g, conv_ln_b, hgrn_lb, hgrn_norm, q_norm, k_norm, lambda_qk,
     subln, w_branch, w_out, w_ffn_in, w_ffn_out) = p
    cos_t, sin_t, bd, lv_f, lv_b = tabs
    u = _in_proj(x, modt, norm1[l], w_in[l].astype(BF16), l)
    ya = _conv_branch(u, conv_w[l], conv_b[l], conv_ln_g[l], conv_ln_b[l])
    lbp = hgrn_lb.reshape(DEPTH * 2, D_B)
    ob, ns_b = _hgrn_pass(True, l, u, lbp, state_hgrn[:, l, 1], lv_b)
    yb, ns_f = _hgrn_pass(False, l, u, lbp, state_hgrn[:, l, 0], lv_f, ob=ob, hgrn_norm_l=hgrn_norm[l])
    qb, kb, vb, kf = _qk_prep(u, q_norm[l], k_norm[l], cos_t, sin_t, bd)
    ck = cache_k[:, l].reshape(DEC_BATCH, PAST, D_C).astype(BF16).reshape(DEC_BATCH * PAST, D_C)
    cv = cache_v[:, l].reshape(DEC_BATCH, PAST, D_C).astype(BF16).reshape(DEC_BATCH * PAST, D_C)
    yc = _attention(l, qb, kb, vb, ck, cv, lambda_qk[l], subln[l])
    x = _merge(ya, yb, yc, u, x, modt, w_branch[l].astype(BF16), w_out[l].astype(BF16), l)
    x = _ffn(x, modt, norm2[l], w_ffn_in[l].astype(BF16), w_ffn_out[l].astype(BF16), l)
    n_ctx = BATCH * SEQ
    new_k = kf[:n_ctx].reshape(BATCH, SEQ, H_C, 2, DH_C)
    new_v = u[:n_ctx, C_AV * CB:(C_AV + 1) * CB].reshape(BATCH, SEQ, H_C, 2 * DH_C)
    new_s = jnp.stack([ns_f, ns_b], axis=1)
    return x, new_k, new_v, new_s


def kernel(x_prompt, x_sample, cache_k, cache_v, state_hgrn, c, c_ctx, w_mod, b_mod, norm1, norm2, w_in, conv_w,
           conv_b, conv_ln_g, conv_ln_b, hgrn_lb, hgrn_norm, q_norm, k_norm, lambda_qk, subln, w_branch, w_out,
           w_ffn_in, w_ffn_out):
    n_ctx = BATCH * SEQ
    x = jnp.concatenate([x_prompt.reshape(n_ctx, D), x_sample.reshape(DEC_BATCH * DEC_SEQ, D)], axis=0)
    cv8 = jnp.concatenate([c_ctx[None, :], c, jnp.zeros((8 - 1 - DEC_BATCH, D), F32)], axis=0)
    mod = _modulation(cv8, w_mod, b_mod)
    modt = mod.reshape(DEPTH, 8, 6, D).transpose(0, 2, 1, 3).reshape(DEPTH * 6 * 8, 1, D)
    cos_t, sin_t = _rope_tables()
    seg = np.arange(D_C) // DH_C
    bd = jnp.asarray(seg[:, None] == seg[None, :], BF16)
    tabs = (cos_t, sin_t, bd, jnp.asarray(_level_map(False)), jnp.asarray(_level_map(True)))
    p = (norm1, norm2, w_in, conv_w, conv_b, conv_ln_g, conv_ln_b, hgrn_lb, hgrn_norm, q_norm, k_norm, lambda_qk,
         subln, w_branch, w_out, w_ffn_in, w_ffn_out)
    ks, vs, ss = [], [], []
    for l in range(DEPTH):
        x, k_l, v_l, s_l = _layer(l, x, modt, cache_k, cache_v, state_hgrn, tabs, p)
        ks.append(k_l)
        vs.append(v_l)
        ss.append(s_l)
    y_prompt = x[:n_ctx].reshape(BATCH, SEQ, D)
    y_sample = x[n_ctx:].reshape(DEC_BATCH, DEC_SEQ, D)
    return (y_prompt, y_sample, jnp.stack(ks, axis=1), jnp.stack(vs, axis=1), jnp.stack(ss, axis=1))
```

```python
import functools
import math

import numpy as np
import jax
import jax.numpy as jnp
from jax import lax
from jax.experimental import pallas as pl
from jax.experimental.pallas import tpu as pltpu

F32 = jnp.float32
BF16 = jnp.bfloat16

D = 1024
BATCH, SEQ = 32, 256
DEPTH = 2
DEC_BATCH, DEC_SEQ = 2, 4096
PAST = 256
GRID_W = 64
D_A, CONV_K = 512, 31
H_B, DK_B = 4, 128
D_B = 512
H_C, DH_C = 4, 64
D_C = 512
ROPE_BASE = 10000.0
D_FF = 2816
IN_COLS = 8192

TM = 256
CTX_TILES = BATCH * SEQ // TM
SMP_TILES = DEC_SEQ // TM
T = BATCH * SEQ + DEC_BATCH * DEC_SEQ
NT = T // TM
CB = 512
HALO = 16
TQ = 512
VMEM_LIMIT = 56 * 1024 * 1024

C_A, C_AG, C_HQ, C_HI, C_HFF, C_HFB, C_HG, C_AQ, C_AK, C_AV, C_GT = 0, 1, 2, 3, 4, 5, 6, 7, 8, 9, 10


def _cparams(sem):
    return pltpu.CompilerParams(dimension_semantics=sem, vmem_limit_bytes=VMEM_LIMIT)


def _mod_row(i):
    return jnp.where(i < CTX_TILES, 0, 1 + (i - CTX_TILES) // SMP_TILES)


def _sigmoid(x):
    return 1.0 / (1.0 + jnp.exp(-x))


def _silu(x):
    return x * _sigmoid(x)


def _dot(a, b):
    return jnp.dot(a, b, preferred_element_type=F32)


def _dot_nt(a, b):
    return lax.dot_general(a, b, (((1,), (1,)), ((), ())), preferred_element_type=F32)


def _dot_tn(a, b):
    return lax.dot_general(a, b, (((0,), (0,)), ((), ())), preferred_element_type=F32)


def _split_bf16(x):
    hi = x.astype(BF16)
    lo = (x - hi.astype(F32)).astype(BF16)
    return hi, lo


MOD_TN = 1536


def _mod_kernel(cv_ref, w_ref, b_ref, o_ref):
    a = _silu(cv_ref[...])
    a_hi, a_lo = _split_bf16(a)
    w_hi, w_lo = _split_bf16(w_ref[0])
    acc = _dot(a_hi, w_hi) + _dot(a_lo, w_hi) + _dot(a_hi, w_lo)
    o_ref[0] = acc + b_ref[0]


def _modulation(cv8, w_mod, b_mod):
    return pl.pallas_call(
        _mod_kernel,
        grid=(DEPTH, 6 * D // MOD_TN),
        in_specs=[
            pl.BlockSpec((8, D), lambda l, j: (0, 0)),
            pl.BlockSpec((1, D, MOD_TN), lambda l, j: (l, 0, j)),
            pl.BlockSpec((1, 1, MOD_TN), lambda l, j: (l, 0, j)),
        ],
        out_specs=pl.BlockSpec((1, 8, MOD_TN), lambda l, j: (l, 0, j)),
        out_shape=jax.ShapeDtypeStruct((DEPTH, 8, 6 * D), F32),
        compiler_params=_cparams(("arbitrary", "arbitrary")),
        name="adaln_mod",
    )(cv8, w_mod, b_mod.reshape(DEPTH, 1, 6 * D))


def _mod_spec(l, which):
    return pl.BlockSpec((1, 1, D), lambda i: ((l * 6 + which) * 8 + _mod_row(i), 0, 0))


def _rms_mod(x, g, sc, sh):
    y = x * lax.rsqrt(jnp.mean(x * x, axis=-1, keepdims=True) + 1e-6) * g
    return y * (1.0 + sc) + sh


def _in_kernel(x_ref, g_ref, sc_ref, sh_ref, w_ref, o_ref):
    h = _rms_mod(x_ref[...], g_ref[...], sc_ref[0], sh_ref[0])
    o_ref[...] = _dot(h.astype(BF16), w_ref[...])


def _in_proj(x, modt, norm1_l, w_in_l, l):
    return pl.pallas_call(
        _in_kernel,
        grid=(NT,),
        in_specs=[
            pl.BlockSpec((TM, D), lambda i: (i, 0)),
            pl.BlockSpec((1, D), lambda i: (0, 0)),
            _mod_spec(l, 1),
            _mod_spec(l, 0),
            pl.BlockSpec((D, IN_COLS), lambda i: (0, 0), pipeline_mode=pl.Buffered(1)),
        ],
        out_specs=pl.BlockSpec((TM, IN_COLS), lambda i: (i, 0)),
        out_shape=jax.ShapeDtypeStruct((T, IN_COLS), F32),
        compiler_params=_cparams(("arbitrary",)),
        name="in_proj",
    )(x, norm1_l.reshape(1, D), modt, modt, w_in_l)


def _conv_kernel(ac_ref, gc_ref, ap_ref, gp_ref, an_ref, gn_ref, w_ref, b_ref, lg_ref, lb_ref, o_ref, pad_ref):
    i = pl.program_id(0)
    in_smp = i >= CTX_TILES
    pos = i % SMP_TILES
    has_prev = jnp.logical_and(in_smp, pos != 0)
    has_next = jnp.logical_and(in_smp, pos != SMP_TILES - 1)
    pad_ref[HALO:HALO + TM, :] = ac_ref[...] * _sigmoid(gc_ref[...])
    pad_ref[0:HALO, :] = jnp.where(has_prev, ap_ref[...] * _sigmoid(gp_ref[...]), 0.0)
    pad_ref[HALO + TM:, :] = jnp.where(has_next, an_ref[...] * _sigmoid(gn_ref[...]), 0.0)
    off = HALO - CONV_K // 2
    acc = jnp.zeros((TM, D_A), F32)
    for k in range(CONV_K):
        acc = acc + pad_ref[off + k:off + k + TM, :] * w_ref[k:k + 1, :]
    acc = acc + b_ref[...]
    mu = jnp.mean(acc, axis=-1, keepdims=True)
    xc = acc - mu
    y = xc * lax.rsqrt(jnp.mean(xc * xc, axis=-1, keepdims=True) + 1e-5)
    y = y * lg_ref[...] + lb_ref[...]
    o_ref[...] = _silu(y).astype(BF16)


def _conv_branch(u, conv_w_l, conv_b_l, ln_g_l, ln_b_l):
    r = TM // HALO
    last = T // HALO - 1
    cur = lambda c: pl.BlockSpec((TM, CB), lambda i: (i, c))
    prev = lambda c: pl.BlockSpec((HALO, CB), lambda i: (jnp.maximum(i * r - 1, 0), c))
    nxt = lambda c: pl.BlockSpec((HALO, CB), lambda i: (jnp.minimum((i + 1) * r, last), c))
    vec = pl.BlockSpec((1, D_A), lambda i: (0, 0))
    return pl.pallas_call(
        _conv_kernel,
        grid=(NT,),
        in_specs=[cur(C_A), cur(C_AG), prev(C_A), prev(C_AG), nxt(C_A), nxt(C_AG),
                  pl.BlockSpec((CONV_K, D_A), lambda i: (0, 0)), vec, vec, vec],
        out_specs=pl.BlockSpec((TM, D_A), lambda i: (i, 0)),
        out_shape=jax.ShapeDtypeStruct((T, D_A), BF16),
        scratch_shapes=[pltpu.VMEM((TM + 2 * HALO, D_A), F32)],
        compiler_params=_cparams(("arbitrary",)),
        name="conv_branch",
    )(u, u, u, u, u, u, conv_w_l, conv_b_l.reshape(1, D_A), ln_g_l.reshape(1, D_A), ln_b_l.reshape(1, D_A))


N_LEVELS = 8


def _level_map(reverse):
    t = np.arange(TM)[:, None]
    s = np.arange(TM)[None, :]
    x = t ^ s
    lv = np.zeros((TM, TM), np.int32)
    for bit in range(N_LEVELS):
        lv[(x >> bit) == 1] = bit + 1
    ok = (t < s) if reverse else (t > s)
    return np.where(ok, lv, 0).astype(np.int32)


def _shift_rows(x, k):
    return pltpu.roll(x, k % TM, 0)


def _cumsum_rows(x, row, reverse):
    sh = 1
    while sh < TM:
        if reverse:
            x = x + jnp.where(row < TM - sh, _shift_rows(x, -sh), 0.0)
        else:
            x = x + jnp.where(row >= sh, _shift_rows(x, sh), 0.0)
        sh *= 2
    return x


def _anchors(cum, row, reverse):
    out = []
    z = cum
    for bit in range(N_LEVELS):
        h = 1 << bit
        hi = (row & h) != 0
        if reverse:
            out.append(jnp.where(hi, z, _shift_rows(z, -h)))
            z = jnp.where(hi, _shift_rows(z, h), z)
        else:
            out.append(jnp.where(hi, _shift_rows(z, h), z))
            z = jnp.where(hi, z, _shift_rows(z, -h))
    return out


def _hgrn_kernel(reverse, l, *refs):
    if reverse:
        (q_ref, v_ref, f_ref, lbp_ref, s0_ref, lv_ref, o_ref, ns_ref, st_ref) = refs
    else:
        (q_ref, v_ref, f_ref, g_ref, ob_ref, lbp_ref, s0_ref, lv_ref, nrm_ref, o_ref, ns_ref, st_ref) = refs
    g = pl.program_id(0)
    i = NT - 1 - g if reverse else g
    is_ctx = i < CTX_TILES
    pos = i % SMP_TILES
    seq_start = jnp.logical_and(i >= CTX_TILES, pos == (SMP_TILES - 1 if reverse else 0))

    @pl.when(is_ctx)
    def _():
        st_ref[...] = jnp.zeros_like(st_ref)

    @pl.when(seq_start)
    def _():
        for h in range(H_B):
            st_ref[h] = s0_ref[0, h].T

    d = 1 if reverse else 0
    rows = [lbp_ref[k * 2 + d:k * 2 + d + 1, :] for k in range(DEPTH)]
    mx = functools.reduce(jnp.maximum, rows)
    ex = [jnp.exp(r - mx) for r in rows]
    den = functools.reduce(jnp.add, ex)
    lb_all = jnp.zeros_like(mx)
    for k in range(1, l + 1):
        lb_all = lb_all + ex[k] / den

    row = lax.broadcasted_iota(jnp.int32, (TM, DK_B), 0)
    lv = lv_ref[...]
    for h in range(H_B):
        sl = slice(h * DK_B, (h + 1) * DK_B)
        lb = lb_all[:, sl]
        fg = lb + (1.0 - lb) * _sigmoid(f_ref[:, sl])
        kg = 1.0 - fg
        q = _silu(q_ref[:, sl])
        v = v_ref[:, sl]
        vb = v.astype(BF16)
        cum = _cumsum_rows(jnp.log(fg), row, reverse)
        tot = cum[0:1, :] if reverse else cum[TM - 1:TM, :]
        st = st_ref[h]
        o = _dot_nt((q * jnp.exp(cum)).astype(BF16), st.astype(BF16))
        a = jnp.zeros((TM, TM), F32)
        for bit, e in enumerate(_anchors(cum, row, reverse)):
            fac = jnp.exp(-jnp.abs(cum - e))
            a_l = _dot_nt((q * fac).astype(BF16), (kg * fac).astype(BF16))
            a = jnp.where(lv == bit + 1, a_l, a)
        o = o + _dot(a.astype(BF16), vb)
        o = o + jnp.sum(q * kg, axis=-1, keepdims=True) * v
        kbar = (kg * jnp.exp(tot - cum)).astype(BF16)
        st_ref[h] = jnp.exp(tot) * st + _dot_tn(vb, kbar)
        if reverse:
            o_ref[:, sl] = o
        else:
            o = o + ob_ref[:, sl]
            y = o * lax.rsqrt(jnp.mean(o * o, axis=-1, keepdims=True) + 1e-6) * nrm_ref[...]
            o_ref[:, sl] = (y * _silu(g_ref[:, sl])).astype(BF16)

    @pl.when(is_ctx)
    def _():
        for h in range(H_B):
            ns_ref[0, h] = st_ref[h].T


def _hgrn_pass(reverse, l, u, hgrn_lb, s0, lvmap, ob=None, hgrn_norm_l=None):
    tile = (lambda g: NT - 1 - g) if reverse else (lambda g: g)
    col = lambda c: pl.BlockSpec((TM, CB), lambda g: (tile(g), c))
    s0_spec = pl.BlockSpec((1, H_B, DK_B, DK_B),
                           lambda g: (jnp.clip((tile(g) - CTX_TILES) // SMP_TILES, 0, DEC_BATCH - 1), 0, 0, 0))
    lbp_spec = pl.BlockSpec((DEPTH * 2, D_B), lambda g: (0, 0))
    lv_spec = pl.BlockSpec((TM, TM), lambda g: (0, 0))
    ns_spec = pl.BlockSpec((1, H_B, DK_B, DK_B), lambda g: (jnp.minimum(tile(g), CTX_TILES - 1), 0, 0, 0))
    ns_shape = jax.ShapeDtypeStruct((BATCH, H_B, DK_B, DK_B), F32)
    f_col = C_HFB if reverse else C_HFF
    if reverse:
        in_specs = [col(C_HQ), col(C_HI), col(f_col), lbp_spec, s0_spec, lv_spec]
        args = (u, u, u, hgrn_lb, s0, lvmap)
        o_shape = jax.ShapeDtypeStruct((T, D_B), F32)
    else:
        in_specs = [col(C_HQ), col(C_HI), col(f_col), col(C_HG), pl.BlockSpec((TM, D_B), lambda g: (g, 0)),
                    lbp_spec, s0_spec, lv_spec, pl.BlockSpec((1, DK_B), lambda g: (0, 0))]
        args = (u, u, u, u, ob, hgrn_lb, s0, lvmap, hgrn_norm_l.reshape(1, DK_B))
        o_shape = jax.ShapeDtypeStruct((T, D_B), BF16)
    return pl.pallas_call(
        functools.partial(_hgrn_kernel, reverse, l),
        grid=(NT,),
        in_specs=in_specs,
        out_specs=[pl.BlockSpec((TM, D_B), lambda g: (tile(g), 0)), ns_spec],
        out_shape=[o_shape, ns_shape],
        scratch_shapes=[pltpu.VMEM((H_B, DK_B, DK_B), F32)],
        compiler_params=_cparams(("arbitrary",)),
        name="hgrn_bwd" if reverse else "hgrn_fwd",
    )(*args)


def _qk_kernel(aq_ref, ak_ref, av_ref, qn_ref, kn_ref, cos_ref, sin_ref, bd_ref, qt_ref, kb_ref, vt_ref, kf_ref):
    bd = bd_ref[...]
    lane = lax.broadcasted_iota(jnp.int32, (TM, 128), 1)
    even = (lane & 1) == 0
    cos = cos_ref[...]
    sin = sin_ref[...]

    def seg_norm(x, g):
        hi, lo = _split_bf16(x * x)
        ss = _dot(hi, bd) + _dot(lo, bd)
        return x * lax.rsqrt(ss * (1.0 / DH_C) + 1e-6) * g

    def rope(x):
        parts = []
        for s in range(D_C // 128):
            xs = x[:, s * 128:(s + 1) * 128]
            sw = jnp.where(even, pltpu.roll(xs, 127, 1), pltpu.roll(xs, 1, 1))
            parts.append(xs * cos + sw * sin)
        return jnp.concatenate(parts, axis=1)

    q = seg_norm(aq_ref[...], qn_ref[...])
    k = seg_norm(ak_ref[...], kn_ref[...])
    kf_ref[...] = k
    qt_ref[...] = (rope(q) * (DH_C ** -0.5 * math.log2(math.e))).T.astype(BF16)
    kb_ref[...] = rope(k).astype(BF16)
    vt_ref[0] = av_ref[...].T.astype(BF16)


def _rope_tables():
    rows = DEC_SEQ // GRID_W
    row = jnp.broadcast_to(jnp.arange(rows)[:, None], (rows, GRID_W)).reshape(-1).astype(F32)
    colp = jnp.broadcast_to(jnp.arange(GRID_W)[None, :], (rows, GRID_W)).reshape(-1).astype(F32)
    half = DH_C // 2
    inv = ROPE_BASE ** (-jnp.arange(0, half, 2, dtype=F32) / half)
    ang = jnp.concatenate([row[:, None] * inv, colp[:, None] * inv], axis=-1)
    cos = jnp.repeat(jnp.cos(ang), 2, axis=-1)
    sin = jnp.repeat(jnp.sin(ang), 2, axis=-1) * jnp.tile(jnp.array([-1.0, 1.0], F32), half)
    cos = jnp.concatenate([jnp.ones((TM, DH_C), F32), cos], axis=0)
    sin = jnp.concatenate([jnp.zeros((TM, DH_C), F32), sin], axis=0)
    return jnp.tile(cos, (1, 2)), jnp.tile(sin, (1, 2))


def _qk_prep(u, q_norm_l, k_norm_l, cos_t, sin_t, bd):
    col = lambda c: pl.BlockSpec((TM, CB), lambda i: (i, c))
    vec = pl.BlockSpec((1, D_C), lambda i: (0, 0))
    tab = pl.BlockSpec((TM, 128), lambda i: (jnp.where(i < CTX_TILES, 0, 1 + i % SMP_TILES), 0))
    out = pl.BlockSpec((TM, D_C), lambda i: (i, 0))
    return pl.pallas_call(
        _qk_kernel,
        grid=(NT,),
        in_specs=[col(C_AQ), col(C_AK), col(C_AV), vec, vec, tab, tab,
                  pl.BlockSpec((D_C, D_C), lambda i: (0, 0))],
        out_specs=[pl.BlockSpec((D_C, TM), lambda i: (0, i)), out,
                   pl.BlockSpec((1, D_C, TM), lambda i: (i, 0, 0)), out],
        out_shape=[jax.ShapeDtypeStruct((D_C, T), BF16), jax.ShapeDtypeStruct((T, D_C), BF16),
                   jax.ShapeDtypeStruct((NT, D_C, TM), BF16), jax.ShapeDtypeStruct((T, D_C), F32)],
        compiler_params=_cparams(("arbitrary",)),
        name="qk_prep",
    )(u, u, u, jnp.tile(q_norm_l, D_C // DH_C).reshape(1, D_C), jnp.tile(k_norm_l, D_C // DH_C).reshape(1, D_C),
      cos_t, sin_t, bd)


def _attn_kernel(lam_init, n_new, has_cache, *refs):
    if has_cache:
        qt_ref, kc_ref, vc_ref, kn_ref, vn_ref, lam_ref, sub_ref, o_ref = refs[:8]
    else:
        qt_ref, kn_ref, vn_ref, lam_ref, sub_ref, o_ref = refs[:6]
    qm_ref, s_ref, p_ref, m_ref, l_ref, a_ref, acc_ref = refs[-7:]
    tq = qt_ref.shape[1]
    qt = qt_ref[...]
    sub_i = lax.broadcasted_iota(jnp.int32, qt.shape, 0)
    qm_ref[0] = jnp.where(sub_i < DH_C, qt, jnp.zeros_like(qt))
    qm_ref[1] = jnp.where(sub_i >= DH_C, qt, jnp.zeros_like(qt))
    m_ref[...] = jnp.full(m_ref.shape, -1e30, F32)
    l_ref[...] = jnp.zeros_like(l_ref)
    acc_ref[...] = jnp.zeros_like(acc_ref)

    def scores(k, buf):
        for c in range(2):
            s_ref[buf, c] = _dot(k, qm_ref[c])

    def softmax_update(buf):
        for c in range(2):
            for cb in range(tq // 128):
                cols = slice(cb * 128, (cb + 1) * 128)
                m_old = m_ref[c, :, cols]
                m_new = jnp.maximum(m_old, jnp.max(s_ref[buf, c, :, cols], axis=0, keepdims=True))
                alpha = jnp.exp2(m_old - m_new)
                p = jnp.exp2(s_ref[buf, c, :, cols] - m_new)
                l_ref[c, :, cols] = alpha * l_ref[c, :, cols] + jnp.sum(p, axis=0, keepdims=True)
                m_ref[c, :, cols] = m_new
                a_ref[c, :, cols] = alpha
                p_ref[buf, c, :, cols] = p.astype(BF16)

    def accumulate(vt, buf):
        for c in range(2):
            acc_ref[c] = a_ref[c] * acc_ref[c] + _dot(vt, p_ref[buf, c])

    new_k = lambda j: kn_ref[pl.ds(pl.multiple_of(j * TM, TM), TM), :]
    if n_new == 1:
        scores(kn_ref[...], 0)
        softmax_update(0)
        accumulate(vn_ref[0], 0)
    else:
        scores(new_k(0), 0)

        def body(i, carry):
            scores(new_k(2 * i + 1), 1)
            softmax_update(0)
            accumulate(vn_ref[2 * i], 0)
            scores(new_k(2 * i + 2), 0)
            softmax_update(1)
            accumulate(vn_ref[2 * i + 1], 1)
            return carry
        lax.fori_loop(0, n_new // 2 - 1, body, 0)
        scores(kn_ref[(n_new - 1) * TM:, :], 1)
        softmax_update(0)
        accumulate(vn_ref[n_new - 2], 0)
        scores(kc_ref[...], 0)
        softmax_update(1)
        accumulate(vn_ref[n_new - 1], 1)
        softmax_update(0)
        accumulate(vc_ref[0], 0)

    lq = lam_ref[...]
    lam = (jnp.exp(jnp.sum(lq[0:1] * lq[1:2], axis=-1, keepdims=True))
           - jnp.exp(jnp.sum(lq[2:3] * lq[3:4], axis=-1, keepdims=True)) + lam_init)
    for cb in range(tq // 128):
        cols = slice(cb * 128, (cb + 1) * 128)
        o = acc_ref[0, :, cols] / l_ref[0, :, cols] - lam * (acc_ref[1, :, cols] / l_ref[1, :, cols])
        y = o * lax.rsqrt(jnp.mean(o * o, axis=0, keepdims=True) + 1e-6)
        o_ref[cols, :] = (y.T * sub_ref[...] * (1.0 - lam_init)).astype(BF16)


def _attn_scratch(tq):
    return [pltpu.VMEM((2, 128, tq), BF16), pltpu.VMEM((2, 2, TM, tq), F32), pltpu.VMEM((2, 2, TM, tq), BF16),
            pltpu.VMEM((2, 1, tq), F32), pltpu.VMEM((2, 1, tq), F32), pltpu.VMEM((2, 1, tq), F32),
            pltpu.VMEM((2, 128, tq), F32)]


def _attention(l, qt, kb, vt, ck, cvt, lambda_qk_l, subln_l):
    lam_init = 0.8 - 0.6 * math.exp(-0.3 * l)
    sub = subln_l.reshape(1, 2 * DH_C)
    o_ctx = pl.pallas_call(
        functools.partial(_attn_kernel, lam_init, 1, False),
        grid=(BATCH, H_C),
        in_specs=[pl.BlockSpec((128, SEQ), lambda b, h: (h, b)),
                  pl.BlockSpec((SEQ, 128), lambda b, h: (b, h)),
                  pl.BlockSpec((1, 128, TM), lambda b, h: (b, h, 0)),
                  pl.BlockSpec((4, DH_C), lambda b, h: (0, 0)),
                  pl.BlockSpec((1, 2 * DH_C), lambda b, h: (0, 0))],
        out_specs=pl.BlockSpec((SEQ, 128), lambda b, h: (b, h)),
        out_shape=jax.ShapeDtypeStruct((BATCH * SEQ, D_C), BF16),
        scratch_shapes=_attn_scratch(SEQ),
        compiler_params=_cparams(("arbitrary", "arbitrary")),
        name="attn_ctx",
    )(qt, kb, vt, lambda_qk_l, sub)

    nq = DEC_SEQ // TQ
    base_q = BATCH * SEQ // TQ
    base_kv = BATCH * SEQ // DEC_SEQ
    o_smp = pl.pallas_call(
        functools.partial(_attn_kernel, lam_init, SMP_TILES, True),
        grid=(DEC_BATCH, H_C, nq),
        in_specs=[pl.BlockSpec((128, TQ), lambda b, h, j: (h, base_q + b * nq + j)),
                  pl.BlockSpec((PAST, 128), lambda b, h, j: (b, h)),
                  pl.BlockSpec((1, 128, PAST), lambda b, h, j: (b, h, 0)),
                  pl.BlockSpec((DEC_SEQ, 128), lambda b, h, j: (base_kv + b, h)),
                  pl.BlockSpec((SMP_TILES, 128, TM), lambda b, h, j: (base_kv + b, h, 0)),
                  pl.BlockSpec((4, DH_C), lambda b, h, j: (0, 0)),
                  pl.BlockSpec((1, 2 * DH_C), lambda b, h, j: (0, 0))],
        out_specs=pl.BlockSpec((TQ, 128), lambda b, h, j: (b * nq + j, h)),
        out_shape=jax.ShapeDtypeStruct((DEC_BATCH * DEC_SEQ, D_C), BF16),
        scratch_shapes=_attn_scratch(TQ),
        compiler_params=_cparams(("arbitrary", "arbitrary", "arbitrary")),
        name="attn_smp",
    )(qt, ck, cvt, kb, vt, lambda_qk_l, sub)
    return jnp.concatenate([o_ctx, o_smp], axis=0)


def _merge_kernel(a_ref, b_ref, c_ref, ga_ref, gb_ref, gc_ref, x_ref, g1_ref, wb_ref, wo_ref, o_ref):
    ya = _dot(a_ref[...], wb_ref[0:D_A, :])
    yb = _dot(b_ref[...], wb_ref[D_A:D_A + D_B, :])
    yc = _dot(c_ref[...], wb_ref[D_A + D_B:, :])
    m = _sigmoid(ga_ref[...]) * ya + _sigmoid(gb_ref[...]) * yb + _sigmoid(gc_ref[...]) * yc
    o_ref[...] = x_ref[...] + g1_ref[0] * _dot(m.astype(BF16), wo_ref[...])


def _merge(ya, yb, yc, u, x, modt, w_branch_l, w_out_l, l):
    row = lambda w: pl.BlockSpec((TM, w), lambda i: (i, 0))
    gate0 = C_GT * CB // D
    gate = lambda k: pl.BlockSpec((TM, D), lambda i: (i, gate0 + k))
    return pl.pallas_call(
        _merge_kernel,
        grid=(NT,),
        in_specs=[row(D_A), row(D_B), row(D_C), gate(0), gate(1), gate(2),
                  row(D), _mod_spec(l, 2),
                  pl.BlockSpec((D_A + D_B + D_C, D), lambda i: (0, 0)),
                  pl.BlockSpec((D, D), lambda i: (0, 0))],
        out_specs=row(D),
        out_shape=jax.ShapeDtypeStruct((T, D), F32),
        compiler_params=_cparams(("arbitrary",)),
        name="merge_out",
    )(ya, yb, yc, u, u, u, x, modt, w_branch_l, w_out_l)


def _ffn_kernel(x_ref, g_ref, sc_ref, sh_ref, g2_ref, wi_ref, wo_ref, o_ref):
    x = x_ref[...]
    h = _rms_mod(x, g_ref[...], sc_ref[0], sh_ref[0]).astype(BF16)
    gu = _dot(h, wi_ref[...])
    ff = _silu(gu[:, :D_FF]) * gu[:, D_FF:]
    o_ref[...] = x + g2_ref[0] * _dot(ff.astype(BF16), wo_ref[...])


def _ffn(x, modt, norm2_l, w_ffn_in_l, w_ffn_out_l, l):
    return pl.pallas_call(
        _ffn_kernel,
        grid=(NT,),
        in_specs=[pl.BlockSpec((TM, D), lambda i: (i, 0)),
                  pl.BlockSpec((1, D), lambda i: (0, 0)),
                  _mod_spec(l, 4), _mod_spec(l, 3), _mod_spec(l, 5),
                  pl.BlockSpec((D, 2 * D_FF), lambda i: (0, 0), pipeline_mode=pl.Buffered(1)),
                  pl.BlockSpec((D_FF, D), lambda i: (0, 0), pipeline_mode=pl.Buffered(1))],
        out_specs=pl.BlockSpec((TM, D), lambda i: (i, 0)),
        out_shape=jax.ShapeDtypeStruct((T, D), F32),
        compiler_params=_cparams(("arbitrary",)),
        name="ffn",
    )(x, norm2_l.reshape(1, D), modt, modt, modt, w_ffn_in_l, w_ffn_out_l)


def _layer(l, x, modt, cache_k, cache_v, state_hgrn, tabs, p):
    (norm1, norm2, w_in, conv_w, conv_b, conv_ln_g, conv_ln_b, hgrn_lb, hgrn_norm, q_norm, k_norm, lambda_qk,
     subln, w_branch, w_out, w_ffn_in, w_ffn_out) = p
    cos_t, sin_t, bd, lv_f, lv_b = tabs
    u = _in_proj(x, modt, norm1[l], w_in[l].astype(BF16), l)
    ya = _conv_branch(u, conv_w[l], conv_b[l], conv_ln_g[l], conv_ln_b[l])
    lbp = hgrn_lb.reshape(DEPTH * 2, D_B)
    ob, ns_b = _hgrn_pass(True, l, u, lbp, state_hgrn[:, l, 1], lv_b)
    yb, ns_f = _hgrn_pass(False, l, u, lbp, state_hgrn[:, l, 0], lv_f, ob=ob, hgrn_norm_l=hgrn_norm[l])
    qt, kb, vt, kf = _qk_prep(u, q_norm[l], k_norm[l], cos_t, sin_t, bd)
    ck = cache_k[:, l].reshape(DEC_BATCH * PAST, D_C).astype(BF16)
    cvt = cache_v[:, l].reshape(DEC_BATCH, PAST, D_C).astype(BF16).transpose(0, 2, 1)
    yc = _attention(l, qt, kb, vt, ck, cvt, lambda_qk[l], subln[l])
    x = _merge(ya, yb, yc, u, x, modt, w_branch[l].astype(BF16), w_out[l].astype(BF16), l)
    x = _ffn(x, modt, norm2[l], w_ffn_in[l].astype(BF16), w_ffn_out[l].astype(BF16), l)
    n_ctx = BATCH * SEQ
    new_k = kf[:n_ctx].reshape(BATCH, SEQ, H_C, 2, DH_C)
    new_v = u[:n_ctx, C_AV * CB:(C_AV + 1) * CB].reshape(BATCH, SEQ, H_C, 2 * DH_C)
    new_s = jnp.stack([ns_f, ns_b], axis=1)
    return x, new_k, new_v, new_s


def kernel(x_prompt, x_sample, cache_k, cache_v, state_hgrn, c, c_ctx, w_mod, b_mod, norm1, norm2, w_in, conv_w,
           conv_b, conv_ln_g, conv_ln_b, hgrn_lb, hgrn_norm, q_norm, k_norm, lambda_qk, subln, w_branch, w_out,
           w_ffn_in, w_ffn_out):
    n_ctx = BATCH * SEQ
    x = jnp.concatenate([x_prompt.reshape(n_ctx, D), x_sample.reshape(DEC_BATCH * DEC_SEQ, D)], axis=0)
    cv8 = jnp.concatenate([c_ctx[None, :], c, jnp.zeros((8 - 1 - DEC_BATCH, D), F32)], axis=0)
    mod = _modulation(cv8, w_mod, b_mod)
    modt = mod.reshape(DEPTH, 8, 6, D).transpose(0, 2, 1, 3).reshape(DEPTH * 6 * 8, 1, D)
    cos_t, sin_t = _rope_tables()
    seg = np.arange(D_C) // DH_C
    bd = jnp.asarray(seg[:, None] == seg[None, :], BF16)
    tabs = (cos_t, sin_t, bd, jnp.asarray(_level_map(False)), jnp.asarray(_level_map(True)))
    p = (norm1, norm2, w_in, conv_w, conv_b, conv_ln_g, conv_ln_b, hgrn_lb, hgrn_norm, q_norm, k_norm, lambda_qk,
         subln, w_branch, w_out, w_ffn_in, w_ffn_out)
    ks, vs, ss = [], [], []
    for l in range(DEPTH):
        x, k_l, v_l, s_l = _layer(l, x, modt, cache_k, cache_v, state_hgrn, tabs, p)
        ks.append(k_l)
        vs.append(v_l)
        ss.append(s_l)
    y_prompt = x[:n_ctx].reshape(BATCH, SEQ, D)
    y_sample = x[n_ctx:].reshape(DEC_BATCH, DEC_SEQ, D)
    return (y_prompt, y_sample, jnp.stack(ks, axis=1), jnp.stack(vs, axis=1), jnp.stack(ss, axis=1))
```

```python
import functools
import math

import numpy as np
import jax
import jax.numpy as jnp
from jax import lax
from jax.experimental import pallas as pl
from jax.experimental.pallas import tpu as pltpu

F32 = jnp.float32
BF16 = jnp.bfloat16

D = 1024
BATCH, SEQ = 32, 256
DEPTH = 2
DEC_BATCH, DEC_SEQ = 2, 4096
PAST = 256
GRID_W = 64
D_A, CONV_K = 512, 31
H_B, DK_B = 4, 128
D_B = 512
H_C, DH_C = 4, 64
D_C = 512
ROPE_BASE = 10000.0
D_FF = 2816
IN_COLS = 8192

TM = 256
CTX_TILES = BATCH * SEQ // TM
SMP_TILES = DEC_SEQ // TM
T = BATCH * SEQ + DEC_BATCH * DEC_SEQ
NT = T // TM
CB = 512
HALO = 16
TQ = 512
VMEM_LIMIT = 56 * 1024 * 1024

C_A, C_AG, C_HQ, C_HI, C_HFF, C_HFB, C_HG, C_AQ, C_AK, C_AV, C_GT = 0, 1, 2, 3, 4, 5, 6, 7, 8, 9, 10


def _cparams(sem):
    return pltpu.CompilerParams(dimension_semantics=sem, vmem_limit_bytes=VMEM_LIMIT)


def _mod_row(i):
    return jnp.where(i < CTX_TILES, 0, 1 + (i - CTX_TILES) // SMP_TILES)


def _sigmoid(x):
    return 0.5 * jnp.tanh(0.5 * x) + 0.5


def _sigmoid_small_accurate(x):
    return jnp.exp(-jnp.log(1.0 + jnp.exp(-x)))


def _silu(x):
    return x * _sigmoid(x)


def _dot(a, b):
    return jnp.dot(a, b, preferred_element_type=F32)


def _dot_nt(a, b):
    return lax.dot_general(a, b, (((1,), (1,)), ((), ())), preferred_element_type=F32)


def _dot_tn(a, b):
    return lax.dot_general(a, b, (((0,), (0,)), ((), ())), preferred_element_type=F32)


def _split_bf16(x):
    hi = x.astype(BF16)
    lo = (x - hi.astype(F32)).astype(BF16)
    return hi, lo


MOD_TN = 1536


def _mod_kernel(cv_ref, w_ref, b_ref, o_ref):
    a = _silu(cv_ref[...])
    a_hi, a_lo = _split_bf16(a)
    w_hi, w_lo = _split_bf16(w_ref[0])
    acc = _dot(a_hi, w_hi) + _dot(a_lo, w_hi) + _dot(a_hi, w_lo)
    o_ref[0] = acc + b_ref[0]


def _modulation(cv8, w_mod, b_mod):
    return pl.pallas_call(
        _mod_kernel,
        grid=(DEPTH, 6 * D // MOD_TN),
        in_specs=[
            pl.BlockSpec((8, D), lambda l, j: (0, 0)),
            pl.BlockSpec((1, D, MOD_TN), lambda l, j: (l, 0, j)),
            pl.BlockSpec((1, 1, MOD_TN), lambda l, j: (l, 0, j)),
        ],
        out_specs=pl.BlockSpec((1, 8, MOD_TN), lambda l, j: (l, 0, j)),
        out_shape=jax.ShapeDtypeStruct((DEPTH, 8, 6 * D), F32),
        compiler_params=_cparams(("arbitrary", "arbitrary")),
        name="adaln_mod",
    )(cv8, w_mod, b_mod.reshape(DEPTH, 1, 6 * D))


def _mod_spec(l, which):
    return pl.BlockSpec((1, 1, D), lambda i: ((l * 6 + which) * 8 + _mod_row(i), 0, 0))


def _rms_mod(x, g, sc, sh):
    y = x * lax.rsqrt(jnp.mean(x * x, axis=-1, keepdims=True) + 1e-6) * g
    return y * (1.0 + sc) + sh


def _x_specs(x):
    if isinstance(x, tuple):
        return [pl.BlockSpec((TM, D), lambda i: (jnp.minimum(i, CTX_TILES - 1), 0)),
                pl.BlockSpec((TM, D), lambda i: (jnp.maximum(i - CTX_TILES, 0), 0))], list(x)
    return [pl.BlockSpec((TM, D), lambda i: (i, 0))], [x]


def _x_tile(x_refs):
    if len(x_refs) == 2:
        return jnp.where(pl.program_id(0) < CTX_TILES, x_refs[0][...], x_refs[1][...])
    return x_refs[0][...]


def _in_kernel(nx, *refs):
    g_ref, sc_ref, sh_ref, w_ref, o_ref = refs[nx:]
    h = _rms_mod(_x_tile(refs[:nx]), g_ref[...], sc_ref[0], sh_ref[0])
    o_ref[...] = _dot(h.astype(BF16), w_ref[...])


def _in_proj(x, modt, norm1_l, w_in_l, l):
    x_specs, xs = _x_specs(x)
    return pl.pallas_call(
        functools.partial(_in_kernel, len(xs)),
        grid=(NT,),
        in_specs=x_specs + [
            pl.BlockSpec((1, D), lambda i: (0, 0)),
            _mod_spec(l, 1),
            _mod_spec(l, 0),
            pl.BlockSpec((D, IN_COLS), lambda i: (0, 0), pipeline_mode=pl.Buffered(1)),
        ],
        out_specs=pl.BlockSpec((TM, IN_COLS), lambda i: (i, 0)),
        out_shape=jax.ShapeDtypeStruct((T, IN_COLS), F32),
        compiler_params=_cparams(("arbitrary",)),
        name="in_proj",
    )(*xs, norm1_l.reshape(1, D), modt, modt, w_in_l)


CONV_RB = 32


def _conv_kernel(ac_ref, gc_ref, ap_ref, gp_ref, an_ref, gn_ref, w_ref, b_ref, lg_ref, lb_ref, o_ref, sh_ref,
                 cv_ref):
    i = pl.program_id(0)
    in_smp = i >= CTX_TILES
    pos = i % SMP_TILES
    has_prev = jnp.logical_and(in_smp, pos != 0)
    has_next = jnp.logical_and(in_smp, pos != SMP_TILES - 1)
    n = TM + 2 * HALO
    sh_ref[0, HALO:HALO + TM, :] = ac_ref[...] * _sigmoid(gc_ref[...])
    sh_ref[0, 0:HALO, :] = jnp.where(has_prev, ap_ref[...] * _sigmoid(gp_ref[...]), 0.0)
    sh_ref[0, HALO + TM:, :] = jnp.where(has_next, an_ref[...] * _sigmoid(gn_ref[...]), 0.0)
    for r in range(1, 8):
        sh_ref[r, 0:n - 8, :] = sh_ref[0, r:r + n - 8, :]
    off = HALO - CONV_K // 2

    def body(rb, carry):
        base = pl.multiple_of(rb * CONV_RB, CONV_RB)
        acc = jnp.zeros((CONV_RB, D_A), F32)
        for k in range(CONV_K):
            r = (off + k) % 8
            acc = acc + sh_ref[r, pl.ds(base + (off + k - r), CONV_RB), :] * w_ref[k:k + 1, :]
        cv_ref[pl.ds(base, CONV_RB), :] = acc
        return carry
    lax.fori_loop(0, TM // CONV_RB, body, 0)
    acc = cv_ref[...] + b_ref[...]
    mu = jnp.mean(acc, axis=-1, keepdims=True)
    xc = acc - mu
    y = xc * lax.rsqrt(jnp.mean(xc * xc, axis=-1, keepdims=True) + 1e-5)
    y = y * lg_ref[...] + lb_ref[...]
    o_ref[...] = _silu(y).astype(BF16)


def _conv_branch(u, conv_w_l, conv_b_l, ln_g_l, ln_b_l):
    r = TM // HALO
    last = T // HALO - 1
    cur = lambda c: pl.BlockSpec((TM, CB), lambda i: (i, c))
    prev = lambda c: pl.BlockSpec((HALO, CB), lambda i: (jnp.maximum(i * r - 1, 0), c))
    nxt = lambda c: pl.BlockSpec((HALO, CB), lambda i: (jnp.minimum((i + 1) * r, last), c))
    vec = pl.BlockSpec((1, D_A), lambda i: (0, 0))
    return pl.pallas_call(
        _conv_kernel,
        grid=(NT,),
        in_specs=[cur(C_A), cur(C_AG), prev(C_A), prev(C_AG), nxt(C_A), nxt(C_AG),
                  pl.BlockSpec((CONV_K, D_A), lambda i: (0, 0)), vec, vec, vec],
        out_specs=pl.BlockSpec((TM, D_A), lambda i: (i, 0)),
        out_shape=jax.ShapeDtypeStruct((T, D_A), BF16),
        scratch_shapes=[pltpu.VMEM((8, TM + 2 * HALO, D_A), F32), pltpu.VMEM((TM, D_A), F32)],
        compiler_params=_cparams(("arbitrary",)),
        name="conv_branch",
    )(u, u, u, u, u, u, conv_w_l, conv_b_l.reshape(1, D_A), ln_g_l.reshape(1, D_A), ln_b_l.reshape(1, D_A))


N_LEVELS = 8


def _level_map(reverse):
    t = np.arange(TM)[:, None]
    s = np.arange(TM)[None, :]
    x = t ^ s
    lv = np.zeros((TM, TM), np.int32)
    for bit in range(N_LEVELS):
        lv[(x >> bit) == 1] = bit + 1
    ok = (t < s) if reverse else (t > s)
    return np.where(ok, lv, 0).astype(np.int32)


def _shift_rows(x, k):
    return pltpu.roll(x, k % TM, 0)


def _tri_map(reverse):
    t = np.arange(TM)[:, None]
    s = np.arange(TM)[None, :]
    return (s >= t) if reverse else (s <= t)


def _cumsum_rows(x, tri):
    hi = x.astype(BF16)
    r1 = x - hi.astype(F32)
    mid = r1.astype(BF16)
    lo = (r1 - mid.astype(F32)).astype(BF16)
    n = x.shape[1]
    s = _dot(tri, jnp.concatenate([hi, mid, lo], axis=1))
    return s[:, :n] + s[:, n:2 * n] + s[:, 2 * n:]


def _anchors(cum, row, reverse):
    out = []
    z = cum
    for bit in range(N_LEVELS):
        h = 1 << bit
        if 2 * h >= 16:
            c3 = cum.reshape(TM // (2 * h), 2 * h, DK_B)
            p = h if reverse else h - 1
            out.append(jnp.broadcast_to(c3[:, p:p + 1, :], c3.shape).reshape(TM, DK_B))
            continue
        hi = (row & h) != 0
        if reverse:
            out.append(jnp.where(hi, z, _shift_rows(z, -h)))
            z = jnp.where(hi, _shift_rows(z, h), z)
        else:
            out.append(jnp.where(hi, _shift_rows(z, h), z))
            z = jnp.where(hi, z, _shift_rows(z, -h))
    return out


LOW_BITS = 5
SAFE_RANGE = 60.0


def _low_map(reverse):
    t = np.arange(TM)[:, None]
    s = np.arange(TM)[None, :]
    same = (t >> LOW_BITS) == (s >> LOW_BITS)
    ok = (t <= s) if reverse else (t >= s)
    return (same & ok).astype(np.int32)


def _block_anchor(cum, reverse):
    n = 1 << LOW_BITS
    c3 = cum.reshape(TM // n, n, DK_B)
    p = n - 1 if reverse else 0
    return jnp.broadcast_to(c3[:, p:p + 1, :], c3.shape).reshape(TM, DK_B)


def _hgrn_kernel(reverse, l, *refs):
    if reverse:
        (q_ref, v_ref, f_ref, lbp_ref, s0_ref, lv_ref, lo_ref, tri_ref, _, o_ref, ns_ref) = refs[:11]
    else:
        (q_ref, v_ref, f_ref, g_ref, ob_ref, lbp_ref, s0_ref, lv_ref, lo_ref, tri_ref, nrm_ref, _, o_ref,
         ns_ref) = refs[:14]
    st_ref, qs_ref, ks_ref, cs_ref, os_ref, as_ref = refs[-6:]
    g = pl.program_id(0)
    i = NT - 1 - g if reverse else g
    is_ctx = i < CTX_TILES
    pos = i % SMP_TILES
    seq_start = jnp.logical_and(i >= CTX_TILES, pos == (SMP_TILES - 1 if reverse else 0))

    @pl.when(is_ctx)
    def _():
        st_ref[...] = jnp.zeros_like(st_ref)

    @pl.when(seq_start)
    def _():
        for h in range(H_B):
            st_ref[h] = s0_ref[0, h].T

    d = 1 if reverse else 0
    rows = [lbp_ref[k * 2 + d:k * 2 + d + 1, :] for k in range(DEPTH)]
    mx = functools.reduce(jnp.maximum, rows)
    ex = [jnp.exp(r - mx) for r in rows]
    den = functools.reduce(jnp.add, ex)
    lb_all = jnp.zeros_like(mx)
    for k in range(1, l + 1):
        lb_all = lb_all + ex[k] / den

    row = lax.broadcasted_iota(jnp.int32, (TM, DK_B), 0)
    spread = jnp.zeros((1, DK_B), F32)
    for h in range(H_B):
        sl = slice(h * DK_B, (h + 1) * DK_B)
        lb = lb_all[:, sl]
        fg = lb + (1.0 - lb) * _sigmoid_small_accurate(f_ref[:, sl])
        cum = _cumsum_rows(jnp.log(fg), tri_ref[...])
        ks_ref[h] = 1.0 - fg
        qs_ref[h] = _silu(q_ref[:, sl])
        cs_ref[h] = cum
        spread = jnp.maximum(spread, jnp.max(_block_anchor(cum, reverse) - cum, axis=0, keepdims=True))
    single_anchor_ok = jnp.max(spread) < SAFE_RANGE

    half = TM // 2
    halves = (slice(0, half), slice(half, TM))
    late, early = (halves[0], halves[1]) if reverse else (halves[1], halves[0])

    lv = lv_ref[0:half, 0:half]

    def operands(h, fq, fk):
        return (qs_ref[h] * fq).astype(BF16), (ks_ref[h] * fk).astype(BF16)

    for h in range(H_B):
        sl = slice(h * DK_B, (h + 1) * DK_B)
        q, kg, cum = qs_ref[h], ks_ref[h], cs_ref[h]
        vb = v_ref[:, sl].astype(BF16)
        tot = cum[0:1, :] if reverse else cum[TM - 1:TM, :]
        st = st_ref[h]
        o = _dot_nt((q * jnp.exp(cum)).astype(BF16), st.astype(BF16))
        anchors = _anchors(cum, row, reverse)
        fac = jnp.exp(-jnp.abs(cum - anchors[N_LEVELS - 1]))
        qf, kf = operands(h, fac, fac)
        o_top = _dot(_dot_nt(qf[late], kf[early]).astype(BF16), vb[early])
        zero = jnp.zeros_like(o_top)
        os_ref[h] = o + jnp.concatenate([o_top, zero] if reverse else [zero, o_top], axis=0)
        a = [jnp.zeros((half, half), F32) for _ in halves]
        for bit in range(LOW_BITS, N_LEVELS - 1):
            fac = jnp.exp(-jnp.abs(cum - anchors[bit]))
            qf, kf = operands(h, fac, fac)
            a = [jnp.where(lv == bit + 1, _dot_nt(qf[r], kf[r]), a[b]) for b, r in enumerate(halves)]
        for b in range(2):
            as_ref[h, b] = a[b]
        kbar = (kg * jnp.exp(tot - cum)).astype(BF16)
        st_ref[h] = jnp.exp(tot) * st + _dot_tn(vb, kbar)

    def heads(single_anchor):
        for h in range(H_B):
            sl = slice(h * DK_B, (h + 1) * DK_B)
            cum = cs_ref[h]
            v = v_ref[:, sl]
            vb = v.astype(BF16)
            o = os_ref[h]
            a = [as_ref[h, b] for b in range(2)]
            if single_anchor:
                e = _block_anchor(cum, reverse)
                qf, kf = operands(h, jnp.exp(cum - e), jnp.exp(e - cum))
                lo = lo_ref[0:half, 0:half] != 0
                a = [jnp.where(lo, _dot_nt(qf[r], kf[r]), a[b]) for b, r in enumerate(halves)]
            else:
                o = o + jnp.sum(qs_ref[h] * ks_ref[h], axis=-1, keepdims=True) * v
                anchors = _anchors(cum, row, reverse)
                for bit in range(LOW_BITS):
                    fac = jnp.exp(-jnp.abs(cum - anchors[bit]))
                    qf, kf = operands(h, fac, fac)
                    a = [jnp.where(lv == bit + 1, _dot_nt(qf[r], kf[r]), a[b]) for b, r in enumerate(halves)]
            o = o + jnp.concatenate([_dot(a[b].astype(BF16), vb[r]) for b, r in enumerate(halves)], axis=0)
            if reverse:
                o_ref[:, sl] = o
            else:
                o = o + ob_ref[:, sl]
                y = o * lax.rsqrt(jnp.mean(o * o, axis=-1, keepdims=True) + 1e-6) * nrm_ref[...]
                o_ref[:, sl] = (y * _silu(g_ref[:, sl])).astype(BF16)

    @pl.when(single_anchor_ok)
    def _():
        heads(True)

    @pl.when(jnp.logical_not(single_anchor_ok))
    def _():
        heads(False)

    @pl.when(is_ctx)
    def _():
        for h in range(H_B):
            ns_ref[0, 0, 0, h] = st_ref[h].T


def _hgrn_pass(reverse, l, u, hgrn_lb, s0, maps, new_state, ob=None, hgrn_norm_l=None):
    tile = (lambda g: NT - 1 - g) if reverse else (lambda g: g)
    col = lambda c: pl.BlockSpec((TM, CB), lambda g: (tile(g), c))
    s0_spec = pl.BlockSpec((1, H_B, DK_B, DK_B),
                           lambda g: (jnp.clip((tile(g) - CTX_TILES) // SMP_TILES, 0, DEC_BATCH - 1), 0, 0, 0))
    lbp_spec = pl.BlockSpec((DEPTH * 2, D_B), lambda g: (0, 0))
    map_spec = pl.BlockSpec((TM, TM), lambda g: (0, 0))
    d = 1 if reverse else 0
    ns_spec = pl.BlockSpec((1, 1, 1, H_B, DK_B, DK_B),
                           lambda g: (jnp.minimum(tile(g), CTX_TILES - 1), l, d, 0, 0, 0))
    any_spec = pl.BlockSpec(memory_space=pl.ANY)
    f_col = C_HFB if reverse else C_HFF
    if reverse:
        in_specs = [col(C_HQ), col(C_HI), col(f_col), lbp_spec, s0_spec, map_spec, map_spec, map_spec, any_spec]
        args = (u, u, u, hgrn_lb, s0, *maps, new_state)
        o_shape = jax.ShapeDtypeStruct((T, D_B), F32)
    else:
        in_specs = [col(C_HQ), col(C_HI), col(f_col), col(C_HG), pl.BlockSpec((TM, D_B), lambda g: (g, 0)),
                    lbp_spec, s0_spec, map_spec, map_spec, map_spec, pl.BlockSpec((1, DK_B), lambda g: (0, 0)),
                    any_spec]
        args = (u, u, u, u, ob, hgrn_lb, s0, *maps, hgrn_norm_l.reshape(1, DK_B), new_state)
        o_shape = jax.ShapeDtypeStruct((T, D_B), BF16)
    head_tile = pltpu.VMEM((H_B, TM, DK_B), F32)
    return pl.pallas_call(
        functools.partial(_hgrn_kernel, reverse, l),
        grid=(NT,),
        in_specs=in_specs,
        out_specs=[pl.BlockSpec((TM, D_B), lambda g: (tile(g), 0)), ns_spec],
        out_shape=[o_shape, jax.ShapeDtypeStruct(new_state.shape, F32)],
        input_output_aliases={len(args) - 1: 1},
        scratch_shapes=[pltpu.VMEM((H_B, DK_B, DK_B), F32), head_tile, head_tile, head_tile, head_tile,
                        pltpu.VMEM((H_B, 2, TM // 2, TM // 2), F32)],
        compiler_params=_cparams(("arbitrary",)),
        name="hgrn_bwd" if reverse else "hgrn_fwd",
    )(*args)


def _qk_kernel(aq_ref, ak_ref, av_ref, qn_ref, kn_ref, cos_ref, sin_ref, bd_ref, _k, _v, qt_ref, kb_ref, vt_ref,
               nk_ref, nv_ref):
    bd = bd_ref[...]
    lane = lax.broadcasted_iota(jnp.int32, (TM, 128), 1)
    even = (lane & 1) == 0
    cos = cos_ref[...]
    sin = sin_ref[...]

    def seg_norm(x, g):
        hi, lo = _split_bf16(x * x)
        ss = _dot(hi, bd) + _dot(lo, bd)
        return x * lax.rsqrt(ss * (1.0 / DH_C) + 1e-6) * g

    def rope(x):
        parts = []
        for s in range(D_C // 128):
            xs = x[:, s * 128:(s + 1) * 128]
            sw = jnp.where(even, pltpu.roll(xs, 127, 1), pltpu.roll(xs, 1, 1))
            parts.append(xs * cos + sw * sin)
        return jnp.concatenate(parts, axis=1)

    q = seg_norm(aq_ref[...], qn_ref[...])
    k = seg_norm(ak_ref[...], kn_ref[...])
    @pl.when(pl.program_id(0) < CTX_TILES)
    def _():
        nk_ref[0, 0] = k
        nv_ref[0, 0] = av_ref[...]

    qt_ref[...] = (rope(q) * (DH_C ** -0.5 * math.log2(math.e))).T.astype(BF16)
    kb_ref[...] = rope(k).astype(BF16)
    vt_ref[0] = av_ref[...].T.astype(BF16)


def _rope_tables():
    rows = DEC_SEQ // GRID_W
    row = jnp.broadcast_to(jnp.arange(rows)[:, None], (rows, GRID_W)).reshape(-1).astype(F32)
    colp = jnp.broadcast_to(jnp.arange(GRID_W)[None, :], (rows, GRID_W)).reshape(-1).astype(F32)
    half = DH_C // 2
    inv = ROPE_BASE ** (-jnp.arange(0, half, 2, dtype=F32) / half)
    ang = jnp.concatenate([row[:, None] * inv, colp[:, None] * inv], axis=-1)
    cos = jnp.repeat(jnp.cos(ang), 2, axis=-1)
    sin = jnp.repeat(jnp.sin(ang), 2, axis=-1) * jnp.tile(jnp.array([-1.0, 1.0], F32), half)
    cos = jnp.concatenate([jnp.ones((TM, DH_C), F32), cos], axis=0)
    sin = jnp.concatenate([jnp.zeros((TM, DH_C), F32), sin], axis=0)
    return jnp.tile(cos, (1, 2)), jnp.tile(sin, (1, 2))


def _qk_prep(l, u, q_norm_l, k_norm_l, cos_t, sin_t, bd, new_k, new_v):
    col = lambda c: pl.BlockSpec((TM, CB), lambda i: (i, c))
    vec = pl.BlockSpec((1, D_C), lambda i: (0, 0))
    tab = pl.BlockSpec((TM, 128), lambda i: (jnp.where(i < CTX_TILES, 0, 1 + i % SMP_TILES), 0))
    any_spec = pl.BlockSpec(memory_space=pl.ANY)
    cache = pl.BlockSpec((1, 1, SEQ, D_C), lambda i: (jnp.minimum(i, CTX_TILES - 1), l, 0, 0))
    return pl.pallas_call(
        _qk_kernel,
        grid=(NT,),
        in_specs=[col(C_AQ), col(C_AK), col(C_AV), vec, vec, tab, tab,
                  pl.BlockSpec((D_C, D_C), lambda i: (0, 0)), any_spec, any_spec],
        out_specs=[pl.BlockSpec((D_C, TM), lambda i: (0, i)), pl.BlockSpec((TM, D_C), lambda i: (i, 0)),
                   pl.BlockSpec((1, D_C, TM), lambda i: (i, 0, 0)), cache, cache],
        out_shape=[jax.ShapeDtypeStruct((D_C, T), BF16), jax.ShapeDtypeStruct((T, D_C), BF16),
                   jax.ShapeDtypeStruct((NT, D_C, TM), BF16),
                   jax.ShapeDtypeStruct(new_k.shape, F32), jax.ShapeDtypeStruct(new_v.shape, F32)],
        input_output_aliases={8: 3, 9: 4},
        compiler_params=_cparams(("arbitrary",)),
        name="qk_prep",
    )(u, u, u, jnp.tile(q_norm_l, D_C // DH_C).reshape(1, D_C), jnp.tile(k_norm_l, D_C // DH_C).reshape(1, D_C),
      cos_t, sin_t, bd, new_k, new_v)


def _attn_kernel(lam_init, n_new, has_cache, *refs):
    if has_cache:
        qt_ref, kc_ref, vc_ref, kn_ref, vn_ref, lam_ref, sub_ref, _, o_ref = refs[:9]
    else:
        qt_ref, kn_ref, vn_ref, lam_ref, sub_ref, o_ref = refs[:6]
    qm_ref, s_ref, p_ref, m_ref, l_ref, a_ref, acc_ref = refs[-7:]
    tq = qt_ref.shape[1]
    qt = qt_ref[...]
    sub_i = lax.broadcasted_iota(jnp.int32, qt.shape, 0)
    qm_ref[0] = jnp.where(sub_i < DH_C, qt, jnp.zeros_like(qt))
    qm_ref[1] = jnp.where(sub_i >= DH_C, qt, jnp.zeros_like(qt))
    m_ref[...] = jnp.full(m_ref.shape, -1e30, F32)
    l_ref[...] = jnp.zeros_like(l_ref)
    acc_ref[...] = jnp.zeros_like(acc_ref)

    def scores(k, buf):
        for c in range(2):
            s_ref[buf, c] = _dot(k, qm_ref[c])

    def softmax_update(buf):
        for c in range(2):
            for cb in range(tq // 128):
                cols = slice(cb * 128, (cb + 1) * 128)
                m_old = m_ref[c, :, cols]
                m_new = jnp.maximum(m_old, jnp.max(s_ref[buf, c, :, cols], axis=0, keepdims=True))
                alpha = jnp.exp2(m_old - m_new)
                p = jnp.exp2(s_ref[buf, c, :, cols] - m_new)
                l_ref[c, :, cols] = alpha * l_ref[c, :, cols] + jnp.sum(p, axis=0, keepdims=True)
                m_ref[c, :, cols] = m_new
                a_ref[c, :, cols] = alpha
                p_ref[buf, c, :, cols] = p.astype(BF16)

    def accumulate(vt, buf):
        for c in range(2):
            acc_ref[c] = a_ref[c] * acc_ref[c] + _dot(vt, p_ref[buf, c])

    new_k = lambda j: kn_ref[pl.ds(pl.multiple_of(j * TM, TM), TM), :]
    if n_new == 1:
        scores(kn_ref[...], 0)
        softmax_update(0)
        accumulate(vn_ref[0], 0)
    else:
        scores(new_k(0), 0)

        def body(i, carry):
            scores(new_k(2 * i + 1), 1)
            softmax_update(0)
            accumulate(vn_ref[2 * i], 0)
            scores(new_k(2 * i + 2), 0)
            softmax_update(1)
            accumulate(vn_ref[2 * i + 1], 1)
            return carry
        lax.fori_loop(0, n_new // 2 - 1, body, 0)
        scores(kn_ref[(n_new - 1) * TM:, :], 1)
        softmax_update(0)
        accumulate(vn_ref[n_new - 2], 0)
        scores(kc_ref[...], 0)
        softmax_update(1)
        accumulate(vn_ref[n_new - 1], 1)
        softmax_update(0)
        accumulate(vc_ref[0], 0)

    lq = lam_ref[...]
    lam = (jnp.exp(jnp.sum(lq[0:1] * lq[1:2], axis=-1, keepdims=True))
           - jnp.exp(jnp.sum(lq[2:3] * lq[3:4], axis=-1, keepdims=True)) + lam_init)
    for cb in range(tq // 128):
        cols = slice(cb * 128, (cb + 1) * 128)
        o = acc_ref[0, :, cols] / l_ref[0, :, cols] - lam * (acc_ref[1, :, cols] / l_ref[1, :, cols])
        y = o * lax.rsqrt(jnp.mean(o * o, axis=0, keepdims=True) + 1e-6)
        o_ref[cols, :] = (y.T * sub_ref[...] * (1.0 - lam_init)).astype(BF16)


def _attn_scratch(tq):
    return [pltpu.VMEM((2, 128, tq), BF16), pltpu.VMEM((2, 2, TM, tq), F32), pltpu.VMEM((2, 2, TM, tq), BF16),
            pltpu.VMEM((2, 1, tq), F32), pltpu.VMEM((2, 1, tq), F32), pltpu.VMEM((2, 1, tq), F32),
            pltpu.VMEM((2, 128, tq), F32)]


def _attention(l, qt, kb, vt, ck, cvt, lambda_qk_l, subln_l):
    lam_init = 0.8 - 0.6 * math.exp(-0.3 * l)
    sub = subln_l.reshape(1, 2 * DH_C)
    o_ctx = pl.pallas_call(
        functools.partial(_attn_kernel, lam_init, 1, False),
        grid=(BATCH, H_C),
        in_specs=[pl.BlockSpec((128, SEQ), lambda b, h: (h, b)),
                  pl.BlockSpec((SEQ, 128), lambda b, h: (b, h)),
                  pl.BlockSpec((1, 128, TM), lambda b, h: (b, h, 0)),
                  pl.BlockSpec((4, DH_C), lambda b, h: (0, 0)),
                  pl.BlockSpec((1, 2 * DH_C), lambda b, h: (0, 0))],
        out_specs=pl.BlockSpec((SEQ, 128), lambda b, h: (b, h)),
        out_shape=jax.ShapeDtypeStruct((T, D_C), BF16),
        scratch_shapes=_attn_scratch(SEQ),
        compiler_params=_cparams(("arbitrary", "arbitrary")),
        name="attn_ctx",
    )(qt, kb, vt, lambda_qk_l, sub)

    nq = DEC_SEQ // TQ
    base_q = BATCH * SEQ // TQ
    base_kv = BATCH * SEQ // DEC_SEQ
    o_smp = pl.pallas_call(
        functools.partial(_attn_kernel, lam_init, SMP_TILES, True),
        grid=(DEC_BATCH, H_C, nq),
        in_specs=[pl.BlockSpec((128, TQ), lambda b, h, j: (h, base_q + b * nq + j)),
                  pl.BlockSpec((PAST, 128), lambda b, h, j: (b, h)),
                  pl.BlockSpec((1, 128, PAST), lambda b, h, j: (b, h, 0)),
                  pl.BlockSpec((DEC_SEQ, 128), lambda b, h, j: (base_kv + b, h)),
                  pl.BlockSpec((SMP_TILES, 128, TM), lambda b, h, j: (base_kv + b, h, 0)),
                  pl.BlockSpec((4, DH_C), lambda b, h, j: (0, 0)),
                  pl.BlockSpec((1, 2 * DH_C), lambda b, h, j: (0, 0)),
                  pl.BlockSpec(memory_space=pl.ANY)],
        out_specs=pl.BlockSpec((TQ, 128), lambda b, h, j: (base_q + b * nq + j, h)),
        out_shape=jax.ShapeDtypeStruct((T, D_C), BF16),
        input_output_aliases={7: 0},
        scratch_shapes=_attn_scratch(TQ),
        compiler_params=_cparams(("arbitrary", "arbitrary", "arbitrary")),
        name="attn_smp",
    )(qt, ck, cvt, kb, vt, lambda_qk_l, sub, o_ctx)
    return o_smp


def _merge_kernel(nx, *refs):
    a_ref, b_ref, c_ref, ga_ref, gb_ref, gc_ref, g1_ref, wb_ref, wo_ref, o_ref = refs[nx:]
    ya = _dot(a_ref[...], wb_ref[0:D_A, :])
    yb = _dot(b_ref[...], wb_ref[D_A:D_A + D_B, :])
    yc = _dot(c_ref[...], wb_ref[D_A + D_B:, :])
    m = _sigmoid(ga_ref[...]) * ya + _sigmoid(gb_ref[...]) * yb + _sigmoid(gc_ref[...]) * yc
    o_ref[...] = _x_tile(refs[:nx]) + g1_ref[0] * _dot(m.astype(BF16), wo_ref[...])


def _merge(ya, yb, yc, u, x, modt, w_branch_l, w_out_l, l):
    row = lambda w: pl.BlockSpec((TM, w), lambda i: (i, 0))
    gate0 = C_GT * CB // D
    gate = lambda k: pl.BlockSpec((TM, D), lambda i: (i, gate0 + k))
    x_specs, xs = _x_specs(x)
    return pl.pallas_call(
        functools.partial(_merge_kernel, len(xs)),
        grid=(NT,),
        in_specs=x_specs + [row(D_A), row(D_B), row(D_C), gate(0), gate(1), gate(2), _mod_spec(l, 2),
                            pl.BlockSpec((D_A + D_B + D_C, D), lambda i: (0, 0)),
                            pl.BlockSpec((D, D), lambda i: (0, 0))],
        out_specs=row(D),
        out_shape=jax.ShapeDtypeStruct((T, D), F32),
        compiler_params=_cparams(("arbitrary",)),
        name="merge_out",
    )(*xs, ya, yb, yc, u, u, u, modt, w_branch_l, w_out_l)


def _ffn_kernel(x_ref, g_ref, sc_ref, sh_ref, g2_ref, wi_ref, wo_ref, *o_refs):
    x = x_ref[...]
    h = _rms_mod(x, g_ref[...], sc_ref[0], sh_ref[0]).astype(BF16)
    gu = _dot(h, wi_ref[...])
    ff = _silu(gu[:, :D_FF]) * gu[:, D_FF:]
    y = x + g2_ref[0] * _dot(ff.astype(BF16), wo_ref[...])
    if len(o_refs) == 1:
        o_refs[0][...] = y
    else:
        @pl.when(pl.program_id(0) < CTX_TILES)
        def _():
            o_refs[0][...] = y

        @pl.when(pl.program_id(0) >= CTX_TILES)
        def _():
            o_refs[1][...] = y


def _ffn(x, modt, norm2_l, w_ffn_in_l, w_ffn_out_l, l, split_out):
    if split_out:
        out_specs, _ = _x_specs((None, None))
        out_shape = [jax.ShapeDtypeStruct((BATCH * SEQ, D), F32), jax.ShapeDtypeStruct((DEC_BATCH * DEC_SEQ, D), F32)]
    else:
        out_specs = pl.BlockSpec((TM, D), lambda i: (i, 0))
        out_shape = jax.ShapeDtypeStruct((T, D), F32)
    return pl.pallas_call(
        _ffn_kernel,
        grid=(NT,),
        in_specs=[pl.BlockSpec((TM, D), lambda i: (i, 0)),
                  pl.BlockSpec((1, D), lambda i: (0, 0)),
                  _mod_spec(l, 4), _mod_spec(l, 3), _mod_spec(l, 5),
                  pl.BlockSpec((D, 2 * D_FF), lambda i: (0, 0), pipeline_mode=pl.Buffered(1)),
                  pl.BlockSpec((D_FF, D), lambda i: (0, 0), pipeline_mode=pl.Buffered(1))],
        out_specs=out_specs,
        out_shape=out_shape,
        compiler_params=_cparams(("arbitrary",)),
        name="ffn",
    )(x, norm2_l.reshape(1, D), modt, modt, modt, w_ffn_in_l, w_ffn_out_l)


def _layer(l, x, modt, cache_k, cache_v, state_hgrn, tabs, p, outs):
    (norm1, norm2, w_in, conv_w, conv_b, conv_ln_g, conv_ln_b, hgrn_lb, hgrn_norm, q_norm, k_norm, lambda_qk,
     subln, w_branch, w_out, w_ffn_in, w_ffn_out) = p
    cos_t, sin_t, bd, maps_f, maps_b = tabs
    new_k, new_v, new_s = outs
    u = _in_proj(x, modt, norm1[l], w_in[l].astype(BF16), l)
    ya = _conv_branch(u, conv_w[l], conv_b[l], conv_ln_g[l], conv_ln_b[l])
    lbp = hgrn_lb.reshape(DEPTH * 2, D_B)
    ob, new_s = _hgrn_pass(True, l, u, lbp, state_hgrn[:, l, 1], maps_b, new_s)
    yb, new_s = _hgrn_pass(False, l, u, lbp, state_hgrn[:, l, 0], maps_f, new_s, ob=ob, hgrn_norm_l=hgrn_norm[l])
    qt, kb, vt, new_k, new_v = _qk_prep(l, u, q_norm[l], k_norm[l], cos_t, sin_t, bd, new_k, new_v)
    ck = cache_k[:, l].reshape(DEC_BATCH * PAST, D_C).astype(BF16)
    cvt = cache_v[:, l].reshape(DEC_BATCH, PAST, D_C).astype(BF16).transpose(0, 2, 1)
    yc = _attention(l, qt, kb, vt, ck, cvt, lambda_qk[l], subln[l])
    x = _merge(ya, yb, yc, u, x, modt, w_branch[l].astype(BF16), w_out[l].astype(BF16), l)
    x = _ffn(x, modt, norm2[l], w_ffn_in[l].astype(BF16), w_ffn_out[l].astype(BF16), l, split_out=l == DEPTH - 1)
    return x, (new_k, new_v, new_s)


def kernel(x_prompt, x_sample, cache_k, cache_v, state_hgrn, c, c_ctx, w_mod, b_mod, norm1, norm2, w_in, conv_w,
           conv_b, conv_ln_g, conv_ln_b, hgrn_lb, hgrn_norm, q_norm, k_norm, lambda_qk, subln, w_branch, w_out,
           w_ffn_in, w_ffn_out):
    x = (x_prompt.reshape(BATCH * SEQ, D), x_sample.reshape(DEC_BATCH * DEC_SEQ, D))
    cv8 = jnp.concatenate([c_ctx[None, :], c, jnp.zeros((8 - 1 - DEC_BATCH, D), F32)], axis=0)
    mod = _modulation(cv8, w_mod, b_mod)
    modt = mod.reshape(DEPTH, 8, 6, D).transpose(0, 2, 1, 3).reshape(DEPTH * 6 * 8, 1, D)
    cos_t, sin_t = _rope_tables()
    seg = np.arange(D_C) // DH_C
    bd = jnp.asarray(seg[:, None] == seg[None, :], BF16)
    maps = lambda rev: (jnp.asarray(_level_map(rev)), jnp.asarray(_low_map(rev)), jnp.asarray(_tri_map(rev), BF16))
    tabs = (cos_t, sin_t, bd, maps(False), maps(True))
    p = (norm1, norm2, w_in, conv_w, conv_b, conv_ln_g, conv_ln_b, hgrn_lb, hgrn_norm, q_norm, k_norm, lambda_qk,
         subln, w_branch, w_out, w_ffn_in, w_ffn_out)
    outs = (jnp.zeros((BATCH, DEPTH, SEQ, D_C), F32), jnp.zeros((BATCH, DEPTH, SEQ, D_C), F32),
            jnp.zeros((BATCH, DEPTH, 2, H_B, DK_B, DK_B), F32))
    for l in range(DEPTH):
        x, outs = _layer(l, x, modt, cache_k, cache_v, state_hgrn, tabs, p, outs)
    new_k, new_v, new_s = outs
    y_prompt = x[0].reshape(BATCH, SEQ, D)
    y_sample = x[1].reshape(DEC_BATCH, DEC_SEQ, D)
    return (y_prompt, y_sample, new_k.reshape(BATCH, DEPTH, SEQ, H_C, 2, DH_C),
            new_v.reshape(BATCH, DEPTH, SEQ, H_C, 2 * DH_C), new_s)
```

```python
import functools
import math

import numpy as np
import jax
import jax.numpy as jnp
from jax import lax
from jax.experimental import pallas as pl
from jax.experimental.pallas import tpu as pltpu

F32 = jnp.float32
BF16 = jnp.bfloat16

D = 1024
BATCH, SEQ = 32, 256
DEPTH = 2
DEC_BATCH, DEC_SEQ = 2, 4096
PAST = 256
GRID_W = 64
D_A, CONV_K = 512, 31
H_B, DK_B = 4, 128
D_B = 512
H_C, DH_C = 4, 64
D_C = 512
ROPE_BASE = 10000.0
D_FF = 2816
IN_COLS = 8192

TM = 256
CTX_TILES = BATCH * SEQ // TM
SMP_TILES = DEC_SEQ // TM
T = BATCH * SEQ + DEC_BATCH * DEC_SEQ
NT = T // TM
CB = 512
HALO = 16
TQ = 512
VMEM_LIMIT = 56 * 1024 * 1024

C_A, C_AG, C_HQ, C_HI, C_HFF, C_HFB, C_HG, C_AQ, C_AK, C_AV, C_GT = 0, 1, 2, 3, 4, 5, 6, 7, 8, 9, 10


def _cparams(sem):
    return pltpu.CompilerParams(dimension_semantics=sem, vmem_limit_bytes=VMEM_LIMIT)


def _mod_row(i):
    return jnp.where(i < CTX_TILES, 0, 1 + (i - CTX_TILES) // SMP_TILES)


def _sigmoid(x):
    return 0.5 * jnp.tanh(0.5 * x) + 0.5


def _sigmoid_small_accurate(x):
    return jnp.exp(-jnp.log(1.0 + jnp.exp(-x)))


def _silu(x):
    return x * _sigmoid(x)


def _dot(a, b):
    return jnp.dot(a, b, preferred_element_type=F32)


def _dot_nt(a, b):
    return lax.dot_general(a, b, (((1,), (1,)), ((), ())), preferred_element_type=F32)


def _dot_tn(a, b):
    return lax.dot_general(a, b, (((0,), (0,)), ((), ())), preferred_element_type=F32)


def _split_bf16(x):
    hi = x.astype(BF16)
    lo = (x - hi.astype(F32)).astype(BF16)
    return hi, lo


MOD_TN = 1536


def _mod_kernel(cv_ref, w_ref, b_ref, o_ref):
    a = _silu(cv_ref[...])
    a_hi, a_lo = _split_bf16(a)
    w_hi, w_lo = _split_bf16(w_ref[0])
    acc = _dot(a_hi, w_hi) + _dot(a_lo, w_hi) + _dot(a_hi, w_lo)
    o_ref[0] = acc + b_ref[0]


def _modulation(cv8, w_mod, b_mod):
    return pl.pallas_call(
        _mod_kernel,
        grid=(DEPTH, 6 * D // MOD_TN),
        in_specs=[
            pl.BlockSpec((8, D), lambda l, j: (0, 0)),
            pl.BlockSpec((1, D, MOD_TN), lambda l, j: (l, 0, j)),
            pl.BlockSpec((1, 1, MOD_TN), lambda l, j: (l, 0, j)),
        ],
        out_specs=pl.BlockSpec((1, 8, MOD_TN), lambda l, j: (l, 0, j)),
        out_shape=jax.ShapeDtypeStruct((DEPTH, 8, 6 * D), F32),
        compiler_params=_cparams(("arbitrary", "arbitrary")),
        name="adaln_mod",
    )(cv8, w_mod, b_mod.reshape(DEPTH, 1, 6 * D))


def _mod_spec(l, which):
    return pl.BlockSpec((1, 1, D), lambda i: ((l * 6 + which) * 8 + _mod_row(i), 0, 0))


def _rms_mod(x, g, sc, sh):
    y = x * lax.rsqrt(jnp.mean(x * x, axis=-1, keepdims=True) + 1e-6) * g
    return y * (1.0 + sc) + sh


def _x_specs(x):
    if isinstance(x, tuple):
        return [pl.BlockSpec((TM, D), lambda i: (jnp.minimum(i, CTX_TILES - 1), 0)),
                pl.BlockSpec((TM, D), lambda i: (jnp.maximum(i - CTX_TILES, 0), 0))], list(x)
    return [pl.BlockSpec((TM, D), lambda i: (i, 0))], [x]


def _x_tile(x_refs):
    if len(x_refs) == 2:
        return jnp.where(pl.program_id(0) < CTX_TILES, x_refs[0][...], x_refs[1][...])
    return x_refs[0][...]


def _in_kernel(nx, *refs):
    g_ref, sc_ref, sh_ref, w_ref, o_ref = refs[nx:]
    h = _rms_mod(_x_tile(refs[:nx]), g_ref[...], sc_ref[0], sh_ref[0])
    o_ref[...] = _dot(h.astype(BF16), w_ref[...])


def _in_proj(x, modt, norm1_l, w_in_l, l):
    x_specs, xs = _x_specs(x)
    return pl.pallas_call(
        functools.partial(_in_kernel, len(xs)),
        grid=(NT,),
        in_specs=x_specs + [
            pl.BlockSpec((1, D), lambda i: (0, 0)),
            _mod_spec(l, 1),
            _mod_spec(l, 0),
            pl.BlockSpec((D, IN_COLS), lambda i: (0, 0), pipeline_mode=pl.Buffered(1)),
        ],
        out_specs=pl.BlockSpec((TM, IN_COLS), lambda i: (i, 0)),
        out_shape=jax.ShapeDtypeStruct((T, IN_COLS), F32),
        compiler_params=_cparams(("arbitrary",)),
        name="in_proj",
    )(*xs, norm1_l.reshape(1, D), modt, modt, w_in_l)


CONV_RB = 32


def _conv_kernel(ac_ref, gc_ref, ap_ref, gp_ref, an_ref, gn_ref, w_ref, b_ref, lg_ref, lb_ref, o_ref, sh_ref,
                 cv_ref):
    i = pl.program_id(0)
    in_smp = i >= CTX_TILES
    pos = i % SMP_TILES
    has_prev = jnp.logical_and(in_smp, pos != 0)
    has_next = jnp.logical_and(in_smp, pos != SMP_TILES - 1)
    n = TM + 2 * HALO
    sh_ref[0, HALO:HALO + TM, :] = ac_ref[...] * _sigmoid(gc_ref[...])
    sh_ref[0, 0:HALO, :] = jnp.where(has_prev, ap_ref[...] * _sigmoid(gp_ref[...]), 0.0)
    sh_ref[0, HALO + TM:, :] = jnp.where(has_next, an_ref[...] * _sigmoid(gn_ref[...]), 0.0)
    for r in range(1, 8):
        sh_ref[r, 0:n - 8, :] = sh_ref[0, r:r + n - 8, :]
    off = HALO - CONV_K // 2

    def body(rb, carry):
        base = pl.multiple_of(rb * CONV_RB, CONV_RB)
        acc = jnp.zeros((CONV_RB, D_A), F32)
        for k in range(CONV_K):
            r = (off + k) % 8
            acc = acc + sh_ref[r, pl.ds(base + (off + k - r), CONV_RB), :] * w_ref[k:k + 1, :]
        cv_ref[pl.ds(base, CONV_RB), :] = acc
        return carry
    lax.fori_loop(0, TM // CONV_RB, body, 0)
    acc = cv_ref[...] + b_ref[...]
    mu = jnp.mean(acc, axis=-1, keepdims=True)
    xc = acc - mu
    y = xc * lax.rsqrt(jnp.mean(xc * xc, axis=-1, keepdims=True) + 1e-5)
    y = y * lg_ref[...] + lb_ref[...]
    o_ref[...] = _silu(y).astype(BF16)


def _conv_branch(u, conv_w_l, conv_b_l, ln_g_l, ln_b_l):
    r = TM // HALO
    last = T // HALO - 1
    cur = lambda c: pl.BlockSpec((TM, CB), lambda i: (i, c))
    prev = lambda c: pl.BlockSpec((HALO, CB), lambda i: (jnp.maximum(i * r - 1, 0), c))
    nxt = lambda c: pl.BlockSpec((HALO, CB), lambda i: (jnp.minimum((i + 1) * r, last), c))
    vec = pl.BlockSpec((1, D_A), lambda i: (0, 0))
    return pl.pallas_call(
        _conv_kernel,
        grid=(NT,),
        in_specs=[cur(C_A), cur(C_AG), prev(C_A), prev(C_AG), nxt(C_A), nxt(C_AG),
                  pl.BlockSpec((CONV_K, D_A), lambda i: (0, 0)), vec, vec, vec],
        out_specs=pl.BlockSpec((TM, D_A), lambda i: (i, 0)),
        out_shape=jax.ShapeDtypeStruct((T, D_A), BF16),
        scratch_shapes=[pltpu.VMEM((8, TM + 2 * HALO, D_A), F32), pltpu.VMEM((TM, D_A), F32)],
        compiler_params=_cparams(("arbitrary",)),
        name="conv_branch",
    )(u, u, u, u, u, u, conv_w_l, conv_b_l.reshape(1, D_A), ln_g_l.reshape(1, D_A), ln_b_l.reshape(1, D_A))


N_LEVELS = 8


def _level_map(reverse):
    t = np.arange(TM)[:, None]
    s = np.arange(TM)[None, :]
    x = t ^ s
    lv = np.zeros((TM, TM), np.int32)
    for bit in range(N_LEVELS):
        lv[(x >> bit) == 1] = bit + 1
    ok = (t < s) if reverse else (t > s)
    return np.where(ok, lv, 0).astype(np.int32)


def _shift_rows(x, k):
    return pltpu.roll(x, k % TM, 0)


def _tri_map(reverse):
    t = np.arange(TM)[:, None]
    s = np.arange(TM)[None, :]
    return (s >= t) if reverse else (s <= t)


def _cumsum_rows(x, tri):
    hi = x.astype(BF16)
    r1 = x - hi.astype(F32)
    mid = r1.astype(BF16)
    lo = (r1 - mid.astype(F32)).astype(BF16)
    n = x.shape[1]
    s = _dot(tri, jnp.concatenate([hi, mid, lo], axis=1))
    return s[:, :n] + s[:, n:2 * n] + s[:, 2 * n:]


def _anchors(cum, row, reverse):
    out = []
    z = cum
    for bit in range(N_LEVELS):
        h = 1 << bit
        if 2 * h >= 16:
            c3 = cum.reshape(TM // (2 * h), 2 * h, DK_B)
            p = h if reverse else h - 1
            out.append(jnp.broadcast_to(c3[:, p:p + 1, :], c3.shape).reshape(TM, DK_B))
            continue
        hi = (row & h) != 0
        if reverse:
            out.append(jnp.where(hi, z, _shift_rows(z, -h)))
            z = jnp.where(hi, _shift_rows(z, h), z)
        else:
            out.append(jnp.where(hi, _shift_rows(z, h), z))
            z = jnp.where(hi, z, _shift_rows(z, -h))
    return out


LOW_BITS = 5
SAFE_RANGE = 60.0


def _low_map(reverse):
    t = np.arange(TM)[:, None]
    s = np.arange(TM)[None, :]
    same = (t >> LOW_BITS) == (s >> LOW_BITS)
    ok = (t <= s) if reverse else (t >= s)
    return (same & ok).astype(np.int32)


def _block_anchor(cum, reverse):
    n = 1 << LOW_BITS
    c3 = cum.reshape(TM // n, n, DK_B)
    p = n - 1 if reverse else 0
    return jnp.broadcast_to(c3[:, p:p + 1, :], c3.shape).reshape(TM, DK_B)


def _hgrn_kernel(reverse, l, *refs):
    (q_ref, v_ref, f_ref, lbp_ref, s0_ref, lv_ref, lo_ref, tri_ref, _, o_ref, ns_ref) = refs[:11]
    st_ref, qs_ref, ks_ref, cs_ref, os_ref, as_ref = refs[-6:]
    g = pl.program_id(0)
    i = NT - 1 - g if reverse else g
    is_ctx = i < CTX_TILES
    pos = i % SMP_TILES
    seq_start = jnp.logical_and(i >= CTX_TILES, pos == (SMP_TILES - 1 if reverse else 0))

    @pl.when(is_ctx)
    def _():
        st_ref[...] = jnp.zeros_like(st_ref)

    @pl.when(seq_start)
    def _():
        for h in range(H_B):
            st_ref[h] = s0_ref[0, h].T

    d = 1 if reverse else 0
    rows = [lbp_ref[k * 2 + d:k * 2 + d + 1, :] for k in range(DEPTH)]
    mx = functools.reduce(jnp.maximum, rows)
    ex = [jnp.exp(r - mx) for r in rows]
    den = functools.reduce(jnp.add, ex)
    lb_all = jnp.zeros_like(mx)
    for k in range(1, l + 1):
        lb_all = lb_all + ex[k] / den

    row = lax.broadcasted_iota(jnp.int32, (TM, DK_B), 0)
    spread = jnp.zeros((1, DK_B), F32)
    for h in range(H_B):
        sl = slice(h * DK_B, (h + 1) * DK_B)
        lb = lb_all[:, sl]
        fg = lb + (1.0 - lb) * _sigmoid_small_accurate(f_ref[:, sl])
        cum = _cumsum_rows(jnp.log(fg), tri_ref[...])
        ks_ref[h] = 1.0 - fg
        qs_ref[h] = _silu(q_ref[:, sl])
        cs_ref[h] = cum
        spread = jnp.maximum(spread, jnp.max(_block_anchor(cum, reverse) - cum, axis=0, keepdims=True))
    single_anchor_ok = jnp.max(spread) < SAFE_RANGE

    half = TM // 2
    halves = (slice(0, half), slice(half, TM))
    late, early = (halves[0], halves[1]) if reverse else (halves[1], halves[0])

    lv = lv_ref[0:half, 0:half]

    def operands(h, fq, fk):
        return (qs_ref[h] * fq).astype(BF16), (ks_ref[h] * fk).astype(BF16)

    for h in range(H_B):
        sl = slice(h * DK_B, (h + 1) * DK_B)
        q, kg, cum = qs_ref[h], ks_ref[h], cs_ref[h]
        vb = v_ref[:, sl].astype(BF16)
        tot = cum[0:1, :] if reverse else cum[TM - 1:TM, :]
        st = st_ref[h]
        o = _dot_nt((q * jnp.exp(cum)).astype(BF16), st.astype(BF16))
        anchors = _anchors(cum, row, reverse)
        fac = jnp.exp(-jnp.abs(cum - anchors[N_LEVELS - 1]))
        qf, kf = operands(h, fac, fac)
        o_top = _dot(_dot_nt(qf[late], kf[early]).astype(BF16), vb[early])
        zero = jnp.zeros_like(o_top)
        os_ref[h] = o + jnp.concatenate([o_top, zero] if reverse else [zero, o_top], axis=0)
        a = [jnp.zeros((half, half), F32) for _ in halves]
        for bit in range(LOW_BITS, N_LEVELS - 1):
            fac = jnp.exp(-jnp.abs(cum - anchors[bit]))
            qf, kf = operands(h, fac, fac)
            a = [jnp.where(lv == bit + 1, _dot_nt(qf[r], kf[r]), a[b]) for b, r in enumerate(halves)]
        for b in range(2):
            as_ref[h, b] = a[b]
        kbar = (kg * jnp.exp(tot - cum)).astype(BF16)
        st_ref[h] = jnp.exp(tot) * st + _dot_tn(vb, kbar)

    def heads(single_anchor):
        for h in range(H_B):
            sl = slice(h * DK_B, (h + 1) * DK_B)
            cum = cs_ref[h]
            v = v_ref[:, sl]
            vb = v.astype(BF16)
            o = os_ref[h]
            a = [as_ref[h, b] for b in range(2)]
            if single_anchor:
                e = _block_anchor(cum, reverse)
                qf, kf = operands(h, jnp.exp(cum - e), jnp.exp(e - cum))
                lo = lo_ref[0:half, 0:half] != 0
                a = [jnp.where(lo, _dot_nt(qf[r], kf[r]), a[b]) for b, r in enumerate(halves)]
            else:
                o = o + jnp.sum(qs_ref[h] * ks_ref[h], axis=-1, keepdims=True) * v
                anchors = _anchors(cum, row, reverse)
                for bit in range(LOW_BITS):
                    fac = jnp.exp(-jnp.abs(cum - anchors[bit]))
                    qf, kf = operands(h, fac, fac)
                    a = [jnp.where(lv == bit + 1, _dot_nt(qf[r], kf[r]), a[b]) for b, r in enumerate(halves)]
            o_ref[:, sl] = o + jnp.concatenate([_dot(a[b].astype(BF16), vb[r]) for b, r in enumerate(halves)],
                                               axis=0)

    @pl.when(single_anchor_ok)
    def _():
        heads(True)

    @pl.when(jnp.logical_not(single_anchor_ok))
    def _():
        heads(False)

    @pl.when(is_ctx)
    def _():
        for h in range(H_B):
            ns_ref[0, 0, 0, h] = st_ref[h].T


def _hgrn_pass(reverse, l, u, hgrn_lb, s0, maps, new_state):
    tile = (lambda g: NT - 1 - g) if reverse else (lambda g: g)
    col = lambda c: pl.BlockSpec((TM, CB), lambda g: (tile(g), c))
    s0_spec = pl.BlockSpec((1, H_B, DK_B, DK_B),
                           lambda g: (jnp.clip((tile(g) - CTX_TILES) // SMP_TILES, 0, DEC_BATCH - 1), 0, 0, 0))
    lbp_spec = pl.BlockSpec((DEPTH * 2, D_B), lambda g: (0, 0))
    map_spec = pl.BlockSpec((TM, TM), lambda g: (0, 0))
    d = 1 if reverse else 0
    ns_spec = pl.BlockSpec((1, 1, 1, H_B, DK_B, DK_B),
                           lambda g: (jnp.minimum(tile(g), CTX_TILES - 1), l, d, 0, 0, 0))
    any_spec = pl.BlockSpec(memory_space=pl.ANY)
    f_col = C_HFB if reverse else C_HFF
    in_specs = [col(C_HQ), col(C_HI), col(f_col), lbp_spec, s0_spec, map_spec, map_spec, map_spec, any_spec]
    args = (u, u, u, hgrn_lb, s0, *maps, new_state)
    o_shape = jax.ShapeDtypeStruct((T, D_B), F32)
    head_tile = pltpu.VMEM((H_B, TM, DK_B), F32)
    return pl.pallas_call(
        functools.partial(_hgrn_kernel, reverse, l),
        grid=(NT,),
        in_specs=in_specs,
        out_specs=[pl.BlockSpec((TM, D_B), lambda g: (tile(g), 0)), ns_spec],
        out_shape=[o_shape, jax.ShapeDtypeStruct(new_state.shape, F32)],
        input_output_aliases={len(args) - 1: 1},
        scratch_shapes=[pltpu.VMEM((H_B, DK_B, DK_B), F32), head_tile, head_tile, head_tile, head_tile,
                        pltpu.VMEM((H_B, 2, TM // 2, TM // 2), F32)],
        compiler_params=_cparams(("arbitrary",)),
        name="hgrn_bwd" if reverse else "hgrn_fwd",
    )(*args)


def _qk_kernel(aq_ref, ak_ref, av_ref, qn_ref, kn_ref, cos_ref, sin_ref, bd_ref, _k, _v, qt_ref, kb_ref, vt_ref,
               nk_ref, nv_ref):
    bd = bd_ref[...]
    lane = lax.broadcasted_iota(jnp.int32, (TM, 128), 1)
    even = (lane & 1) == 0
    cos = cos_ref[...]
    sin = sin_ref[...]

    def seg_norm(x, g):
        hi, lo = _split_bf16(x * x)
        ss = _dot(hi, bd) + _dot(lo, bd)
        return x * lax.rsqrt(ss * (1.0 / DH_C) + 1e-6) * g

    def rope(x):
        parts = []
        for s in range(D_C // 128):
            xs = x[:, s * 128:(s + 1) * 128]
            sw = jnp.where(even, pltpu.roll(xs, 127, 1), pltpu.roll(xs, 1, 1))
            parts.append(xs * cos + sw * sin)
        return jnp.concatenate(parts, axis=1)

    q = seg_norm(aq_ref[...], qn_ref[...])
    k = seg_norm(ak_ref[...], kn_ref[...])
    @pl.when(pl.program_id(0) < CTX_TILES)
    def _():
        nk_ref[0, 0] = k
        nv_ref[0, 0] = av_ref[...]

    qt_ref[...] = (rope(q) * (DH_C ** -0.5 * math.log2(math.e))).T.astype(BF16)
    kb_ref[...] = rope(k).astype(BF16)
    vt_ref[0] = av_ref[...].T.astype(BF16)


def _rope_tables():
    rows = DEC_SEQ // GRID_W
    row = jnp.broadcast_to(jnp.arange(rows)[:, None], (rows, GRID_W)).reshape(-1).astype(F32)
    colp = jnp.broadcast_to(jnp.arange(GRID_W)[None, :], (rows, GRID_W)).reshape(-1).astype(F32)
    half = DH_C // 2
    inv = ROPE_BASE ** (-jnp.arange(0, half, 2, dtype=F32) / half)
    ang = jnp.concatenate([row[:, None] * inv, colp[:, None] * inv], axis=-1)
    cos = jnp.repeat(jnp.cos(ang), 2, axis=-1)
    sin = jnp.repeat(jnp.sin(ang), 2, axis=-1) * jnp.tile(jnp.array([-1.0, 1.0], F32), half)
    cos = jnp.concatenate([jnp.ones((TM, DH_C), F32), cos], axis=0)
    sin = jnp.concatenate([jnp.zeros((TM, DH_C), F32), sin], axis=0)
    return jnp.tile(cos, (1, 2)), jnp.tile(sin, (1, 2))


def _qk_prep(l, u, q_norm_l, k_norm_l, cos_t, sin_t, bd, new_k, new_v):
    col = lambda c: pl.BlockSpec((TM, CB), lambda i: (i, c))
    vec = pl.BlockSpec((1, D_C), lambda i: (0, 0))
    tab = pl.BlockSpec((TM, 128), lambda i: (jnp.where(i < CTX_TILES, 0, 1 + i % SMP_TILES), 0))
    any_spec = pl.BlockSpec(memory_space=pl.ANY)
    cache = pl.BlockSpec((1, 1, SEQ, D_C), lambda i: (jnp.minimum(i, CTX_TILES - 1), l, 0, 0))
    return pl.pallas_call(
        _qk_kernel,
        grid=(NT,),
        in_specs=[col(C_AQ), col(C_AK), col(C_AV), vec, vec, tab, tab,
                  pl.BlockSpec((D_C, D_C), lambda i: (0, 0)), any_spec, any_spec],
        out_specs=[pl.BlockSpec((D_C, TM), lambda i: (0, i)), pl.BlockSpec((TM, D_C), lambda i: (i, 0)),
                   pl.BlockSpec((1, D_C, TM), lambda i: (i, 0, 0)), cache, cache],
        out_shape=[jax.ShapeDtypeStruct((D_C, T), BF16), jax.ShapeDtypeStruct((T, D_C), BF16),
                   jax.ShapeDtypeStruct((NT, D_C, TM), BF16),
                   jax.ShapeDtypeStruct(new_k.shape, F32), jax.ShapeDtypeStruct(new_v.shape, F32)],
        input_output_aliases={8: 3, 9: 4},
        compiler_params=_cparams(("arbitrary",)),
        name="qk_prep",
    )(u, u, u, jnp.tile(q_norm_l, D_C // DH_C).reshape(1, D_C), jnp.tile(k_norm_l, D_C // DH_C).reshape(1, D_C),
      cos_t, sin_t, bd, new_k, new_v)


def _attn_kernel(lam_init, n_new, has_cache, n_heads, *refs):
    if has_cache:
        qt_ref, kc_ref, vc_ref, kn_ref, vn_ref, lam_ref, sub_ref, _, o_ref = refs[:9]
    else:
        qt_ref, kn_ref, vn_ref, lam_ref, sub_ref, _, o_ref = refs[:7]
    qm_ref, s_ref, p_ref, m_ref, l_ref, a_ref, acc_ref = refs[-7:]
    tq = qt_ref.shape[1]
    lq = lam_ref[...]
    lam = (jnp.exp(jnp.sum(lq[0:1] * lq[1:2], axis=-1, keepdims=True))
           - jnp.exp(jnp.sum(lq[2:3] * lq[3:4], axis=-1, keepdims=True)) + lam_init)

    def scores(k, buf):
        for c in range(2):
            s_ref[buf, c] = _dot(k, qm_ref[c])

    def softmax_update(buf):
        for c in range(2):
            for cb in range(tq // 128):
                cols = slice(cb * 128, (cb + 1) * 128)
                m_old = m_ref[c, :, cols]
                m_new = jnp.maximum(m_old, jnp.max(s_ref[buf, c, :, cols], axis=0, keepdims=True))
                alpha = jnp.exp2(m_old - m_new)
                p = jnp.exp2(s_ref[buf, c, :, cols] - m_new)
                l_ref[c, :, cols] = alpha * l_ref[c, :, cols] + jnp.sum(p, axis=0, keepdims=True)
                m_ref[c, :, cols] = m_new
                a_ref[buf, c, :, cols] = alpha
                p_ref[buf, c, :, cols] = p.astype(BF16)

    def accumulate(vt, buf):
        for c in range(2):
            acc_ref[c] = a_ref[buf, c] * acc_ref[c] + _dot(vt, p_ref[buf, c])

    for h in range(n_heads):
        hs = slice(h * 128, (h + 1) * 128)
        qt = qt_ref[hs, :]
        sub_i = lax.broadcasted_iota(jnp.int32, qt.shape, 0)
        qm_ref[0] = jnp.where(sub_i < DH_C, qt, jnp.zeros_like(qt))
        qm_ref[1] = jnp.where(sub_i >= DH_C, qt, jnp.zeros_like(qt))
        m_ref[...] = jnp.full(m_ref.shape, -1e30, F32)
        l_ref[...] = jnp.zeros_like(l_ref)
        acc_ref[...] = jnp.zeros_like(acc_ref)

        if n_new == 1:
            scores(kn_ref[:, hs], 0)
            softmax_update(0)
            accumulate(vn_ref[0, hs, :], 0)
        else:
            new_k = lambda j: kn_ref[pl.ds(pl.multiple_of(j * TM, TM), TM), hs]
            scores(new_k(0), 0)

            def body(i, carry):
                scores(new_k(2 * i + 1), 1)
                softmax_update(0)
                accumulate(vn_ref[2 * i, hs, :], 0)
                scores(new_k(2 * i + 2), 0)
                softmax_update(1)
                accumulate(vn_ref[2 * i + 1, hs, :], 1)
                return carry
            lax.fori_loop(0, n_new // 2 - 1, body, 0)
            scores(kn_ref[(n_new - 1) * TM:, hs], 1)
            softmax_update(0)
            accumulate(vn_ref[n_new - 2, hs, :], 0)
            scores(kc_ref[:, hs], 0)
            softmax_update(1)
            accumulate(vn_ref[n_new - 1, hs, :], 1)
            softmax_update(0)
            accumulate(vc_ref[0, hs, :], 0)

        for cb in range(tq // 128):
            cols = slice(cb * 128, (cb + 1) * 128)
            o = acc_ref[0, :, cols] / l_ref[0, :, cols] - lam * (acc_ref[1, :, cols] / l_ref[1, :, cols])
            y = o * lax.rsqrt(jnp.mean(o * o, axis=0, keepdims=True) + 1e-6)
            o_ref[cols, hs] = (y.T * sub_ref[...] * (1.0 - lam_init)).astype(BF16)


def _attn_scratch(tq):
    return [pltpu.VMEM((2, 128, tq), BF16), pltpu.VMEM((2, 2, TM, tq), F32), pltpu.VMEM((2, 2, TM, tq), BF16),
            pltpu.VMEM((2, 1, tq), F32), pltpu.VMEM((2, 1, tq), F32), pltpu.VMEM((2, 2, 1, tq), F32),
            pltpu.VMEM((2, 128, tq), F32)]


def _attention(l, qt, kb, vt, ck, cvt, lambda_qk_l, subln_l):
    lam_init = 0.8 - 0.6 * math.exp(-0.3 * l)
    sub = subln_l.reshape(1, 2 * DH_C)
    o_ctx = pl.pallas_call(
        functools.partial(_attn_kernel, lam_init, 1, False, H_C),
        grid=(BATCH,),
        in_specs=[pl.BlockSpec((D_C, SEQ), lambda b: (0, b)),
                  pl.BlockSpec((SEQ, D_C), lambda b: (b, 0)),
                  pl.BlockSpec((1, D_C, TM), lambda b: (b, 0, 0)),
                  pl.BlockSpec((4, DH_C), lambda b: (0, 0)),
                  pl.BlockSpec((1, 2 * DH_C), lambda b: (0, 0)),
                  pl.BlockSpec(memory_space=pl.ANY)],
        out_specs=pl.BlockSpec((SEQ, D_C), lambda b: (b, 0)),
        out_shape=jax.ShapeDtypeStruct((T, D_C), BF16),
        input_output_aliases={5: 0},
        scratch_shapes=_attn_scratch(SEQ),
        compiler_params=_cparams(("arbitrary",)),
        name="attn_ctx",
    )(qt, kb, vt, lambda_qk_l, sub, jnp.zeros((T, D_C), BF16))

    nq = DEC_SEQ // TQ
    base_q = BATCH * SEQ // TQ
    base_kv = BATCH * SEQ // DEC_SEQ
    o_smp = pl.pallas_call(
        functools.partial(_attn_kernel, lam_init, SMP_TILES, True, 1),
        grid=(DEC_BATCH, H_C, nq),
        in_specs=[pl.BlockSpec((128, TQ), lambda b, h, j: (h, base_q + b * nq + j)),
                  pl.BlockSpec((PAST, 128), lambda b, h, j: (b, h)),
                  pl.BlockSpec((1, 128, PAST), lambda b, h, j: (b, h, 0)),
                  pl.BlockSpec((DEC_SEQ, 128), lambda b, h, j: (base_kv + b, h)),
                  pl.BlockSpec((SMP_TILES, 128, TM), lambda b, h, j: (base_kv + b, h, 0)),
                  pl.BlockSpec((4, DH_C), lambda b, h, j: (0, 0)),
                  pl.BlockSpec((1, 2 * DH_C), lambda b, h, j: (0, 0)),
                  pl.BlockSpec(memory_space=pl.ANY)],
        out_specs=pl.BlockSpec((TQ, 128), lambda b, h, j: (base_q + b * nq + j, h)),
        out_shape=jax.ShapeDtypeStruct((T, D_C), BF16),
        input_output_aliases={7: 0},
        scratch_shapes=_attn_scratch(TQ),
        compiler_params=_cparams(("arbitrary", "arbitrary", "arbitrary")),
        name="attn_smp",
    )(qt, ck, cvt, kb, vt, lambda_qk_l, sub, o_ctx)
    return o_smp


def _merge_ffn_kernel(nx, *refs):
    (a_ref, bf_ref, bb_ref, hg_ref, hn_ref, c_ref, ga_ref, gb_ref, gc_ref, g1_ref, wb_ref, wo_ref,
     n2_ref, sc_ref, sh_ref, g2_ref, wi_ref, wd_ref) = refs[nx:nx + 18]
    o_refs = refs[nx + 18:]
    hb = []
    for h in range(H_B):
        hs = slice(h * DK_B, (h + 1) * DK_B)
        o = bf_ref[:, hs] + bb_ref[:, hs]
        y = o * lax.rsqrt(jnp.mean(o * o, axis=-1, keepdims=True) + 1e-6) * hn_ref[...]
        hb.append((y * _silu(hg_ref[:, hs])).astype(BF16))
    ya = _dot(a_ref[...], wb_ref[0:D_A, :])
    yb = _dot(jnp.concatenate(hb, axis=1), wb_ref[D_A:D_A + D_B, :])
    yc = _dot(c_ref[...], wb_ref[D_A + D_B:, :])
    m = _sigmoid(ga_ref[...]) * ya + _sigmoid(gb_ref[...]) * yb + _sigmoid(gc_ref[...]) * yc
    x1 = _x_tile(refs[:nx]) + g1_ref[0] * _dot(m.astype(BF16), wo_ref[...])
    h = _rms_mod(x1, n2_ref[...], sc_ref[0], sh_ref[0]).astype(BF16)
    gu = _dot(h, wi_ref[...])
    ff = _silu(gu[:, :D_FF]) * gu[:, D_FF:]
    y = x1 + g2_ref[0] * _dot(ff.astype(BF16), wd_ref[...])
    if len(o_refs) == 1:
        o_refs[0][...] = y
    else:
        @pl.when(pl.program_id(0) < CTX_TILES)
        def _():
            o_refs[0][...] = y

        @pl.when(pl.program_id(0) >= CTX_TILES)
        def _():
            o_refs[1][...] = y


def _merge_ffn(ya, of, ob, hgrn_norm_l, yc, u, x, modt, w_branch_l, w_out_l, norm2_l, w_ffn_in_l, w_ffn_out_l, l,
               split_out):
    row = lambda w: pl.BlockSpec((TM, w), lambda i: (i, 0))
    gate0 = C_GT * CB // D
    gate = lambda k: pl.BlockSpec((TM, D), lambda i: (i, gate0 + k))
    resident = lambda shape: pl.BlockSpec(shape, lambda i: (0, 0), pipeline_mode=pl.Buffered(1))
    x_specs, xs = _x_specs(x)
    if split_out:
        out_specs, _ = _x_specs((None, None))
        out_shape = [jax.ShapeDtypeStruct((BATCH * SEQ, D), F32), jax.ShapeDtypeStruct((DEC_BATCH * DEC_SEQ, D), F32)]
    else:
        out_specs = row(D)
        out_shape = jax.ShapeDtypeStruct((T, D), F32)
    return pl.pallas_call(
        functools.partial(_merge_ffn_kernel, len(xs)),
        grid=(NT,),
        in_specs=x_specs + [row(D_A), row(D_B), row(D_B), pl.BlockSpec((TM, CB), lambda i: (i, C_HG)),
                            pl.BlockSpec((1, DK_B), lambda i: (0, 0)), row(D_C),
                            gate(0), gate(1), gate(2), _mod_spec(l, 2),
                            resident((D_A + D_B + D_C, D)), resident((D, D)),
                            pl.BlockSpec((1, D), lambda i: (0, 0)),
                            _mod_spec(l, 4), _mod_spec(l, 3), _mod_spec(l, 5),
                            resident((D, 2 * D_FF)), resident((D_FF, D))],
        out_specs=out_specs,
        out_shape=out_shape,
        compiler_params=_cparams(("arbitrary",)),
        name="merge_ffn",
    )(*xs, ya, of, ob, u, hgrn_norm_l.reshape(1, DK_B), yc, u, u, u, modt, w_branch_l, w_out_l,
      norm2_l.reshape(1, D), modt, modt, modt, w_ffn_in_l, w_ffn_out_l)


def _layer(l, x, modt, cache_k, cache_v, state_hgrn, tabs, p, outs):
    (norm1, norm2, w_in, conv_w, conv_b, conv_ln_g, conv_ln_b, hgrn_lb, hgrn_norm, q_norm, k_norm, lambda_qk,
     subln, w_branch, w_out, w_ffn_in, w_ffn_out) = p
    cos_t, sin_t, bd, maps_f, maps_b = tabs
    new_k, new_v, new_s = outs
    u = _in_proj(x, modt, norm1[l], w_in[l].astype(BF16), l)
    ya = _conv_branch(u, conv_w[l], conv_b[l], conv_ln_g[l], conv_ln_b[l])
    lbp = hgrn_lb.reshape(DEPTH * 2, D_B)
    ob, new_s = _hgrn_pass(True, l, u, lbp, state_hgrn[:, l, 1], maps_b, new_s)
    of, new_s = _hgrn_pass(False, l, u, lbp, state_hgrn[:, l, 0], maps_f, new_s)
    qt, kb, vt, new_k, new_v = _qk_prep(l, u, q_norm[l], k_norm[l], cos_t, sin_t, bd, new_k, new_v)
    ck = cache_k[:, l].reshape(DEC_BATCH * PAST, D_C).astype(BF16)
    cvt = cache_v[:, l].reshape(DEC_BATCH, PAST, D_C).astype(BF16).transpose(0, 2, 1)
    yc = _attention(l, qt, kb, vt, ck, cvt, lambda_qk[l], subln[l])
    x = _merge_ffn(ya, of, ob, hgrn_norm[l], yc, u, x, modt, w_branch[l].astype(BF16), w_out[l].astype(BF16), norm2[l],
                   w_ffn_in[l].astype(BF16), w_ffn_out[l].astype(BF16), l, split_out=l == DEPTH - 1)
    return x, (new_k, new_v, new_s)


def kernel(x_prompt, x_sample, cache_k, cache_v, state_hgrn, c, c_ctx, w_mod, b_mod, norm1, norm2, w_in, conv_w,
           conv_b, conv_ln_g, conv_ln_b, hgrn_lb, hgrn_norm, q_norm, k_norm, lambda_qk, subln, w_branch, w_out,
           w_ffn_in, w_ffn_out):
    x = (x_prompt.reshape(BATCH * SEQ, D), x_sample.reshape(DEC_BATCH * DEC_SEQ, D))
    cv8 = jnp.concatenate([c_ctx[None, :], c, jnp.zeros((8 - 1 - DEC_BATCH, D), F32)], axis=0)
    mod = _modulation(cv8, w_mod, b_mod)
    modt = mod.reshape(DEPTH, 8, 6, D).transpose(0, 2, 1, 3).reshape(DEPTH * 6 * 8, 1, D)
    cos_t, sin_t = _rope_tables()
    seg = np.arange(D_C) // DH_C
    bd = jnp.asarray(seg[:, None] == seg[None, :], BF16)
    maps = lambda rev: (jnp.asarray(_level_map(rev)), jnp.asarray(_low_map(rev)), jnp.asarray(_tri_map(rev), BF16))
    tabs = (cos_t, sin_t, bd, maps(False), maps(True))
    p = (norm1, norm2, w_in, conv_w, conv_b, conv_ln_g, conv_ln_b, hgrn_lb, hgrn_norm, q_norm, k_norm, lambda_qk,
         subln, w_branch, w_out, w_ffn_in, w_ffn_out)
    outs = (jnp.zeros((BATCH, DEPTH, SEQ, D_C), F32), jnp.zeros((BATCH, DEPTH, SEQ, D_C), F32),
            jnp.zeros((BATCH, DEPTH, 2, H_B, DK_B, DK_B), F32))
    for l in range(DEPTH):
        x, outs = _layer(l, x, modt, cache_k, cache_v, state_hgrn, tabs, p, outs)
    new_k, new_v, new_s = outs
    y_prompt = x[0].reshape(BATCH, SEQ, D)
    y_sample = x[1].reshape(DEC_BATCH, DEC_SEQ, D)
    return (y_prompt, y_sample, new_k.reshape(BATCH, DEPTH, SEQ, H_C, 2, DH_C),
            new_v.reshape(BATCH, DEPTH, SEQ, H_C, 2 * DH_C), new_s)
```

```python
import functools
import math

import numpy as np
import jax
import jax.numpy as jnp
from jax import lax
from jax.experimental import pallas as pl
from jax.experimental.pallas import tpu as pltpu

F32 = jnp.float32
BF16 = jnp.bfloat16

D = 1024
BATCH, SEQ = 32, 256
DEPTH = 2
DEC_BATCH, DEC_SEQ = 2, 4096
PAST = 256
GRID_W = 64
D_A, CONV_K = 512, 31
H_B, DK_B = 4, 128
D_B = 512
H_C, DH_C = 4, 64
D_C = 512
ROPE_BASE = 10000.0
D_FF = 2816
IN_COLS = 8192

TM = 256
CTX_TILES = BATCH * SEQ // TM
SMP_TILES = DEC_SEQ // TM
T = BATCH * SEQ + DEC_BATCH * DEC_SEQ
NT = T // TM
CB = 512
HALO = 16
TQ = 512
VMEM_LIMIT = 56 * 1024 * 1024

C_A, C_AG, C_HQ, C_HI, C_HFF, C_HFB, C_HG, C_AQ, C_AK, C_AV, C_GT = 0, 1, 2, 3, 4, 5, 6, 7, 8, 9, 10


def _cparams(sem):
    return pltpu.CompilerParams(dimension_semantics=sem, vmem_limit_bytes=VMEM_LIMIT)


def _mod_row(i):
    return jnp.where(i < CTX_TILES, 0, 1 + (i - CTX_TILES) // SMP_TILES)


def _sigmoid(x):
    return 0.5 * jnp.tanh(0.5 * x) + 0.5


def _sigmoid_small_accurate(x):
    return jnp.exp(-jnp.log(1.0 + jnp.exp(-x)))


def _silu(x):
    return x * _sigmoid(x)


def _dot(a, b):
    return jnp.dot(a, b, preferred_element_type=F32)


def _dot_nt(a, b):
    return lax.dot_general(a, b, (((1,), (1,)), ((), ())), preferred_element_type=F32)


def _dot_tn(a, b):
    return lax.dot_general(a, b, (((0,), (0,)), ((), ())), preferred_element_type=F32)


def _split_bf16(x):
    hi = x.astype(BF16)
    lo = (x - hi.astype(F32)).astype(BF16)
    return hi, lo


MOD_TN = 1536


def _mod_kernel(cv_ref, w_ref, b_ref, o_ref):
    a = _silu(cv_ref[...])
    a_hi, a_lo = _split_bf16(a)
    w_hi, w_lo = _split_bf16(w_ref[0])
    acc = _dot(a_hi, w_hi) + _dot(a_lo, w_hi) + _dot(a_hi, w_lo)
    o_ref[0] = acc + b_ref[0]


def _modulation(cv8, w_mod, b_mod):
    return pl.pallas_call(
        _mod_kernel,
        grid=(DEPTH, 6 * D // MOD_TN),
        in_specs=[
            pl.BlockSpec((8, D), lambda l, j: (0, 0)),
            pl.BlockSpec((1, D, MOD_TN), lambda l, j: (l, 0, j)),
            pl.BlockSpec((1, 1, MOD_TN), lambda l, j: (l, 0, j)),
        ],
        out_specs=pl.BlockSpec((1, 8, MOD_TN), lambda l, j: (l, 0, j)),
        out_shape=jax.ShapeDtypeStruct((DEPTH, 8, 6 * D), F32),
        compiler_params=_cparams(("arbitrary", "arbitrary")),
        name="adaln_mod",
    )(cv8, w_mod, b_mod.reshape(DEPTH, 1, 6 * D))


def _mod_spec(l, which):
    return pl.BlockSpec((1, 1, D), lambda i: ((l * 6 + which) * 8 + _mod_row(i), 0, 0))


def _rms_mod(x, g, sc, sh):
    y = x * lax.rsqrt(jnp.mean(x * x, axis=-1, keepdims=True) + 1e-6) * g
    return y * (1.0 + sc) + sh


def _x_specs(x):
    if isinstance(x, tuple):
        return [pl.BlockSpec((TM, D), lambda i: (jnp.minimum(i, CTX_TILES - 1), 0)),
                pl.BlockSpec((TM, D), lambda i: (jnp.maximum(i - CTX_TILES, 0), 0))], list(x)
    return [pl.BlockSpec((TM, D), lambda i: (i, 0))], [x]


def _x_tile(x_refs):
    if len(x_refs) == 2:
        return jnp.where(pl.program_id(0) < CTX_TILES, x_refs[0][...], x_refs[1][...])
    return x_refs[0][...]


IN_TM = 512
IN_SPLIT = 2


def _in_kernel(nx, *refs):
    g_ref, sc_ref, sh_ref, w_ref, o_ref = refs[nx:]
    if nx == 2:
        x = jnp.where(pl.program_id(1) < BATCH * SEQ // IN_TM, refs[0][...], refs[1][...])
    else:
        x = refs[0][...]
    h = _rms_mod(x, g_ref[...], sc_ref[0], sh_ref[0])
    o_ref[...] = _dot(h.astype(BF16), w_ref[...])


def _in_proj(x, modt, norm1_l, w_in_l, l):
    n_ctx = BATCH * SEQ // IN_TM
    per_seq = DEC_SEQ // IN_TM
    mod_row = lambda i: jnp.where(i < n_ctx, 0, 1 + (i - n_ctx) // per_seq)
    mod = lambda which: pl.BlockSpec((1, 1, D), lambda j, i: ((l * 6 + which) * 8 + mod_row(i), 0, 0))
    if isinstance(x, tuple):
        x_specs = [pl.BlockSpec((IN_TM, D), lambda j, i: (jnp.minimum(i, n_ctx - 1), 0)),
                   pl.BlockSpec((IN_TM, D), lambda j, i: (jnp.maximum(i - n_ctx, 0), 0))]
        xs = list(x)
    else:
        x_specs, xs = [pl.BlockSpec((IN_TM, D), lambda j, i: (i, 0))], [x]
    cols = IN_COLS // IN_SPLIT
    return pl.pallas_call(
        functools.partial(_in_kernel, len(xs)),
        grid=(IN_SPLIT, T // IN_TM),
        in_specs=x_specs + [
            pl.BlockSpec((1, D), lambda j, i: (0, 0)),
            mod(1),
            mod(0),
            pl.BlockSpec((D, cols), lambda j, i: (0, j)),
        ],
        out_specs=pl.BlockSpec((IN_TM, cols), lambda j, i: (i, j)),
        out_shape=jax.ShapeDtypeStruct((T, IN_COLS), F32),
        compiler_params=_cparams(("arbitrary", "arbitrary")),
        name="in_proj",
    )(*xs, norm1_l.reshape(1, D), modt, modt, w_in_l)


CONV_RB = 32


def _conv_kernel(ac_ref, gc_ref, ap_ref, gp_ref, an_ref, gn_ref, w_ref, b_ref, lg_ref, lb_ref, o_ref, sh_ref,
                 cv_ref):
    i = pl.program_id(0)
    in_smp = i >= CTX_TILES
    pos = i % SMP_TILES
    has_prev = jnp.logical_and(in_smp, pos != 0)
    has_next = jnp.logical_and(in_smp, pos != SMP_TILES - 1)
    n = TM + 2 * HALO
    sh_ref[0, HALO:HALO + TM, :] = ac_ref[...] * _sigmoid(gc_ref[...])
    sh_ref[0, 0:HALO, :] = jnp.where(has_prev, ap_ref[...] * _sigmoid(gp_ref[...]), 0.0)
    sh_ref[0, HALO + TM:, :] = jnp.where(has_next, an_ref[...] * _sigmoid(gn_ref[...]), 0.0)
    for r in range(1, 8):
        sh_ref[r, 0:n - 8, :] = sh_ref[0, r:r + n - 8, :]
    off = HALO - CONV_K // 2

    def body(rb, carry):
        base = pl.multiple_of(rb * CONV_RB, CONV_RB)
        acc = jnp.zeros((CONV_RB, D_A), F32)
        for k in range(CONV_K):
            r = (off + k) % 8
            acc = acc + sh_ref[r, pl.ds(base + (off + k - r), CONV_RB), :] * w_ref[k:k + 1, :]
        cv_ref[pl.ds(base, CONV_RB), :] = acc
        return carry
    lax.fori_loop(0, TM // CONV_RB, body, 0)
    acc = cv_ref[...] + b_ref[...]
    mu = jnp.mean(acc, axis=-1, keepdims=True)
    xc = acc - mu
    y = xc * lax.rsqrt(jnp.mean(xc * xc, axis=-1, keepdims=True) + 1e-5)
    y = y * lg_ref[...] + lb_ref[...]
    o_ref[...] = _silu(y).astype(BF16)


def _conv_branch(u, conv_w_l, conv_b_l, ln_g_l, ln_b_l):
    r = TM // HALO
    last = T // HALO - 1
    cur = lambda c: pl.BlockSpec((TM, CB), lambda i: (i, c))
    prev = lambda c: pl.BlockSpec((HALO, CB), lambda i: (jnp.maximum(i * r - 1, 0), c))
    nxt = lambda c: pl.BlockSpec((HALO, CB), lambda i: (jnp.minimum((i + 1) * r, last), c))
    vec = pl.BlockSpec((1, D_A), lambda i: (0, 0))
    return pl.pallas_call(
        _conv_kernel,
        grid=(NT,),
        in_specs=[cur(C_A), cur(C_AG), prev(C_A), prev(C_AG), nxt(C_A), nxt(C_AG),
                  pl.BlockSpec((CONV_K, D_A), lambda i: (0, 0)), vec, vec, vec],
        out_specs=pl.BlockSpec((TM, D_A), lambda i: (i, 0)),
        out_shape=jax.ShapeDtypeStruct((T, D_A), BF16),
        scratch_shapes=[pltpu.VMEM((8, TM + 2 * HALO, D_A), F32), pltpu.VMEM((TM, D_A), F32)],
        compiler_params=_cparams(("arbitrary",)),
        name="conv_branch",
    )(u, u, u, u, u, u, conv_w_l, conv_b_l.reshape(1, D_A), ln_g_l.reshape(1, D_A), ln_b_l.reshape(1, D_A))


N_LEVELS = 8


def _level_map(reverse):
    t = np.arange(TM)[:, None]
    s = np.arange(TM)[None, :]
    x = t ^ s
    lv = np.zeros((TM, TM), np.int32)
    for bit in range(N_LEVELS):
        lv[(x >> bit) == 1] = bit + 1
    ok = (t < s) if reverse else (t > s)
    return np.where(ok, lv, 0).astype(np.int32)


def _shift_rows(x, k):
    return pltpu.roll(x, k % TM, 0)


def _cumsum_rows(x, row, reverse):
    sh = 1
    while sh < TM:
        if reverse:
            x = x + jnp.where(row < TM - sh, _shift_rows(x, -sh), 0.0)
        else:
            x = x + jnp.where(row >= sh, _shift_rows(x, sh), 0.0)
        sh *= 2
    return x


def _anchors(cum, row, reverse):
    out = []
    z = cum
    for bit in range(N_LEVELS):
        h = 1 << bit
        if 2 * h >= 16:
            c3 = cum.reshape(TM // (2 * h), 2 * h, DK_B)
            p = h if reverse else h - 1
            out.append(jnp.broadcast_to(c3[:, p:p + 1, :], c3.shape).reshape(TM, DK_B))
            continue
        hi = (row & h) != 0
        if reverse:
            out.append(jnp.where(hi, z, _shift_rows(z, -h)))
            z = jnp.where(hi, _shift_rows(z, h), z)
        else:
            out.append(jnp.where(hi, _shift_rows(z, h), z))
            z = jnp.where(hi, z, _shift_rows(z, -h))
    return out


LOW_BITS = 5
SAFE_RANGE = 60.0


def _low_map(reverse):
    t = np.arange(TM)[:, None]
    s = np.arange(TM)[None, :]
    same = (t >> LOW_BITS) == (s >> LOW_BITS)
    ok = (t <= s) if reverse else (t >= s)
    return (same & ok).astype(np.int32)


def _block_anchor(cum, reverse):
    n = 1 << LOW_BITS
    c3 = cum.reshape(TM // n, n, DK_B)
    p = n - 1 if reverse else 0
    return jnp.broadcast_to(c3[:, p:p + 1, :], c3.shape).reshape(TM, DK_B)


def _hgrn_kernel(reverse, l, *refs):
    (q_ref, v_ref, f_ref, lbp_ref, s0_ref, lv_ref, lo_ref, _, o_ref, ns_ref) = refs[:10]
    st_ref, qs_ref, ks_ref, cs_ref, os_ref, as_ref = refs[-6:]
    g = pl.program_id(0)
    i = NT - 1 - g if reverse else g
    is_ctx = i < CTX_TILES
    pos = i % SMP_TILES
    seq_start = jnp.logical_and(i >= CTX_TILES, pos == (SMP_TILES - 1 if reverse else 0))

    @pl.when(is_ctx)
    def _():
        st_ref[...] = jnp.zeros_like(st_ref)

    @pl.when(seq_start)
    def _():
        for h in range(H_B):
            st_ref[h] = s0_ref[0, h].T

    d = 1 if reverse else 0
    rows = [lbp_ref[k * 2 + d:k * 2 + d + 1, :] for k in range(DEPTH)]
    mx = functools.reduce(jnp.maximum, rows)
    ex = [jnp.exp(r - mx) for r in rows]
    den = functools.reduce(jnp.add, ex)
    lb_all = jnp.zeros_like(mx)
    for k in range(1, l + 1):
        lb_all = lb_all + ex[k] / den

    row = lax.broadcasted_iota(jnp.int32, (TM, DK_B), 0)
    spread = jnp.zeros((1, DK_B), F32)
    for h in range(H_B):
        sl = slice(h * DK_B, (h + 1) * DK_B)
        lb = lb_all[:, sl]
        fg = lb + (1.0 - lb) * _sigmoid_small_accurate(f_ref[:, sl])
        cum = _cumsum_rows(jnp.log(fg), row, reverse)
        ks_ref[h] = 1.0 - fg
        qs_ref[h] = _silu(q_ref[:, sl])
        cs_ref[h] = cum
        spread = jnp.maximum(spread, jnp.max(_block_anchor(cum, reverse) - cum, axis=0, keepdims=True))
    single_anchor_ok = jnp.max(spread) < SAFE_RANGE

    half = TM // 2
    halves = (slice(0, half), slice(half, TM))
    late, early = (halves[0], halves[1]) if reverse else (halves[1], halves[0])

    lv = lv_ref[0:half, 0:half]

    def operands(h, fq, fk):
        return (qs_ref[h] * fq).astype(BF16), (ks_ref[h] * fk).astype(BF16)

    for h in range(H_B):
        sl = slice(h * DK_B, (h + 1) * DK_B)
        q, kg, cum = qs_ref[h], ks_ref[h], cs_ref[h]
        vb = v_ref[:, sl].astype(BF16)
        tot = cum[0:1, :] if reverse else cum[TM - 1:TM, :]
        st = st_ref[h]
        o = _dot_nt((q * jnp.exp(cum)).astype(BF16), st.astype(BF16))
        anchors = _anchors(cum, row, reverse)
        fac = jnp.exp(-jnp.abs(cum - anchors[N_LEVELS - 1]))
        qf, kf = operands(h, fac, fac)
        o_top = _dot(_dot_nt(qf[late], kf[early]).astype(BF16), vb[early])
        zero = jnp.zeros_like(o_top)
        os_ref[h] = o + jnp.concatenate([o_top, zero] if reverse else [zero, o_top], axis=0)
        a = [jnp.zeros((half, half), F32) for _ in halves]
        for bit in range(LOW_BITS, N_LEVELS - 1):
            fac = jnp.exp(-jnp.abs(cum - anchors[bit]))
            qf, kf = operands(h, fac, fac)
            a = [jnp.where(lv == bit + 1, _dot_nt(qf[r], kf[r]), a[b]) for b, r in enumerate(halves)]
        for b in range(2):
            as_ref[h, b] = a[b]
        kbar = (kg * jnp.exp(tot - cum)).astype(BF16)
        st_ref[h] = jnp.exp(tot) * st + _dot_tn(vb, kbar)

    def heads(single_anchor):
        for h in range(H_B):
            sl = slice(h * DK_B, (h + 1) * DK_B)
            cum = cs_ref[h]
            v = v_ref[:, sl]
            vb = v.astype(BF16)
            o = os_ref[h]
            a = [as_ref[h, b] for b in range(2)]
            if single_anchor:
                e = _block_anchor(cum, reverse)
                qf, kf = operands(h, jnp.exp(cum - e), jnp.exp(e - cum))
                lo = lo_ref[0:half, 0:half] != 0
                a = [jnp.where(lo, _dot_nt(qf[r], kf[r]), a[b]) for b, r in enumerate(halves)]
            else:
                o = o + jnp.sum(qs_ref[h] * ks_ref[h], axis=-1, keepdims=True) * v
                anchors = _anchors(cum, row, reverse)
                for bit in range(LOW_BITS):
                    fac = jnp.exp(-jnp.abs(cum - anchors[bit]))
                    qf, kf = operands(h, fac, fac)
                    a = [jnp.where(lv == bit + 1, _dot_nt(qf[r], kf[r]), a[b]) for b, r in enumerate(halves)]
            o_ref[:, sl] = o + jnp.concatenate([_dot(a[b].astype(BF16), vb[r]) for b, r in enumerate(halves)],
                                               axis=0)

    @pl.when(single_anchor_ok)
    def _():
        heads(True)

    @pl.when(jnp.logical_not(single_anchor_ok))
    def _():
        heads(False)

    @pl.when(is_ctx)
    def _():
        for h in range(H_B):
            ns_ref[0, 0, 0, h] = st_ref[h].T


def _hgrn_pass(reverse, l, u, hgrn_lb, s0, maps, new_state):
    tile = (lambda g: NT - 1 - g) if reverse else (lambda g: g)
    col = lambda c: pl.BlockSpec((TM, CB), lambda g: (tile(g), c))
    s0_spec = pl.BlockSpec((1, H_B, DK_B, DK_B),
                           lambda g: (jnp.clip((tile(g) - CTX_TILES) // SMP_TILES, 0, DEC_BATCH - 1), 0, 0, 0))
    lbp_spec = pl.BlockSpec((DEPTH * 2, D_B), lambda g: (0, 0))
    map_spec = pl.BlockSpec((TM, TM), lambda g: (0, 0))
    d = 1 if reverse else 0
    ns_spec = pl.BlockSpec((1, 1, 1, H_B, DK_B, DK_B),
                           lambda g: (jnp.minimum(tile(g), CTX_TILES - 1), l, d, 0, 0, 0))
    any_spec = pl.BlockSpec(memory_space=pl.ANY)
    f_col = C_HFB if reverse else C_HFF
    in_specs = [col(C_HQ), col(C_HI), col(f_col), lbp_spec, s0_spec, map_spec, map_spec, any_spec]
    args = (u, u, u, hgrn_lb, s0, *maps, new_state)
    o_shape = jax.ShapeDtypeStruct((T, D_B), F32)
    head_tile = pltpu.VMEM((H_B, TM, DK_B), F32)
    return pl.pallas_call(
        functools.partial(_hgrn_kernel, reverse, l),
        grid=(NT,),
        in_specs=in_specs,
        out_specs=[pl.BlockSpec((TM, D_B), lambda g: (tile(g), 0)), ns_spec],
        out_shape=[o_shape, jax.ShapeDtypeStruct(new_state.shape, F32)],
        input_output_aliases={len(args) - 1: 1},
        scratch_shapes=[pltpu.VMEM((H_B, DK_B, DK_B), F32), head_tile, head_tile, head_tile, head_tile,
                        pltpu.VMEM((H_B, 2, TM // 2, TM // 2), F32)],
        compiler_params=_cparams(("arbitrary",)),
        name="hgrn_bwd" if reverse else "hgrn_fwd",
    )(*args)


def _qk_kernel(aq_ref, ak_ref, av_ref, qn_ref, kn_ref, cos_ref, sin_ref, bd_ref, _k, _v, qt_ref, kb_ref, vt_ref,
               nk_ref, nv_ref):
    bd = bd_ref[...]
    lane = lax.broadcasted_iota(jnp.int32, (TM, 128), 1)
    even = (lane & 1) == 0
    cos = cos_ref[...]
    sin = sin_ref[...]

    def seg_norm(x, g):
        ss = _dot((x * x).astype(BF16), bd)
        return x * lax.rsqrt(ss * (1.0 / DH_C) + 1e-6) * g

    def rope(x):
        parts = []
        for s in range(D_C // 128):
            xs = x[:, s * 128:(s + 1) * 128]
            sw = jnp.where(even, pltpu.roll(xs, 127, 1), pltpu.roll(xs, 1, 1))
            parts.append(xs * cos + sw * sin)
        return jnp.concatenate(parts, axis=1)

    q = seg_norm(aq_ref[...], qn_ref[...])
    k = seg_norm(ak_ref[...], kn_ref[...])
    @pl.when(pl.program_id(0) < CTX_TILES)
    def _():
        nk_ref[0, 0] = k
        nv_ref[0, 0] = av_ref[...]

    qt_ref[...] = (rope(q) * (DH_C ** -0.5 * math.log2(math.e))).T.astype(BF16)
    kb_ref[...] = rope(k).astype(BF16)
    vt_ref[0] = av_ref[...].T.astype(BF16)


def _rope_tables():
    rows = DEC_SEQ // GRID_W
    row = jnp.broadcast_to(jnp.arange(rows)[:, None], (rows, GRID_W)).reshape(-1).astype(F32)
    colp = jnp.broadcast_to(jnp.arange(GRID_W)[None, :], (rows, GRID_W)).reshape(-1).astype(F32)
    half = DH_C // 2
    inv = ROPE_BASE ** (-jnp.arange(0, half, 2, dtype=F32) / half)
    ang = jnp.concatenate([row[:, None] * inv, colp[:, None] * inv], axis=-1)
    cos = jnp.repeat(jnp.cos(ang), 2, axis=-1)
    sin = jnp.repeat(jnp.sin(ang), 2, axis=-1) * jnp.tile(jnp.array([-1.0, 1.0], F32), half)
    cos = jnp.concatenate([jnp.ones((TM, DH_C), F32), cos], axis=0)
    sin = jnp.concatenate([jnp.zeros((TM, DH_C), F32), sin], axis=0)
    return jnp.tile(cos, (1, 2)), jnp.tile(sin, (1, 2))


def _qk_prep(l, u, q_norm_l, k_norm_l, cos_t, sin_t, bd, new_k, new_v):
    col = lambda c: pl.BlockSpec((TM, CB), lambda i: (i, c))
    vec = pl.BlockSpec((1, D_C), lambda i: (0, 0))
    tab = pl.BlockSpec((TM, 128), lambda i: (jnp.where(i < CTX_TILES, 0, 1 + i % SMP_TILES), 0))
    any_spec = pl.BlockSpec(memory_space=pl.ANY)
    cache = pl.BlockSpec((1, 1, SEQ, D_C), lambda i: (jnp.minimum(i, CTX_TILES - 1), l, 0, 0))
    return pl.pallas_call(
        _qk_kernel,
        grid=(NT,),
        in_specs=[col(C_AQ), col(C_AK), col(C_AV), vec, vec, tab, tab,
                  pl.BlockSpec((D_C, D_C), lambda i: (0, 0)), any_spec, any_spec],
        out_specs=[pl.BlockSpec((D_C, TM), lambda i: (0, i)), pl.BlockSpec((TM, D_C), lambda i: (i, 0)),
                   pl.BlockSpec((1, D_C, TM), lambda i: (i, 0, 0)), cache, cache],
        out_shape=[jax.ShapeDtypeStruct((D_C, T), BF16), jax.ShapeDtypeStruct((T, D_C), BF16),
                   jax.ShapeDtypeStruct((NT, D_C, TM), BF16),
                   jax.ShapeDtypeStruct(new_k.shape, F32), jax.ShapeDtypeStruct(new_v.shape, F32)],
        input_output_aliases={8: 3, 9: 4},
        compiler_params=_cparams(("arbitrary",)),
        name="qk_prep",
    )(u, u, u, jnp.tile(q_norm_l, D_C // DH_C).reshape(1, D_C), jnp.tile(k_norm_l, D_C // DH_C).reshape(1, D_C),
      cos_t, sin_t, bd, new_k, new_v)


def _attn_kernel(lam_init, n_new, has_cache, n_heads, *refs):
    if has_cache:
        qt_ref, kc_ref, vc_ref, kn_ref, vn_ref, lam_ref, sub_ref, _, o_ref = refs[:9]
    else:
        qt_ref, kn_ref, vn_ref, lam_ref, sub_ref, _, o_ref = refs[:7]
    qm_ref, s_ref, p_ref, m_ref, l_ref, a_ref, acc_ref = refs[-7:]
    tq = qt_ref.shape[1]
    lq = lam_ref[...]
    lam = (jnp.exp(jnp.sum(lq[0:1] * lq[1:2], axis=-1, keepdims=True))
           - jnp.exp(jnp.sum(lq[2:3] * lq[3:4], axis=-1, keepdims=True)) + lam_init)

    def scores(k, buf):
        for c in range(2):
            s_ref[buf, c] = _dot(k, qm_ref[c])

    def softmax_update(buf):
        for c in range(2):
            for cb in range(tq // 128):
                cols = slice(cb * 128, (cb + 1) * 128)
                m_old = m_ref[c, :, cols]
                m_new = jnp.maximum(m_old, jnp.max(s_ref[buf, c, :, cols], axis=0, keepdims=True))
                alpha = jnp.exp2(m_old - m_new)
                p = jnp.exp2(s_ref[buf, c, :, cols] - m_new)
                l_ref[c, :, cols] = alpha * l_ref[c, :, cols] + jnp.sum(p, axis=0, keepdims=True)
                m_ref[c, :, cols] = m_new
                a_ref[buf, c, :, cols] = alpha
                p_ref[buf, c, :, cols] = p.astype(BF16)

    def accumulate(vt, buf):
        for c in range(2):
            acc_ref[c] = a_ref[buf, c] * acc_ref[c] + _dot(vt, p_ref[buf, c])

    for h in range(n_heads):
        hs = slice(h * 128, (h + 1) * 128)
        qt = qt_ref[hs, :]
        sub_i = lax.broadcasted_iota(jnp.int32, qt.shape, 0)
        qm_ref[0] = jnp.where(sub_i < DH_C, qt, jnp.zeros_like(qt))
        qm_ref[1] = jnp.where(sub_i >= DH_C, qt, jnp.zeros_like(qt))
        m_ref[...] = jnp.full(m_ref.shape, -1e30, F32)
        l_ref[...] = jnp.zeros_like(l_ref)
        acc_ref[...] = jnp.zeros_like(acc_ref)

        if n_new == 1:
            scores(kn_ref[:, hs], 0)
            softmax_update(0)
            accumulate(vn_ref[0, hs, :], 0)
        else:
            new_k = lambda j: kn_ref[pl.ds(pl.multiple_of(j * TM, TM), TM), hs]
            scores(new_k(0), 0)

            def body(i, carry):
                scores(new_k(2 * i + 1), 1)
                softmax_update(0)
                accumulate(vn_ref[2 * i, hs, :], 0)
                scores(new_k(2 * i + 2), 0)
                softmax_update(1)
                accumulate(vn_ref[2 * i + 1, hs, :], 1)
                return carry
            lax.fori_loop(0, n_new // 2 - 1, body, 0)
            scores(kn_ref[(n_new - 1) * TM:, hs], 1)
            softmax_update(0)
            accumulate(vn_ref[n_new - 2, hs, :], 0)
            scores(kc_ref[:, hs], 0)
            softmax_update(1)
            accumulate(vn_ref[n_new - 1, hs, :], 1)
            softmax_update(0)
            accumulate(vc_ref[0, hs, :], 0)

        for cb in range(tq // 128):
            cols = slice(cb * 128, (cb + 1) * 128)
            o = acc_ref[0, :, cols] / l_ref[0, :, cols] - lam * (acc_ref[1, :, cols] / l_ref[1, :, cols])
            y = o * lax.rsqrt(jnp.mean(o * o, axis=0, keepdims=True) + 1e-6)
            o_ref[cols, hs] = (y.T * sub_ref[...] * (1.0 - lam_init)).astype(BF16)


def _attn_scratch(tq):
    return [pltpu.VMEM((2, 128, tq), BF16), pltpu.VMEM((2, 2, TM, tq), F32), pltpu.VMEM((2, 2, TM, tq), BF16),
            pltpu.VMEM((2, 1, tq), F32), pltpu.VMEM((2, 1, tq), F32), pltpu.VMEM((2, 2, 1, tq), F32),
            pltpu.VMEM((2, 128, tq), F32)]


def _attention(l, qt, kb, vt, ck, cvt, lambda_qk_l, subln_l):
    lam_init = 0.8 - 0.6 * math.exp(-0.3 * l)
    sub = subln_l.reshape(1, 2 * DH_C)
    o_ctx = pl.pallas_call(
        functools.partial(_attn_kernel, lam_init, 1, False, H_C),
        grid=(BATCH,),
        in_specs=[pl.BlockSpec((D_C, SEQ), lambda b: (0, b)),
                  pl.BlockSpec((SEQ, D_C), lambda b: (b, 0)),
                  pl.BlockSpec((1, D_C, TM), lambda b: (b, 0, 0)),
                  pl.BlockSpec((4, DH_C), lambda b: (0, 0)),
                  pl.BlockSpec((1, 2 * DH_C), lambda b: (0, 0)),
                  pl.BlockSpec(memory_space=pl.ANY)],
        out_specs=pl.BlockSpec((SEQ, D_C), lambda b: (b, 0)),
        out_shape=jax.ShapeDtypeStruct((T, D_C), BF16),
        input_output_aliases={5: 0},
        scratch_shapes=_attn_scratch(SEQ),
        compiler_params=_cparams(("arbitrary",)),
        name="attn_ctx",
    )(qt, kb, vt, lambda_qk_l, sub, jnp.zeros((T, D_C), BF16))

    nq = DEC_SEQ // TQ
    base_q = BATCH * SEQ // TQ
    base_kv = BATCH * SEQ // DEC_SEQ
    o_smp = pl.pallas_call(
        functools.partial(_attn_kernel, lam_init, SMP_TILES, True, 1),
        grid=(DEC_BATCH, H_C, nq),
        in_specs=[pl.BlockSpec((128, TQ), lambda b, h, j: (h, base_q + b * nq + j)),
                  pl.BlockSpec((PAST, 128), lambda b, h, j: (b, h)),
                  pl.BlockSpec((1, 128, PAST), lambda b, h, j: (b, h, 0)),
                  pl.BlockSpec((DEC_SEQ, 128), lambda b, h, j: (base_kv + b, h)),
                  pl.BlockSpec((SMP_TILES, 128, TM), lambda b, h, j: (base_kv + b, h, 0)),
                  pl.BlockSpec((4, DH_C), lambda b, h, j: (0, 0)),
                  pl.BlockSpec((1, 2 * DH_C), lambda b, h, j: (0, 0)),
                  pl.BlockSpec(memory_space=pl.ANY)],
        out_specs=pl.BlockSpec((TQ, 128), lambda b, h, j: (base_q + b * nq + j, h)),
        out_shape=jax.ShapeDtypeStruct((T, D_C), BF16),
        input_output_aliases={7: 0},
        scratch_shapes=_attn_scratch(TQ),
        compiler_params=_cparams(("arbitrary", "arbitrary", "arbitrary")),
        name="attn_smp",
    )(qt, ck, cvt, kb, vt, lambda_qk_l, sub, o_ctx)
    return o_smp


def _merge_ffn_kernel(nx, *refs):
    (a_ref, bf_ref, bb_ref, hg_ref, hn_ref, c_ref, ga_ref, gb_ref, gc_ref, g1_ref, wb_ref, wo_ref,
     n2_ref, sc_ref, sh_ref, g2_ref, wi_ref, wd_ref) = refs[nx:nx + 18]
    o_refs = refs[nx + 18:]
    hb = []
    for h in range(H_B):
        hs = slice(h * DK_B, (h + 1) * DK_B)
        o = bf_ref[:, hs] + bb_ref[:, hs]
        y = o * lax.rsqrt(jnp.mean(o * o, axis=-1, keepdims=True) + 1e-6) * hn_ref[...]
        hb.append((y * _silu(hg_ref[:, hs])).astype(BF16))
    ya = _dot(a_ref[...], wb_ref[0:D_A, :])
    yb = _dot(jnp.concatenate(hb, axis=1), wb_ref[D_A:D_A + D_B, :])
    yc = _dot(c_ref[...], wb_ref[D_A + D_B:, :])
    m = _sigmoid(ga_ref[...]) * ya + _sigmoid(gb_ref[...]) * yb + _sigmoid(gc_ref[...]) * yc
    x1 = _x_tile(refs[:nx]) + g1_ref[0] * _dot(m.astype(BF16), wo_ref[...])
    h = _rms_mod(x1, n2_ref[...], sc_ref[0], sh_ref[0]).astype(BF16)
    gu = _dot(h, wi_ref[...])
    ff = _silu(gu[:, :D_FF]) * gu[:, D_FF:]
    y = x1 + g2_ref[0] * _dot(ff.astype(BF16), wd_ref[...])
    if len(o_refs) == 1:
        o_refs[0][...] = y
    else:
        @pl.when(pl.program_id(0) < CTX_TILES)
        def _():
            o_refs[0][...] = y

        @pl.when(pl.program_id(0) >= CTX_TILES)
        def _():
            o_refs[1][...] = y


def _merge_ffn(ya, of, ob, hgrn_norm_l, yc, u, x, modt, w_branch_l, w_out_l, norm2_l, w_ffn_in_l, w_ffn_out_l, l,
               split_out):
    row = lambda w: pl.BlockSpec((TM, w), lambda i: (i, 0))
    gate0 = C_GT * CB // D
    gate = lambda k: pl.BlockSpec((TM, D), lambda i: (i, gate0 + k))
    resident = lambda shape: pl.BlockSpec(shape, lambda i: (0, 0), pipeline_mode=pl.Buffered(1))
    x_specs, xs = _x_specs(x)
    if split_out:
        out_specs, _ = _x_specs((None, None))
        out_shape = [jax.ShapeDtypeStruct((BATCH * SEQ, D), F32), jax.ShapeDtypeStruct((DEC_BATCH * DEC_SEQ, D), F32)]
    else:
        out_specs = row(D)
        out_shape = jax.ShapeDtypeStruct((T, D), F32)
    return pl.pallas_call(
        functools.partial(_merge_ffn_kernel, len(xs)),
        grid=(NT,),
        in_specs=x_specs + [row(D_A), row(D_B), row(D_B), pl.BlockSpec((TM, CB), lambda i: (i, C_HG)),
                            pl.BlockSpec((1, DK_B), lambda i: (0, 0)), row(D_C),
                            gate(0), gate(1), gate(2), _mod_spec(l, 2),
                            resident((D_A + D_B + D_C, D)), resident((D, D)),
                            pl.BlockSpec((1, D), lambda i: (0, 0)),
                            _mod_spec(l, 4), _mod_spec(l, 3), _mod_spec(l, 5),
                            resident((D, 2 * D_FF)), resident((D_FF, D))],
        out_specs=out_specs,
        out_shape=out_shape,
        compiler_params=_cparams(("arbitrary",)),
        name="merge_ffn",
    )(*xs, ya, of, ob, u, hgrn_norm_l.reshape(1, DK_B), yc, u, u, u, modt, w_branch_l, w_out_l,
      norm2_l.reshape(1, D), modt, modt, modt, w_ffn_in_l, w_ffn_out_l)


def _layer(l, x, modt, cache_k, cache_v, state_hgrn, tabs, p, outs):
    (norm1, norm2, w_in, conv_w, conv_b, conv_ln_g, conv_ln_b, hgrn_lb, hgrn_norm, q_norm, k_norm, lambda_qk,
     subln, w_branch, w_out, w_ffn_in, w_ffn_out) = p
    cos_t, sin_t, bd, maps_f, maps_b = tabs
    new_k, new_v, new_s = outs
    u = _in_proj(x, modt, norm1[l], w_in[l].astype(BF16), l)
    ya = _conv_branch(u, conv_w[l], conv_b[l], conv_ln_g[l], conv_ln_b[l])
    lbp = hgrn_lb.reshape(DEPTH * 2, D_B)
    ob, new_s = _hgrn_pass(True, l, u, lbp, state_hgrn[:, l, 1], maps_b, new_s)
    of, new_s = _hgrn_pass(False, l, u, lbp, state_hgrn[:, l, 0], maps_f, new_s)
    qt, kb, vt, new_k, new_v = _qk_prep(l, u, q_norm[l], k_norm[l], cos_t, sin_t, bd, new_k, new_v)
    ck = cache_k[:, l].reshape(DEC_BATCH * PAST, D_C).astype(BF16)
    cvt = cache_v[:, l].reshape(DEC_BATCH, PAST, D_C).astype(BF16).transpose(0, 2, 1)
    yc = _attention(l, qt, kb, vt, ck, cvt, lambda_qk[l], subln[l])
    x = _merge_ffn(ya, of, ob, hgrn_norm[l], yc, u, x, modt, w_branch[l].astype(BF16), w_out[l].astype(BF16), norm2[l],
                   w_ffn_in[l].astype(BF16), w_ffn_out[l].astype(BF16), l, split_out=l == DEPTH - 1)
    return x, (new_k, new_v, new_s)


def kernel(x_prompt, x_sample, cache_k, cache_v, state_hgrn, c, c_ctx, w_mod, b_mod, norm1, norm2, w_in, conv_w,
           conv_b, conv_ln_g, conv_ln_b, hgrn_lb, hgrn_norm, q_norm, k_norm, lambda_qk, subln, w_branch, w_out,
           w_ffn_in, w_ffn_out):
    x = (x_prompt.reshape(BATCH * SEQ, D), x_sample.reshape(DEC_BATCH * DEC_SEQ, D))
    cv8 = jnp.concatenate([c_ctx[None, :], c, jnp.zeros((8 - 1 - DEC_BATCH, D), F32)], axis=0)
    mod = _modulation(cv8, w_mod, b_mod)
    modt = mod.reshape(DEPTH, 8, 6, D).transpose(0, 2, 1, 3).reshape(DEPTH * 6 * 8, 1, D)
    cos_t, sin_t = _rope_tables()
    seg = np.arange(D_C) // DH_C
    bd = jnp.asarray(seg[:, None] == seg[None, :], BF16)
    maps = lambda rev: (jnp.asarray(_level_map(rev)), jnp.asarray(_low_map(rev)))
    tabs = (cos_t, sin_t, bd, maps(False), maps(True))
    p = (norm1, norm2, w_in, conv_w, conv_b, conv_ln_g, conv_ln_b, hgrn_lb, hgrn_norm, q_norm, k_norm, lambda_qk,
         subln, w_branch, w_out, w_ffn_in, w_ffn_out)
    outs = (jnp.zeros((BATCH, DEPTH, SEQ, D_C), F32), jnp.zeros((BATCH, DEPTH, SEQ, D_C), F32),
            jnp.zeros((BATCH, DEPTH, 2, H_B, DK_B, DK_B), F32))
    for l in range(DEPTH):
        x, outs = _layer(l, x, modt, cache_k, cache_v, state_hgrn, tabs, p, outs)
    new_k, new_v, new_s = outs
    y_prompt = x[0].reshape(BATCH, SEQ, D)
    y_sample = x[1].reshape(DEC_BATCH, DEC_SEQ, D)
    return (y_prompt, y_sample, new_k.reshape(BATCH, DEPTH, SEQ, H_C, 2, DH_C),
            new_v.reshape(BATCH, DEPTH, SEQ, H_C, 2 * DH_C), new_s)
```

```python
import functools
import math

import numpy as np
import jax
import jax.numpy as jnp
from jax import lax
from jax.experimental import pallas as pl
from jax.experimental.pallas import tpu as pltpu

F32 = jnp.float32
BF16 = jnp.bfloat16

D = 1024
BATCH, SEQ = 32, 256
DEPTH = 2
DEC_BATCH, DEC_SEQ = 2, 4096
PAST = 256
GRID_W = 64
D_A, CONV_K = 512, 31
H_B, DK_B = 4, 128
D_B = 512
H_C, DH_C = 4, 64
D_C = 512
ROPE_BASE = 10000.0
D_FF = 2816
IN_COLS = 8192

TM = 256
CTX_TILES = BATCH * SEQ // TM
SMP_TILES = DEC_SEQ // TM
T = BATCH * SEQ + DEC_BATCH * DEC_SEQ
NT = T // TM
CB = 512
HALO = 16
TQ = 512
VMEM_LIMIT = 56 * 1024 * 1024

C_A, C_AG, C_HQ, C_HI, C_HFF, C_HFB, C_HG, C_AQ, C_AK, C_AV, C_GT = 0, 1, 2, 3, 4, 5, 6, 7, 8, 9, 10


def _cparams(sem):
    return pltpu.CompilerParams(dimension_semantics=sem, vmem_limit_bytes=VMEM_LIMIT)


def _mod_row(i):
    return jnp.where(i < CTX_TILES, 0, 1 + (i - CTX_TILES) // SMP_TILES)


def _sigmoid(x):
    return 0.5 * jnp.tanh(0.5 * x) + 0.5


def _sigmoid_small_accurate(x):
    return jnp.exp(-jnp.log(1.0 + jnp.exp(-x)))


def _silu(x):
    return x * _sigmoid(x)


def _dot(a, b):
    return jnp.dot(a, b, preferred_element_type=F32)


def _dot_nt(a, b):
    return lax.dot_general(a, b, (((1,), (1,)), ((), ())), preferred_element_type=F32)


def _dot_tn(a, b):
    return lax.dot_general(a, b, (((0,), (0,)), ((), ())), preferred_element_type=F32)


def _split_bf16(x):
    hi = x.astype(BF16)
    lo = (x - hi.astype(F32)).astype(BF16)
    return hi, lo


MOD_TN = 1536


def _mod_kernel(cv_ref, w_ref, b_ref, o_ref):
    a = _silu(cv_ref[...])
    a_hi, a_lo = _split_bf16(a)
    w_hi, w_lo = _split_bf16(w_ref[0])
    acc = _dot(a_hi, w_hi) + _dot(a_lo, w_hi) + _dot(a_hi, w_lo)
    o_ref[0] = acc + b_ref[0]


def _modulation(cv8, w_mod, b_mod):
    return pl.pallas_call(
        _mod_kernel,
        grid=(DEPTH, 6 * D // MOD_TN),
        in_specs=[
            pl.BlockSpec((8, D), lambda l, j: (0, 0)),
            pl.BlockSpec((1, D, MOD_TN), lambda l, j: (l, 0, j)),
            pl.BlockSpec((1, 1, MOD_TN), lambda l, j: (l, 0, j)),
        ],
        out_specs=pl.BlockSpec((1, 8, MOD_TN), lambda l, j: (l, 0, j)),
        out_shape=jax.ShapeDtypeStruct((DEPTH, 8, 6 * D), F32),
        compiler_params=_cparams(("arbitrary", "arbitrary")),
        name="adaln_mod",
    )(cv8, w_mod, b_mod.reshape(DEPTH, 1, 6 * D))


def _mod_spec(l, which):
    return pl.BlockSpec((1, 1, D), lambda i: ((l * 6 + which) * 8 + _mod_row(i), 0, 0))


def _rms_mod(x, g, sc, sh):
    y = x * lax.rsqrt(jnp.mean(x * x, axis=-1, keepdims=True) + 1e-6) * g
    return y * (1.0 + sc) + sh


def _x_specs(x):
    if isinstance(x, tuple):
        return [pl.BlockSpec((TM, D), lambda i: (jnp.minimum(i, CTX_TILES - 1), 0)),
                pl.BlockSpec((TM, D), lambda i: (jnp.maximum(i - CTX_TILES, 0), 0))], list(x)
    return [pl.BlockSpec((TM, D), lambda i: (i, 0))], [x]


def _x_tile(x_refs):
    if len(x_refs) == 2:
        return jnp.where(pl.program_id(0) < CTX_TILES, x_refs[0][...], x_refs[1][...])
    return x_refs[0][...]


IN_TM = 512
IN_SPLIT = 2


def _in_kernel(nx, *refs):
    g_ref, sc_ref, sh_ref, w_ref, o_ref = refs[nx:]
    if nx == 2:
        x = jnp.where(pl.program_id(1) < BATCH * SEQ // IN_TM, refs[0][...], refs[1][...])
    else:
        x = refs[0][...]
    h = _rms_mod(x, g_ref[...], sc_ref[0], sh_ref[0])
    o_ref[...] = _dot(h.astype(BF16), w_ref[...])


def _in_proj(x, modt, norm1_l, w_in_l, l):
    n_ctx = BATCH * SEQ // IN_TM
    per_seq = DEC_SEQ // IN_TM
    mod_row = lambda i: jnp.where(i < n_ctx, 0, 1 + (i - n_ctx) // per_seq)
    mod = lambda which: pl.BlockSpec((1, 1, D), lambda j, i: ((l * 6 + which) * 8 + mod_row(i), 0, 0))
    if isinstance(x, tuple):
        x_specs = [pl.BlockSpec((IN_TM, D), lambda j, i: (jnp.minimum(i, n_ctx - 1), 0)),
                   pl.BlockSpec((IN_TM, D), lambda j, i: (jnp.maximum(i - n_ctx, 0), 0))]
        xs = list(x)
    else:
        x_specs, xs = [pl.BlockSpec((IN_TM, D), lambda j, i: (i, 0))], [x]
    cols = IN_COLS // IN_SPLIT
    return pl.pallas_call(
        functools.partial(_in_kernel, len(xs)),
        grid=(IN_SPLIT, T // IN_TM),
        in_specs=x_specs + [
            pl.BlockSpec((1, D), lambda j, i: (0, 0)),
            mod(1),
            mod(0),
            pl.BlockSpec((D, cols), lambda j, i: (0, j)),
        ],
        out_specs=pl.BlockSpec((IN_TM, cols), lambda j, i: (i, j)),
        out_shape=jax.ShapeDtypeStruct((T, IN_COLS), F32),
        compiler_params=_cparams(("arbitrary", "arbitrary")),
        name="in_proj",
    )(*xs, norm1_l.reshape(1, D), modt, modt, w_in_l)


CONV_RB = 32


def _conv_kernel(ac_ref, gc_ref, ap_ref, gp_ref, an_ref, gn_ref, w_ref, b_ref, lg_ref, lb_ref, o_ref, sh_ref,
                 cv_ref):
    i = pl.program_id(0)
    in_smp = i >= CTX_TILES
    pos = i % SMP_TILES
    has_prev = jnp.logical_and(in_smp, pos != 0)
    has_next = jnp.logical_and(in_smp, pos != SMP_TILES - 1)
    n = TM + 2 * HALO
    sh_ref[0, HALO:HALO + TM, :] = ac_ref[...] * _sigmoid(gc_ref[...])
    sh_ref[0, 0:HALO, :] = jnp.where(has_prev, ap_ref[...] * _sigmoid(gp_ref[...]), 0.0)
    sh_ref[0, HALO + TM:, :] = jnp.where(has_next, an_ref[...] * _sigmoid(gn_ref[...]), 0.0)
    for r in range(1, 8):
        sh_ref[r, 0:n - 8, :] = sh_ref[0, r:r + n - 8, :]
    off = HALO - CONV_K // 2

    def body(rb, carry):
        base = pl.multiple_of(rb * CONV_RB, CONV_RB)
        acc = jnp.zeros((CONV_RB, D_A), F32)
        for k in range(CONV_K):
            r = (off + k) % 8
            acc = acc + sh_ref[r, pl.ds(base + (off + k - r), CONV_RB), :] * w_ref[k:k + 1, :]
        cv_ref[pl.ds(base, CONV_RB), :] = acc
        return carry
    lax.fori_loop(0, TM // CONV_RB, body, 0)
    acc = cv_ref[...] + b_ref[...]
    mu = jnp.mean(acc, axis=-1, keepdims=True)
    xc = acc - mu
    y = xc * lax.rsqrt(jnp.mean(xc * xc, axis=-1, keepdims=True) + 1e-5)
    y = y * lg_ref[...] + lb_ref[...]
    o_ref[...] = _silu(y).astype(BF16)


def _conv_branch(u, conv_w_l, conv_b_l, ln_g_l, ln_b_l):
    r = TM // HALO
    last = T // HALO - 1
    cur = lambda c: pl.BlockSpec((TM, CB), lambda i: (i, c))
    prev = lambda c: pl.BlockSpec((HALO, CB), lambda i: (jnp.maximum(i * r - 1, 0), c))
    nxt = lambda c: pl.BlockSpec((HALO, CB), lambda i: (jnp.minimum((i + 1) * r, last), c))
    vec = pl.BlockSpec((1, D_A), lambda i: (0, 0))
    return pl.pallas_call(
        _conv_kernel,
        grid=(NT,),
        in_specs=[cur(C_A), cur(C_AG), prev(C_A), prev(C_AG), nxt(C_A), nxt(C_AG),
                  pl.BlockSpec((CONV_K, D_A), lambda i: (0, 0)), vec, vec, vec],
        out_specs=pl.BlockSpec((TM, D_A), lambda i: (i, 0)),
        out_shape=jax.ShapeDtypeStruct((T, D_A), BF16),
        scratch_shapes=[pltpu.VMEM((8, TM + 2 * HALO, D_A), F32), pltpu.VMEM((TM, D_A), F32)],
        compiler_params=_cparams(("arbitrary",)),
        name="conv_branch",
    )(u, u, u, u, u, u, conv_w_l, conv_b_l.reshape(1, D_A), ln_g_l.reshape(1, D_A), ln_b_l.reshape(1, D_A))


N_LEVELS = 8


def _level_map(reverse):
    t = np.arange(TM)[:, None]
    s = np.arange(TM)[None, :]
    x = t ^ s
    lv = np.zeros((TM, TM), np.int32)
    for bit in range(N_LEVELS):
        lv[(x >> bit) == 1] = bit + 1
    ok = (t < s) if reverse else (t > s)
    return np.where(ok, lv, 0).astype(np.int32)


def _shift_rows(x, k):
    return pltpu.roll(x, k % TM, 0)


def _cumsum_rows(x, row, reverse):
    sh = 1
    while sh < TM:
        if reverse:
            x = x + jnp.where(row < TM - sh, _shift_rows(x, -sh), 0.0)
        else:
            x = x + jnp.where(row >= sh, _shift_rows(x, sh), 0.0)
        sh *= 2
    return x


def _anchors(cum, row, reverse):
    out = []
    z = cum
    for bit in range(N_LEVELS):
        h = 1 << bit
        if 2 * h >= 16:
            c3 = cum.reshape(TM // (2 * h), 2 * h, DK_B)
            p = h if reverse else h - 1
            out.append(jnp.broadcast_to(c3[:, p:p + 1, :], c3.shape).reshape(TM, DK_B))
            continue
        hi = (row & h) != 0
        if reverse:
            out.append(jnp.where(hi, z, _shift_rows(z, -h)))
            z = jnp.where(hi, _shift_rows(z, h), z)
        else:
            out.append(jnp.where(hi, _shift_rows(z, h), z))
            z = jnp.where(hi, z, _shift_rows(z, -h))
    return out


LOW_BITS = 5
SAFE_RANGE = 60.0


def _low_map(reverse):
    t = np.arange(TM)[:, None]
    s = np.arange(TM)[None, :]
    same = (t >> LOW_BITS) == (s >> LOW_BITS)
    ok = (t <= s) if reverse else (t >= s)
    return (same & ok).astype(np.int32)


def _block_anchor(cum, reverse):
    n = 1 << LOW_BITS
    c3 = cum.reshape(TM // n, n, DK_B)
    p = n - 1 if reverse else 0
    return jnp.broadcast_to(c3[:, p:p + 1, :], c3.shape).reshape(TM, DK_B)


def _hgrn_kernel(reverse, l, *refs):
    (q_ref, v_ref, f_ref, lbp_ref, s0_ref, lv_ref, lo_ref, _, o_ref, ns_ref) = refs[:10]
    st_ref, qs_ref, ks_ref, cs_ref, os_ref, as_ref = refs[-6:]
    g = pl.program_id(0)
    i = NT - 1 - g if reverse else g
    is_ctx = i < CTX_TILES
    pos = i % SMP_TILES
    seq_start = jnp.logical_and(i >= CTX_TILES, pos == (SMP_TILES - 1 if reverse else 0))

    @pl.when(is_ctx)
    def _():
        st_ref[...] = jnp.zeros_like(st_ref)

    @pl.when(seq_start)
    def _():
        for h in range(H_B):
            st_ref[h] = s0_ref[0, h].T

    d = 1 if reverse else 0
    rows = [lbp_ref[k * 2 + d:k * 2 + d + 1, :] for k in range(DEPTH)]
    mx = functools.reduce(jnp.maximum, rows)
    ex = [jnp.exp(r - mx) for r in rows]
    den = functools.reduce(jnp.add, ex)
    lb_all = jnp.zeros_like(mx)
    for k in range(1, l + 1):
        lb_all = lb_all + ex[k] / den

    row = lax.broadcasted_iota(jnp.int32, (TM, DK_B), 0)
    spread = jnp.zeros((1, DK_B), F32)
    for h in range(H_B):
        sl = slice(h * DK_B, (h + 1) * DK_B)
        lb = lb_all[:, sl]
        fg = lb + (1.0 - lb) * _sigmoid_small_accurate(f_ref[:, sl])
        cum = _cumsum_rows(jnp.log(fg), row, reverse)
        ks_ref[h] = 1.0 - fg
        qs_ref[h] = _silu(q_ref[:, sl])
        cs_ref[h] = cum
        spread = jnp.maximum(spread, jnp.max(_block_anchor(cum, reverse) - cum, axis=0, keepdims=True))
    single_anchor_ok = jnp.max(spread) < SAFE_RANGE

    half = TM // 2
    halves = (slice(0, half), slice(half, TM))
    late, early = (halves[0], halves[1]) if reverse else (halves[1], halves[0])

    lv = lv_ref[0:half, 0:half]

    def operands(h, fq, fk):
        return (qs_ref[h] * fq).astype(BF16), (ks_ref[h] * fk).astype(BF16)

    for h in range(H_B):
        sl = slice(h * DK_B, (h + 1) * DK_B)
        q, kg, cum = qs_ref[h], ks_ref[h], cs_ref[h]
        vb = v_ref[:, sl].astype(BF16)
        tot = cum[0:1, :] if reverse else cum[TM - 1:TM, :]
        st = st_ref[h]
        o = _dot_nt((q * jnp.exp(cum)).astype(BF16), st.astype(BF16))
        anchors = _anchors(cum, row, reverse)
        fac = jnp.exp(-jnp.abs(cum - anchors[N_LEVELS - 1]))
        qf, kf = operands(h, fac, fac)
        o_top = _dot(_dot_nt(qf[late], kf[early]).astype(BF16), vb[early])
        zero = jnp.zeros_like(o_top)
        os_ref[h] = o + jnp.concatenate([o_top, zero] if reverse else [zero, o_top], axis=0)
        a = [jnp.zeros((half, half), F32) for _ in halves]
        for bit in range(LOW_BITS, N_LEVELS - 1):
            fac = jnp.exp(-jnp.abs(cum - anchors[bit]))
            qf, kf = operands(h, fac, fac)
            a = [jnp.where(lv == bit + 1, _dot_nt(qf[r], kf[r]), a[b]) for b, r in enumerate(halves)]
        for b in range(2):
            as_ref[h, b] = a[b]
        kbar = (kg * jnp.exp(tot - cum)).astype(BF16)
        st_ref[h] = jnp.exp(tot) * st + _dot_tn(vb, kbar)

    def heads(single_anchor):
        for h in range(H_B):
            sl = slice(h * DK_B, (h + 1) * DK_B)
            cum = cs_ref[h]
            v = v_ref[:, sl]
            vb = v.astype(BF16)
            o = os_ref[h]
            a = [as_ref[h, b] for b in range(2)]
            if single_anchor:
                e = _block_anchor(cum, reverse)
                qf, kf = operands(h, jnp.exp(cum - e), jnp.exp(e - cum))
                lo = lo_ref[0:half, 0:half] != 0
                a = [jnp.where(lo, _dot_nt(qf[r], kf[r]), a[b]) for b, r in enumerate(halves)]
            else:
                o = o + jnp.sum(qs_ref[h] * ks_ref[h], axis=-1, keepdims=True) * v
                anchors = _anchors(cum, row, reverse)
                for bit in range(LOW_BITS):
                    fac = jnp.exp(-jnp.abs(cum - anchors[bit]))
                    qf, kf = operands(h, fac, fac)
                    a = [jnp.where(lv == bit + 1, _dot_nt(qf[r], kf[r]), a[b]) for b, r in enumerate(halves)]
            o_ref[:, sl] = o + jnp.concatenate([_dot(a[b].astype(BF16), vb[r]) for b, r in enumerate(halves)],
                                               axis=0)

    @pl.when(single_anchor_ok)
    def _():
        heads(True)

    @pl.when(jnp.logical_not(single_anchor_ok))
    def _():
        heads(False)

    @pl.when(is_ctx)
    def _():
        for h in range(H_B):
            ns_ref[0, 0, 0, h] = st_ref[h].T


def _hgrn_pass(reverse, l, u, hgrn_lb, s0, maps, new_state):
    tile = (lambda g: NT - 1 - g) if reverse else (lambda g: g)
    col = lambda c: pl.BlockSpec((TM, CB), lambda g: (tile(g), c))
    s0_spec = pl.BlockSpec((1, H_B, DK_B, DK_B),
                           lambda g: (jnp.clip((tile(g) - CTX_TILES) // SMP_TILES, 0, DEC_BATCH - 1), 0, 0, 0))
    lbp_spec = pl.BlockSpec((DEPTH * 2, D_B), lambda g: (0, 0))
    map_spec = pl.BlockSpec((TM, TM), lambda g: (0, 0))
    d = 1 if reverse else 0
    ns_spec = pl.BlockSpec((1, 1, 1, H_B, DK_B, DK_B),
                           lambda g: (jnp.minimum(tile(g), CTX_TILES - 1), l, d, 0, 0, 0))
    any_spec = pl.BlockSpec(memory_space=pl.ANY)
    f_col = C_HFB if reverse else C_HFF
    in_specs = [col(C_HQ), col(C_HI), col(f_col), lbp_spec, s0_spec, map_spec, map_spec, any_spec]
    args = (u, u, u, hgrn_lb, s0, *maps, new_state)
    o_shape = jax.ShapeDtypeStruct((T, D_B), F32)
    head_tile = pltpu.VMEM((H_B, TM, DK_B), F32)
    return pl.pallas_call(
        functools.partial(_hgrn_kernel, reverse, l),
        grid=(NT,),
        in_specs=in_specs,
        out_specs=[pl.BlockSpec((TM, D_B), lambda g: (tile(g), 0)), ns_spec],
        out_shape=[o_shape, jax.ShapeDtypeStruct(new_state.shape, F32)],
        input_output_aliases={len(args) - 1: 1},
        scratch_shapes=[pltpu.VMEM((H_B, DK_B, DK_B), F32), head_tile, head_tile, head_tile, head_tile,
                        pltpu.VMEM((H_B, 2, TM // 2, TM // 2), F32)],
        compiler_params=_cparams(("arbitrary",)),
        name="hgrn_bwd" if reverse else "hgrn_fwd",
    )(*args)


def _qk_kernel(aq_ref, ak_ref, av_ref, qn_ref, kn_ref, cos_ref, sin_ref, bd_ref, _k, _v, qt_ref, kb_ref, vt_ref,
               nk_ref, nv_ref):
    bd = bd_ref[...]
    lane = lax.broadcasted_iota(jnp.int32, (TM, 128), 1)
    even = (lane & 1) == 0
    cos = cos_ref[...]
    sin = sin_ref[...]

    def seg_norm(x, g):
        ss = _dot((x * x).astype(BF16), bd)
        return x * lax.rsqrt(ss * (1.0 / DH_C) + 1e-6) * g

    def rope(x):
        parts = []
        for s in range(D_C // 128):
            xs = x[:, s * 128:(s + 1) * 128]
            sw = jnp.where(even, pltpu.roll(xs, 127, 1), pltpu.roll(xs, 1, 1))
            parts.append(xs * cos + sw * sin)
        return jnp.concatenate(parts, axis=1)

    q = seg_norm(aq_ref[...], qn_ref[...])
    k = seg_norm(ak_ref[...], kn_ref[...])
    @pl.when(pl.program_id(0) < CTX_TILES)
    def _():
        nk_ref[0, 0] = k
        nv_ref[0, 0] = av_ref[...]

    qt_ref[...] = (rope(q) * (DH_C ** -0.5 * math.log2(math.e))).T.astype(BF16)
    kb_ref[...] = rope(k).astype(BF16)
    vt_ref[0] = av_ref[...].T.astype(BF16)


def _rope_tables():
    rows = DEC_SEQ // GRID_W
    row = jnp.broadcast_to(jnp.arange(rows)[:, None], (rows, GRID_W)).reshape(-1).astype(F32)
    colp = jnp.broadcast_to(jnp.arange(GRID_W)[None, :], (rows, GRID_W)).reshape(-1).astype(F32)
    half = DH_C // 2
    inv = ROPE_BASE ** (-jnp.arange(0, half, 2, dtype=F32) / half)
    ang = jnp.concatenate([row[:, None] * inv, colp[:, None] * inv], axis=-1)
    cos = jnp.repeat(jnp.cos(ang), 2, axis=-1)
    sin = jnp.repeat(jnp.sin(ang), 2, axis=-1) * jnp.tile(jnp.array([-1.0, 1.0], F32), half)
    cos = jnp.concatenate([jnp.ones((TM, DH_C), F32), cos], axis=0)
    sin = jnp.concatenate([jnp.zeros((TM, DH_C), F32), sin], axis=0)
    return jnp.tile(cos, (1, 2)), jnp.tile(sin, (1, 2))


def _qk_prep(l, u, q_norm_l, k_norm_l, cos_t, sin_t, bd, new_k, new_v):
    col = lambda c: pl.BlockSpec((TM, CB), lambda i: (i, c))
    vec = pl.BlockSpec((1, D_C), lambda i: (0, 0))
    tab = pl.BlockSpec((TM, 128), lambda i: (jnp.where(i < CTX_TILES, 0, 1 + i % SMP_TILES), 0))
    any_spec = pl.BlockSpec(memory_space=pl.ANY)
    cache = pl.BlockSpec((1, 1, SEQ, D_C), lambda i: (jnp.minimum(i, CTX_TILES - 1), l, 0, 0))
    return pl.pallas_call(
        _qk_kernel,
        grid=(NT,),
        in_specs=[col(C_AQ), col(C_AK), col(C_AV), vec, vec, tab, tab,
                  pl.BlockSpec((D_C, D_C), lambda i: (0, 0)), any_spec, any_spec],
        out_specs=[pl.BlockSpec((D_C, TM), lambda i: (0, i)), pl.BlockSpec((TM, D_C), lambda i: (i, 0)),
                   pl.BlockSpec((1, D_C, TM), lambda i: (i, 0, 0)), cache, cache],
        out_shape=[jax.ShapeDtypeStruct((D_C, T), BF16), jax.ShapeDtypeStruct((T, D_C), BF16),
                   jax.ShapeDtypeStruct((NT, D_C, TM), BF16),
                   jax.ShapeDtypeStruct(new_k.shape, F32), jax.ShapeDtypeStruct(new_v.shape, F32)],
        input_output_aliases={8: 3, 9: 4},
        compiler_params=_cparams(("arbitrary",)),
        name="qk_prep",
    )(u, u, u, jnp.tile(q_norm_l, D_C // DH_C).reshape(1, D_C), jnp.tile(k_norm_l, D_C // DH_C).reshape(1, D_C),
      cos_t, sin_t, bd, new_k, new_v)


def _attn_kernel(lam_init, n_new, has_cache, n_heads, *refs):
    if has_cache:
        qt_ref, kc_ref, vc_ref, kn_ref, vn_ref, lam_ref, sub_ref, _, o_ref = refs[:9]
    else:
        qt_ref, kn_ref, vn_ref, lam_ref, sub_ref, _, o_ref = refs[:7]
    qm_ref, s_ref, w_ref, m_ref, l_ref, r_ref, mc_ref, acc_ref = refs[-8:]
    tq = qt_ref.shape[1]
    lq = lam_ref[...]
    lam = (jnp.exp(jnp.sum(lq[0:1] * lq[1:2], axis=-1, keepdims=True))
           - jnp.exp(jnp.sum(lq[2:3] * lq[3:4], axis=-1, keepdims=True)) + lam_init)
    col_blocks = [slice(cb * 128, (cb + 1) * 128) for cb in range(tq // 128)]

    def scores(k, par, slot):
        for c in range(2):
            s_ref[c, par, slot] = _dot(k, qm_ref[c])

    def stats(par, slot):
        for c in range(2):
            for cols in col_blocks:
                s = s_ref[c, par, slot, :, cols]
                m_old = m_ref[c, :, cols]
                m_new = jnp.maximum(m_old, jnp.max(s, axis=0, keepdims=True))
                p = jnp.exp2(s - m_new)
                l_ref[c, :, cols] = jnp.exp2(m_old - m_new) * l_ref[c, :, cols] + jnp.sum(p, axis=0, keepdims=True)
                m_ref[c, :, cols] = m_new
                mc_ref[c, par, slot, :, cols] = m_new
                s_ref[c, par, slot, :, cols] = p

    def weights(par, slot):
        for cols in col_blocks:
            f0 = jnp.exp2(mc_ref[0, par, slot, :, cols] - m_ref[0, :, cols]) * r_ref[0, :, cols]
            f1 = jnp.exp2(mc_ref[1, par, slot, :, cols] - m_ref[1, :, cols]) * r_ref[1, :, cols]
            w_ref[par, :, cols] = (s_ref[0, par, slot, :, cols] * f0 - s_ref[1, par, slot, :, cols] * f1).astype(BF16)

    def accumulate(vt, par):
        acc_ref[...] += _dot(vt, w_ref[par])

    half = n_new // 2
    for h in range(n_heads):
        hs = slice(h * 128, (h + 1) * 128)
        qt = qt_ref[hs, :]
        sub_i = lax.broadcasted_iota(jnp.int32, qt.shape, 0)
        qm_ref[0] = jnp.where(sub_i < DH_C, qt, jnp.zeros_like(qt))
        qm_ref[1] = jnp.where(sub_i >= DH_C, qt, jnp.zeros_like(qt))
        m_ref[...] = jnp.full(m_ref.shape, -1e30, F32)
        l_ref[...] = jnp.zeros_like(l_ref)
        acc_ref[...] = jnp.zeros_like(acc_ref)
        new_k = lambda j: kn_ref[pl.ds(pl.multiple_of(j * TM, TM), TM), hs]

        if n_new == 1:
            scores(kn_ref[:, hs], 0, 0)
            stats(0, 0)
        else:
            scores(new_k(0), 0, 0)

            def fill(i, carry):
                scores(new_k(2 * i + 1), 1, i)
                stats(0, i)
                scores(new_k(2 * i + 2), 0, i + 1)
                stats(1, i)
                return carry
            lax.fori_loop(0, half - 1, fill, 0)
            scores(kn_ref[(n_new - 1) * TM:, hs], 1, half - 1)
            stats(0, half - 1)
            scores(kc_ref[:, hs], 0, half)
            stats(1, half - 1)
            stats(0, half)

        r_ref[0] = 1.0 / l_ref[0]
        r_ref[1] = lam / l_ref[1]
        weights(0, 0)
        if n_new == 1:
            accumulate(vn_ref[0, hs, :], 0)
        else:
            def drain(i, carry):
                weights(1, i)
                accumulate(vn_ref[2 * i, hs, :], 0)
                weights(0, i + 1)
                accumulate(vn_ref[2 * i + 1, hs, :], 1)
                return carry
            lax.fori_loop(0, half, drain, 0)
            accumulate(vc_ref[0, hs, :], 0)

        for cols in col_blocks:
            o = acc_ref[:, cols]
            y = o * lax.rsqrt(jnp.mean(o * o, axis=0, keepdims=True) + 1e-6)
            o_ref[cols, hs] = (y.T * sub_ref[...] * (1.0 - lam_init)).astype(BF16)


def _attn_scratch(tq, n_chunks):
    slots = (n_chunks + 1) // 2
    return [pltpu.VMEM((2, 128, tq), BF16), pltpu.VMEM((2, 2, slots, TM, tq), F32), pltpu.VMEM((2, TM, tq), BF16),
            pltpu.VMEM((2, 1, tq), F32), pltpu.VMEM((2, 1, tq), F32), pltpu.VMEM((2, 1, tq), F32),
            pltpu.VMEM((2, 2, slots, 1, tq), F32), pltpu.VMEM((128, tq), F32)]


def _attention(l, qt, kb, vt, ck, cvt, lambda_qk_l, subln_l):
    lam_init = 0.8 - 0.6 * math.exp(-0.3 * l)
    sub = subln_l.reshape(1, 2 * DH_C)
    o_ctx = pl.pallas_call(
        functools.partial(_attn_kernel, lam_init, 1, False, H_C),
        grid=(BATCH,),
        in_specs=[pl.BlockSpec((D_C, SEQ), lambda b: (0, b)),
                  pl.BlockSpec((SEQ, D_C), lambda b: (b, 0)),
                  pl.BlockSpec((1, D_C, TM), lambda b: (b, 0, 0)),
                  pl.BlockSpec((4, DH_C), lambda b: (0, 0)),
                  pl.BlockSpec((1, 2 * DH_C), lambda b: (0, 0)),
                  pl.BlockSpec(memory_space=pl.ANY)],
        out_specs=pl.BlockSpec((SEQ, D_C), lambda b: (b, 0)),
        out_shape=jax.ShapeDtypeStruct((T, D_C), BF16),
        input_output_aliases={5: 0},
        scratch_shapes=_attn_scratch(SEQ, 1),
        compiler_params=_cparams(("arbitrary",)),
        name="attn_ctx",
    )(qt, kb, vt, lambda_qk_l, sub, jnp.zeros((T, D_C), BF16))

    nq = DEC_SEQ // TQ
    base_q = BATCH * SEQ // TQ
    base_kv = BATCH * SEQ // DEC_SEQ
    o_smp = pl.pallas_call(
        functools.partial(_attn_kernel, lam_init, SMP_TILES, True, 1),
        grid=(DEC_BATCH, H_C, nq),
        in_specs=[pl.BlockSpec((128, TQ), lambda b, h, j: (h, base_q + b * nq + j)),
                  pl.BlockSpec((PAST, 128), lambda b, h, j: (b, h)),
                  pl.BlockSpec((1, 128, PAST), lambda b, h, j: (b, h, 0)),
                  pl.BlockSpec((DEC_SEQ, 128), lambda b, h, j: (base_kv + b, h)),
                  pl.BlockSpec((SMP_TILES, 128, TM), lambda b, h, j: (base_kv + b, h, 0)),
                  pl.BlockSpec((4, DH_C), lambda b, h, j: (0, 0)),
                  pl.BlockSpec((1, 2 * DH_C), lambda b, h, j: (0, 0)),
                  pl.BlockSpec(memory_space=pl.ANY)],
        out_specs=pl.BlockSpec((TQ, 128), lambda b, h, j: (base_q + b * nq + j, h)),
        out_shape=jax.ShapeDtypeStruct((T, D_C), BF16),
        input_output_aliases={7: 0},
        scratch_shapes=_attn_scratch(TQ, SMP_TILES + 1),
        compiler_params=_cparams(("arbitrary", "arbitrary", "arbitrary")),
        name="attn_smp",
    )(qt, ck, cvt, kb, vt, lambda_qk_l, sub, o_ctx)
    return o_smp


def _merge_ffn_kernel(nx, *refs):
    (a_ref, bf_ref, bb_ref, hg_ref, hn_ref, c_ref, ga_ref, gb_ref, gc_ref, g1_ref, wb_ref, wo_ref,
     n2_ref, sc_ref, sh_ref, g2_ref, wi_ref, wd_ref) = refs[nx:nx + 18]
    o_refs = refs[nx + 18:]
    hb = []
    for h in range(H_B):
        hs = slice(h * DK_B, (h + 1) * DK_B)
        o = bf_ref[:, hs] + bb_ref[:, hs]
        y = o * lax.rsqrt(jnp.mean(o * o, axis=-1, keepdims=True) + 1e-6) * hn_ref[...]
        hb.append((y * _silu(hg_ref[:, hs])).astype(BF16))
    ya = _dot(a_ref[...], wb_ref[0:D_A, :])
    yb = _dot(jnp.concatenate(hb, axis=1), wb_ref[D_A:D_A + D_B, :])
    yc = _dot(c_ref[...], wb_ref[D_A + D_B:, :])
    m = _sigmoid(ga_ref[...]) * ya + _sigmoid(gb_ref[...]) * yb + _sigmoid(gc_ref[...]) * yc
    x1 = _x_tile(refs[:nx]) + g1_ref[0] * _dot(m.astype(BF16), wo_ref[...])
    h = _rms_mod(x1, n2_ref[...], sc_ref[0], sh_ref[0]).astype(BF16)
    gu = _dot(h, wi_ref[...])
    ff = _silu(gu[:, :D_FF]) * gu[:, D_FF:]
    y = x1 + g2_ref[0] * _dot(ff.astype(BF16), wd_ref[...])
    if len(o_refs) == 1:
        o_refs[0][...] = y
    else:
        @pl.when(pl.program_id(0) < CTX_TILES)
        def _():
            o_refs[0][...] = y

        @pl.when(pl.program_id(0) >= CTX_TILES)
        def _():
            o_refs[1][...] = y


def _merge_ffn(ya, of, ob, hgrn_norm_l, yc, u, x, modt, w_branch_l, w_out_l, norm2_l, w_ffn_in_l, w_ffn_out_l, l,
               split_out):
    row = lambda w: pl.BlockSpec((TM, w), lambda i: (i, 0))
    gate0 = C_GT * CB // D
    gate = lambda k: pl.BlockSpec((TM, D), lambda i: (i, gate0 + k))
    resident = lambda shape: pl.BlockSpec(shape, lambda i: (0, 0), pipeline_mode=pl.Buffered(1))
    x_specs, xs = _x_specs(x)
    if split_out:
        out_specs, _ = _x_specs((None, None))
        out_shape = [jax.ShapeDtypeStruct((BATCH * SEQ, D), F32), jax.ShapeDtypeStruct((DEC_BATCH * DEC_SEQ, D), F32)]
    else:
        out_specs = row(D)
        out_shape = jax.ShapeDtypeStruct((T, D), F32)
    return pl.pallas_call(
        functools.partial(_merge_ffn_kernel, len(xs)),
        grid=(NT,),
        in_specs=x_specs + [row(D_A), row(D_B), row(D_B), pl.BlockSpec((TM, CB), lambda i: (i, C_HG)),
                            pl.BlockSpec((1, DK_B), lambda i: (0, 0)), row(D_C),
                            gate(0), gate(1), gate(2), _mod_spec(l, 2),
                            resident((D_A + D_B + D_C, D)), resident((D, D)),
                            pl.BlockSpec((1, D), lambda i: (0, 0)),
                            _mod_spec(l, 4), _mod_spec(l, 3), _mod_spec(l, 5),
                            resident((D, 2 * D_FF)), resident((D_FF, D))],
        out_specs=out_specs,
        out_shape=out_shape,
        compiler_params=_cparams(("arbitrary",)),
        name="merge_ffn",
    )(*xs, ya, of, ob, u, hgrn_norm_l.reshape(1, DK_B), yc, u, u, u, modt, w_branch_l, w_out_l,
      norm2_l.reshape(1, D), modt, modt, modt, w_ffn_in_l, w_ffn_out_l)


def _layer(l, x, modt, cache_k, cache_v, state_hgrn, tabs, p, outs):
    (norm1, norm2, w_in, conv_w, conv_b, conv_ln_g, conv_ln_b, hgrn_lb, hgrn_norm, q_norm, k_norm, lambda_qk,
     subln, w_branch, w_out, w_ffn_in, w_ffn_out) = p
    cos_t, sin_t, bd, maps_f, maps_b = tabs
    new_k, new_v, new_s = outs
    u = _in_proj(x, modt, norm1[l], w_in[l].astype(BF16), l)
    ya = _conv_branch(u, conv_w[l], conv_b[l], conv_ln_g[l], conv_ln_b[l])
    lbp = hgrn_lb.reshape(DEPTH * 2, D_B)
    ob, new_s = _hgrn_pass(True, l, u, lbp, state_hgrn[:, l, 1], maps_b, new_s)
    of, new_s = _hgrn_pass(False, l, u, lbp, state_hgrn[:, l, 0], maps_f, new_s)
    qt, kb, vt, new_k, new_v = _qk_prep(l, u, q_norm[l], k_norm[l], cos_t, sin_t, bd, new_k, new_v)
    ck = cache_k[:, l].reshape(DEC_BATCH * PAST, D_C).astype(BF16)
    cvt = cache_v[:, l].reshape(DEC_BATCH, PAST, D_C).astype(BF16).transpose(0, 2, 1)
    yc = _attention(l, qt, kb, vt, ck, cvt, lambda_qk[l], subln[l])
    x = _merge_ffn(ya, of, ob, hgrn_norm[l], yc, u, x, modt, w_branch[l].astype(BF16), w_out[l].astype(BF16), norm2[l],
                   w_ffn_in[l].astype(BF16), w_ffn_out[l].astype(BF16), l, split_out=l == DEPTH - 1)
    return x, (new_k, new_v, new_s)


def kernel(x_prompt, x_sample, cache_k, cache_v, state_hgrn, c, c_ctx, w_mod, b_mod, norm1, norm2, w_in, conv_w,
           conv_b, conv_ln_g, conv_ln_b, hgrn_lb, hgrn_norm, q_norm, k_norm, lambda_qk, subln, w_branch, w_out,
           w_ffn_in, w_ffn_out):
    x = (x_prompt.reshape(BATCH * SEQ, D), x_sample.reshape(DEC_BATCH * DEC_SEQ, D))
    cv8 = jnp.concatenate([c_ctx[None, :], c, jnp.zeros((8 - 1 - DEC_BATCH, D), F32)], axis=0)
    mod = _modulation(cv8, w_mod, b_mod)
    modt = mod.reshape(DEPTH, 8, 6, D).transpose(0, 2, 1, 3).reshape(DEPTH * 6 * 8, 1, D)
    cos_t, sin_t = _rope_tables()
    seg = np.arange(D_C) // DH_C
    bd = jnp.asarray(seg[:, None] == seg[None, :], BF16)
    maps = lambda rev: (jnp.asarray(_level_map(rev)), jnp.asarray(_low_map(rev)))
    tabs = (cos_t, sin_t, bd, maps(False), maps(True))
    p = (norm1, norm2, w_in, conv_w, conv_b, conv_ln_g, conv_ln_b, hgrn_lb, hgrn_norm, q_norm, k_norm, lambda_qk,
         subln, w_branch, w_out, w_ffn_in, w_ffn_out)
    outs = (jnp.zeros((BATCH, DEPTH, SEQ, D_C), F32), jnp.zeros((BATCH, DEPTH, SEQ, D_C), F32),
            jnp.zeros((BATCH, DEPTH, 2, H_B, DK_B, DK_B), F32))
    for l in range(DEPTH):
        x, outs = _layer(l, x, modt, cache_k, cache_v, state_hgrn, tabs, p, outs)
    new_k, new_v, new_s = outs
    y_prompt = x[0].reshape(BATCH, SEQ, D)
    y_sample = x[1].reshape(DEC_BATCH, DEC_SEQ, D)
    return (y_prompt, y_sample, new_k.reshape(BATCH, DEPTH, SEQ, H_C, 2, DH_C),
            new_v.reshape(BATCH, DEPTH, SEQ, H_C, 2 * DH_C), new_s)
```

```python
import functools
import math

import numpy as np
import jax
import jax.numpy as jnp
from jax import lax
from jax.experimental import pallas as pl
from jax.experimental.pallas import tpu as pltpu

F32 = jnp.float32
BF16 = jnp.bfloat16

D = 1024
BATCH, SEQ = 32, 256
DEPTH = 2
DEC_BATCH, DEC_SEQ = 2, 4096
PAST = 256
GRID_W = 64
D_A, CONV_K = 512, 31
H_B, DK_B = 4, 128
D_B = 512
H_C, DH_C = 4, 64
D_C = 512
ROPE_BASE = 10000.0
D_FF = 2816
IN_COLS = 8192

TM = 256
CTX_TILES = BATCH * SEQ // TM
SMP_TILES = DEC_SEQ // TM
T = BATCH * SEQ + DEC_BATCH * DEC_SEQ
NT = T // TM
CB = 512
HALO = 16
TQ = 512
VMEM_LIMIT = 56 * 1024 * 1024

C_A, C_AG, C_HQ, C_HI, C_HFF, C_HFB, C_HG, C_AQ, C_AK, C_AV, C_GT = 0, 1, 2, 3, 4, 5, 6, 7, 8, 9, 10


def _cparams(sem):
    return pltpu.CompilerParams(dimension_semantics=sem, vmem_limit_bytes=VMEM_LIMIT)


def _mod_row(i):
    return jnp.where(i < CTX_TILES, 0, 1 + (i - CTX_TILES) // SMP_TILES)


def _sigmoid(x):
    return 0.5 * jnp.tanh(0.5 * x) + 0.5


def _sigmoid_small_accurate(x):
    return jnp.exp(-jnp.log(1.0 + jnp.exp(-x)))


def _silu(x):
    return x * _sigmoid(x)


def _dot(a, b):
    return jnp.dot(a, b, preferred_element_type=F32)


def _dot_nt(a, b):
    return lax.dot_general(a, b, (((1,), (1,)), ((), ())), preferred_element_type=F32)


def _dot_tn(a, b):
    return lax.dot_general(a, b, (((0,), (0,)), ((), ())), preferred_element_type=F32)


def _split_bf16(x):
    hi = x.astype(BF16)
    lo = (x - hi.astype(F32)).astype(BF16)
    return hi, lo


MOD_TN = 1536


def _mod_kernel(cv_ref, w_ref, b_ref, o_ref):
    a = _silu(cv_ref[...])
    a_hi, a_lo = _split_bf16(a)
    w_hi, w_lo = _split_bf16(w_ref[0])
    acc = _dot(a_hi, w_hi) + _dot(a_lo, w_hi) + _dot(a_hi, w_lo)
    o_ref[0] = acc + b_ref[0]


def _modulation(cv8, w_mod, b_mod):
    return pl.pallas_call(
        _mod_kernel,
        grid=(DEPTH, 6 * D // MOD_TN),
        in_specs=[
            pl.BlockSpec((8, D), lambda l, j: (0, 0)),
            pl.BlockSpec((1, D, MOD_TN), lambda l, j: (l, 0, j)),
            pl.BlockSpec((1, 1, MOD_TN), lambda l, j: (l, 0, j)),
        ],
        out_specs=pl.BlockSpec((1, 8, MOD_TN), lambda l, j: (l, 0, j)),
        out_shape=jax.ShapeDtypeStruct((DEPTH, 8, 6 * D), F32),
        compiler_params=_cparams(("arbitrary", "arbitrary")),
        name="adaln_mod",
    )(cv8, w_mod, b_mod.reshape(DEPTH, 1, 6 * D))


def _mod_spec(l, which):
    return pl.BlockSpec((1, 1, D), lambda i: ((l * 6 + which) * 8 + _mod_row(i), 0, 0))


def _rms_mod(x, g, sc, sh):
    y = x * lax.rsqrt(jnp.mean(x * x, axis=-1, keepdims=True) + 1e-6) * g
    return y * (1.0 + sc) + sh


def _x_specs(x):
    if isinstance(x, tuple):
        return [pl.BlockSpec((TM, D), lambda i: (jnp.minimum(i, CTX_TILES - 1), 0)),
                pl.BlockSpec((TM, D), lambda i: (jnp.maximum(i - CTX_TILES, 0), 0))], list(x)
    return [pl.BlockSpec((TM, D), lambda i: (i, 0))], [x]


def _x_tile(x_refs):
    if len(x_refs) == 2:
        return jnp.where(pl.program_id(0) < CTX_TILES, x_refs[0][...], x_refs[1][...])
    return x_refs[0][...]


IN_TM = 512
IN_SPLIT = 2


def _in_kernel(nx, *refs):
    g_ref, sc_ref, sh_ref, w_ref, o_ref = refs[nx:]
    if nx == 2:
        x = jnp.where(pl.program_id(1) < BATCH * SEQ // IN_TM, refs[0][...], refs[1][...])
    else:
        x = refs[0][...]
    h = _rms_mod(x, g_ref[...], sc_ref[0], sh_ref[0])
    o_ref[...] = _dot(h.astype(BF16), w_ref[...])


def _in_proj(x, modt, norm1_l, w_in_l, l):
    n_ctx = BATCH * SEQ // IN_TM
    per_seq = DEC_SEQ // IN_TM
    mod_row = lambda i: jnp.where(i < n_ctx, 0, 1 + (i - n_ctx) // per_seq)
    mod = lambda which: pl.BlockSpec((1, 1, D), lambda j, i: ((l * 6 + which) * 8 + mod_row(i), 0, 0))
    if isinstance(x, tuple):
        x_specs = [pl.BlockSpec((IN_TM, D), lambda j, i: (jnp.minimum(i, n_ctx - 1), 0)),
                   pl.BlockSpec((IN_TM, D), lambda j, i: (jnp.maximum(i - n_ctx, 0), 0))]
        xs = list(x)
    else:
        x_specs, xs = [pl.BlockSpec((IN_TM, D), lambda j, i: (i, 0))], [x]
    cols = IN_COLS // IN_SPLIT
    return pl.pallas_call(
        functools.partial(_in_kernel, len(xs)),
        grid=(IN_SPLIT, T // IN_TM),
        in_specs=x_specs + [
            pl.BlockSpec((1, D), lambda j, i: (0, 0)),
            mod(1),
            mod(0),
            pl.BlockSpec((D, cols), lambda j, i: (0, j)),
        ],
        out_specs=pl.BlockSpec((IN_TM, cols), lambda j, i: (i, j)),
        out_shape=jax.ShapeDtypeStruct((T, IN_COLS), F32),
        compiler_params=_cparams(("arbitrary", "arbitrary")),
        name="in_proj",
    )(*xs, norm1_l.reshape(1, D), modt, modt, w_in_l)


CONV_RB = 32


def _conv_kernel(ac_ref, gc_ref, ap_ref, gp_ref, an_ref, gn_ref, w_ref, b_ref, lg_ref, lb_ref, o_ref, sh_ref,
                 cv_ref):
    i = pl.program_id(0)
    in_smp = i >= CTX_TILES
    pos = i % SMP_TILES
    has_prev = jnp.logical_and(in_smp, pos != 0)
    has_next = jnp.logical_and(in_smp, pos != SMP_TILES - 1)
    n = TM + 2 * HALO
    sh_ref[0, HALO:HALO + TM, :] = ac_ref[...] * _sigmoid(gc_ref[...])
    sh_ref[0, 0:HALO, :] = jnp.where(has_prev, ap_ref[...] * _sigmoid(gp_ref[...]), 0.0)
    sh_ref[0, HALO + TM:, :] = jnp.where(has_next, an_ref[...] * _sigmoid(gn_ref[...]), 0.0)
    for r in range(1, 8):
        sh_ref[r, 0:n - 8, :] = sh_ref[0, r:r + n - 8, :]
    off = HALO - CONV_K // 2

    def body(rb, carry):
        base = pl.multiple_of(rb * CONV_RB, CONV_RB)
        acc = jnp.zeros((CONV_RB, D_A), F32)
        for k in range(CONV_K):
            r = (off + k) % 8
            acc = acc + sh_ref[r, pl.ds(base + (off + k - r), CONV_RB), :] * w_ref[k:k + 1, :]
        cv_ref[pl.ds(base, CONV_RB), :] = acc
        return carry
    lax.fori_loop(0, TM // CONV_RB, body, 0)
    acc = cv_ref[...] + b_ref[...]
    mu = jnp.mean(acc, axis=-1, keepdims=True)
    xc = acc - mu
    y = xc * lax.rsqrt(jnp.mean(xc * xc, axis=-1, keepdims=True) + 1e-5)
    y = y * lg_ref[...] + lb_ref[...]
    o_ref[...] = _silu(y).astype(BF16)


def _conv_branch(u, conv_w_l, conv_b_l, ln_g_l, ln_b_l):
    r = TM // HALO
    last = T // HALO - 1
    cur = lambda c: pl.BlockSpec((TM, CB), lambda i: (i, c))
    prev = lambda c: pl.BlockSpec((HALO, CB), lambda i: (jnp.maximum(i * r - 1, 0), c))
    nxt = lambda c: pl.BlockSpec((HALO, CB), lambda i: (jnp.minimum((i + 1) * r, last), c))
    vec = pl.BlockSpec((1, D_A), lambda i: (0, 0))
    return pl.pallas_call(
        _conv_kernel,
        grid=(NT,),
        in_specs=[cur(C_A), cur(C_AG), prev(C_A), prev(C_AG), nxt(C_A), nxt(C_AG),
                  pl.BlockSpec((CONV_K, D_A), lambda i: (0, 0)), vec, vec, vec],
        out_specs=pl.BlockSpec((TM, D_A), lambda i: (i, 0)),
        out_shape=jax.ShapeDtypeStruct((T, D_A), BF16),
        scratch_shapes=[pltpu.VMEM((8, TM + 2 * HALO, D_A), F32), pltpu.VMEM((TM, D_A), F32)],
        compiler_params=_cparams(("arbitrary",)),
        name="conv_branch",
    )(u, u, u, u, u, u, conv_w_l, conv_b_l.reshape(1, D_A), ln_g_l.reshape(1, D_A), ln_b_l.reshape(1, D_A))


N_LEVELS = 8


def _level_map(reverse):
    t = np.arange(TM)[:, None]
    s = np.arange(TM)[None, :]
    x = t ^ s
    lv = np.zeros((TM, TM), np.int32)
    for bit in range(N_LEVELS):
        lv[(x >> bit) == 1] = bit + 1
    ok = (t < s) if reverse else (t > s)
    return np.where(ok, lv, 0).astype(np.int32)


def _shift_rows(x, k):
    return pltpu.roll(x, k % TM, 0)


def _cumsum_rows(x, row, reverse):
    sh = 1
    while sh < TM:
        if reverse:
            x = x + jnp.where(row < TM - sh, _shift_rows(x, -sh), 0.0)
        else:
            x = x + jnp.where(row >= sh, _shift_rows(x, sh), 0.0)
        sh *= 2
    return x


def _anchors(cum, row, reverse):
    out = []
    z = cum
    for bit in range(N_LEVELS):
        h = 1 << bit
        if 2 * h >= 16:
            c3 = cum.reshape(TM // (2 * h), 2 * h, DK_B)
            p = h if reverse else h - 1
            out.append(jnp.broadcast_to(c3[:, p:p + 1, :], c3.shape).reshape(TM, DK_B))
            continue
        hi = (row & h) != 0
        if reverse:
            out.append(jnp.where(hi, z, _shift_rows(z, -h)))
            z = jnp.where(hi, _shift_rows(z, h), z)
        else:
            out.append(jnp.where(hi, _shift_rows(z, h), z))
            z = jnp.where(hi, z, _shift_rows(z, -h))
    return out


LOW_BITS = 5
SAFE_RANGE = 60.0


def _low_map(reverse):
    t = np.arange(TM)[:, None]
    s = np.arange(TM)[None, :]
    same = (t >> LOW_BITS) == (s >> LOW_BITS)
    ok = (t <= s) if reverse else (t >= s)
    return (same & ok).astype(np.int32)


def _block_anchor(cum, reverse):
    n = 1 << LOW_BITS
    c3 = cum.reshape(TM // n, n, DK_B)
    p = n - 1 if reverse else 0
    return jnp.broadcast_to(c3[:, p:p + 1, :], c3.shape).reshape(TM, DK_B)


def _hgrn_kernel(reverse, l, *refs):
    (q_ref, v_ref, f_ref, lbp_ref, s0_ref, lv_ref, lo_ref, _, o_ref, ns_ref) = refs[:10]
    st_ref, qs_ref, ks_ref, cs_ref, os_ref, as_ref = refs[-6:]
    g = pl.program_id(0)
    i = NT - 1 - g if reverse else g
    is_ctx = i < CTX_TILES
    pos = i % SMP_TILES
    seq_start = jnp.logical_and(i >= CTX_TILES, pos == (SMP_TILES - 1 if reverse else 0))

    @pl.when(is_ctx)
    def _():
        st_ref[...] = jnp.zeros_like(st_ref)

    @pl.when(seq_start)
    def _():
        for h in range(H_B):
            st_ref[h] = s0_ref[0, h].T

    d = 1 if reverse else 0
    rows = [lbp_ref[k * 2 + d:k * 2 + d + 1, :] for k in range(DEPTH)]
    mx = functools.reduce(jnp.maximum, rows)
    ex = [jnp.exp(r - mx) for r in rows]
    den = functools.reduce(jnp.add, ex)
    lb_all = jnp.zeros_like(mx)
    for k in range(1, l + 1):
        lb_all = lb_all + ex[k] / den

    row = lax.broadcasted_iota(jnp.int32, (TM, DK_B), 0)
    spread = jnp.zeros((1, DK_B), F32)
    for h in range(H_B):
        sl = slice(h * DK_B, (h + 1) * DK_B)
        lb = lb_all[:, sl]
        fg = lb + (1.0 - lb) * _sigmoid_small_accurate(f_ref[:, sl])
        cum = _cumsum_rows(jnp.log(fg), row, reverse)
        ks_ref[h] = 1.0 - fg
        qs_ref[h] = _silu(q_ref[:, sl])
        cs_ref[h] = cum
        spread = jnp.maximum(spread, jnp.max(_block_anchor(cum, reverse) - cum, axis=0, keepdims=True))
    single_anchor_ok = jnp.max(spread) < SAFE_RANGE

    half = TM // 2
    halves = (slice(0, half), slice(half, TM))
    late, early = (halves[0], halves[1]) if reverse else (halves[1], halves[0])

    lv = lv_ref[0:half, 0:half]

    def operands(h, fq, fk):
        return (qs_ref[h] * fq).astype(BF16), (ks_ref[h] * fk).astype(BF16)

    for h in range(H_B):
        sl = slice(h * DK_B, (h + 1) * DK_B)
        q, kg, cum = qs_ref[h], ks_ref[h], cs_ref[h]
        vb = v_ref[:, sl].astype(BF16)
        tot = cum[0:1, :] if reverse else cum[TM - 1:TM, :]
        st = st_ref[h]
        o = _dot_nt((q * jnp.exp(cum)).astype(BF16), st.astype(BF16))
        anchors = _anchors(cum, row, reverse)
        fac = jnp.exp(-jnp.abs(cum - anchors[N_LEVELS - 1]))
        qf, kf = operands(h, fac, fac)
        o_top = _dot(_dot_nt(qf[late], kf[early]).astype(BF16), vb[early])
        zero = jnp.zeros_like(o_top)
        os_ref[h] = o + jnp.concatenate([o_top, zero] if reverse else [zero, o_top], axis=0)
        a = [jnp.zeros((half, half), F32) for _ in halves]
        for bit in range(LOW_BITS, N_LEVELS - 1):
            fac = jnp.exp(-jnp.abs(cum - anchors[bit]))
            qf, kf = operands(h, fac, fac)
            a = [jnp.where(lv == bit + 1, _dot_nt(qf[r], kf[r]), a[b]) for b, r in enumerate(halves)]
        for b in range(2):
            as_ref[h, b] = a[b]
        kbar = (kg * jnp.exp(tot - cum)).astype(BF16)
        st_ref[h] = jnp.exp(tot) * st + _dot_tn(vb, kbar)

    def heads(single_anchor):
        for h in range(H_B):
            sl = slice(h * DK_B, (h + 1) * DK_B)
            cum = cs_ref[h]
            v = v_ref[:, sl]
            vb = v.astype(BF16)
            o = os_ref[h]
            a = [as_ref[h, b] for b in range(2)]
            if single_anchor:
                e = _block_anchor(cum, reverse)
                qf, kf = operands(h, jnp.exp(cum - e), jnp.exp(e - cum))
                lo = lo_ref[0:half, 0:half] != 0
                a = [jnp.where(lo, _dot_nt(qf[r], kf[r]), a[b]) for b, r in enumerate(halves)]
            else:
                o = o + jnp.sum(qs_ref[h] * ks_ref[h], axis=-1, keepdims=True) * v
                anchors = _anchors(cum, row, reverse)
                for bit in range(LOW_BITS):
                    fac = jnp.exp(-jnp.abs(cum - anchors[bit]))
                    qf, kf = operands(h, fac, fac)
                    a = [jnp.where(lv == bit + 1, _dot_nt(qf[r], kf[r]), a[b]) for b, r in enumerate(halves)]
            o_ref[:, sl] = o + jnp.concatenate([_dot(a[b].astype(BF16), vb[r]) for b, r in enumerate(halves)],
                                               axis=0)

    @pl.when(single_anchor_ok)
    def _():
        heads(True)

    @pl.when(jnp.logical_not(single_anchor_ok))
    def _():
        heads(False)

    @pl.when(is_ctx)
    def _():
        for h in range(H_B):
            ns_ref[0, 0, 0, h] = st_ref[h].T


def _hgrn_pass(reverse, l, u, hgrn_lb, s0, maps, new_state):
    tile = (lambda g: NT - 1 - g) if reverse else (lambda g: g)
    col = lambda c: pl.BlockSpec((TM, CB), lambda g: (tile(g), c))
    s0_spec = pl.BlockSpec((1, H_B, DK_B, DK_B),
                           lambda g: (jnp.clip((tile(g) - CTX_TILES) // SMP_TILES, 0, DEC_BATCH - 1), 0, 0, 0))
    lbp_spec = pl.BlockSpec((DEPTH * 2, D_B), lambda g: (0, 0))
    map_spec = pl.BlockSpec((TM, TM), lambda g: (0, 0))
    d = 1 if reverse else 0
    ns_spec = pl.BlockSpec((1, 1, 1, H_B, DK_B, DK_B),
                           lambda g: (jnp.minimum(tile(g), CTX_TILES - 1), l, d, 0, 0, 0))
    any_spec = pl.BlockSpec(memory_space=pl.ANY)
    f_col = C_HFB if reverse else C_HFF
    in_specs = [col(C_HQ), col(C_HI), col(f_col), lbp_spec, s0_spec, map_spec, map_spec, any_spec]
    args = (u, u, u, hgrn_lb, s0, *maps, new_state)
    o_shape = jax.ShapeDtypeStruct((T, D_B), F32)
    head_tile = pltpu.VMEM((H_B, TM, DK_B), F32)
    return pl.pallas_call(
        functools.partial(_hgrn_kernel, reverse, l),
        grid=(NT,),
        in_specs=in_specs,
        out_specs=[pl.BlockSpec((TM, D_B), lambda g: (tile(g), 0)), ns_spec],
        out_shape=[o_shape, jax.ShapeDtypeStruct(new_state.shape, F32)],
        input_output_aliases={len(args) - 1: 1},
        scratch_shapes=[pltpu.VMEM((H_B, DK_B, DK_B), F32), head_tile, head_tile, head_tile, head_tile,
                        pltpu.VMEM((H_B, 2, TM // 2, TM // 2), F32)],
        compiler_params=_cparams(("arbitrary",)),
        name="hgrn_bwd" if reverse else "hgrn_fwd",
    )(*args)


def _qk_kernel(aq_ref, ak_ref, av_ref, qn_ref, kn_ref, cos_ref, sin_ref, bd_ref, _k, _v, qt_ref, kb_ref, vt_ref,
               nk_ref, nv_ref):
    bd = bd_ref[...]
    lane = lax.broadcasted_iota(jnp.int32, (TM, 128), 1)
    even = (lane & 1) == 0
    cos = cos_ref[...]
    sin = sin_ref[...]

    def seg_norm(x, g):
        ss = _dot((x * x).astype(BF16), bd)
        return x * lax.rsqrt(ss * (1.0 / DH_C) + 1e-6) * g

    def rope(x):
        parts = []
        for s in range(D_C // 128):
            xs = x[:, s * 128:(s + 1) * 128]
            sw = jnp.where(even, pltpu.roll(xs, 127, 1), pltpu.roll(xs, 1, 1))
            parts.append(xs * cos + sw * sin)
        return jnp.concatenate(parts, axis=1)

    q = seg_norm(aq_ref[...], qn_ref[...])
    k = seg_norm(ak_ref[...], kn_ref[...])
    @pl.when(pl.program_id(0) < CTX_TILES)
    def _():
        nk_ref[0, 0] = k
        nv_ref[0, 0] = av_ref[...]

    qt_ref[...] = (rope(q) * (DH_C ** -0.5 * math.log2(math.e))).T.astype(BF16)
    kb_ref[...] = rope(k).astype(BF16)
    vt_ref[0] = av_ref[...].T.astype(BF16)


def _rope_tables():
    rows = DEC_SEQ // GRID_W
    row = jnp.broadcast_to(jnp.arange(rows)[:, None], (rows, GRID_W)).reshape(-1).astype(F32)
    colp = jnp.broadcast_to(jnp.arange(GRID_W)[None, :], (rows, GRID_W)).reshape(-1).astype(F32)
    half = DH_C // 2
    inv = ROPE_BASE ** (-jnp.arange(0, half, 2, dtype=F32) / half)
    ang = jnp.concatenate([row[:, None] * inv, colp[:, None] * inv], axis=-1)
    cos = jnp.repeat(jnp.cos(ang), 2, axis=-1)
    sin = jnp.repeat(jnp.sin(ang), 2, axis=-1) * jnp.tile(jnp.array([-1.0, 1.0], F32), half)
    cos = jnp.concatenate([jnp.ones((TM, DH_C), F32), cos], axis=0)
    sin = jnp.concatenate([jnp.zeros((TM, DH_C), F32), sin], axis=0)
    return jnp.tile(cos, (1, 2)), jnp.tile(sin, (1, 2))


def _qk_prep(l, u, q_norm_l, k_norm_l, cos_t, sin_t, bd, new_k, new_v):
    col = lambda c: pl.BlockSpec((TM, CB), lambda i: (i, c))
    vec = pl.BlockSpec((1, D_C), lambda i: (0, 0))
    tab = pl.BlockSpec((TM, 128), lambda i: (jnp.where(i < CTX_TILES, 0, 1 + i % SMP_TILES), 0))
    any_spec = pl.BlockSpec(memory_space=pl.ANY)
    cache = pl.BlockSpec((1, 1, SEQ, D_C), lambda i: (jnp.minimum(i, CTX_TILES - 1), l, 0, 0))
    return pl.pallas_call(
        _qk_kernel,
        grid=(NT,),
        in_specs=[col(C_AQ), col(C_AK), col(C_AV), vec, vec, tab, tab,
                  pl.BlockSpec((D_C, D_C), lambda i: (0, 0)), any_spec, any_spec],
        out_specs=[pl.BlockSpec((D_C, TM), lambda i: (0, i)), pl.BlockSpec((TM, D_C), lambda i: (i, 0)),
                   pl.BlockSpec((1, D_C, TM), lambda i: (i, 0, 0)), cache, cache],
        out_shape=[jax.ShapeDtypeStruct((D_C, T), BF16), jax.ShapeDtypeStruct((T, D_C), BF16),
                   jax.ShapeDtypeStruct((NT, D_C, TM), BF16),
                   jax.ShapeDtypeStruct(new_k.shape, F32), jax.ShapeDtypeStruct(new_v.shape, F32)],
        input_output_aliases={8: 3, 9: 4},
        compiler_params=_cparams(("arbitrary",)),
        name="qk_prep",
    )(u, u, u, jnp.tile(q_norm_l, D_C // DH_C).reshape(1, D_C), jnp.tile(k_norm_l, D_C // DH_C).reshape(1, D_C),
      cos_t, sin_t, bd, new_k, new_v)


def _attn_kernel(lam_init, n_new, has_cache, n_heads, *refs):
    if has_cache:
        qt_ref, kc_ref, vc_ref, kn_ref, vn_ref, lam_ref, sub_ref, _, o_ref = refs[:9]
    else:
        qt_ref, kn_ref, vn_ref, lam_ref, sub_ref, _, o_ref = refs[:7]
    qm_ref, s_ref, p_ref, m_ref, l_ref, a_ref, acc_ref = refs[-7:]
    tq = qt_ref.shape[1]
    ncb = tq // 128
    lq = lam_ref[...]
    lam = (jnp.exp(jnp.sum(lq[0:1] * lq[1:2], axis=-1, keepdims=True))
           - jnp.exp(jnp.sum(lq[2:3] * lq[3:4], axis=-1, keepdims=True)) + lam_init)

    def scores(k, buf):
        for c in range(2):
            s = _dot(k, qm_ref[c])
            for cb in range(ncb):
                s_ref[buf, c, cb] = s[:, cb * 128:(cb + 1) * 128]

    def softmax_update(buf):
        for c in range(2):
            for cb in range(ncb):
                cols = slice(cb * 128, (cb + 1) * 128)
                s = s_ref[buf, c, cb]
                m_old = m_ref[c, :, cols]
                m_new = jnp.maximum(m_old, jnp.max(s, axis=0, keepdims=True))
                alpha = jnp.exp2(m_old - m_new)
                p = jnp.exp2(s - m_new)
                l_ref[c, :, cols] = alpha * l_ref[c, :, cols] + jnp.sum(p, axis=0, keepdims=True)
                m_ref[c, :, cols] = m_new
                a_ref[buf, c, :, cols] = alpha
                p_ref[buf, c, cb] = p.astype(BF16)

    def accumulate(vt, buf):
        for c in range(2):
            p = jnp.concatenate([p_ref[buf, c, cb] for cb in range(ncb)], axis=1)
            acc_ref[c] = a_ref[buf, c] * acc_ref[c] + _dot(vt, p)

    for h in range(n_heads):
        hs = slice(h * 128, (h + 1) * 128)
        qt = qt_ref[hs, :]
        sub_i = lax.broadcasted_iota(jnp.int32, qt.shape, 0)
        qm_ref[0] = jnp.where(sub_i < DH_C, qt, jnp.zeros_like(qt))
        qm_ref[1] = jnp.where(sub_i >= DH_C, qt, jnp.zeros_like(qt))
        m_ref[...] = jnp.full(m_ref.shape, -1e30, F32)
        l_ref[...] = jnp.zeros_like(l_ref)
        acc_ref[...] = jnp.zeros_like(acc_ref)

        if n_new == 1:
            scores(kn_ref[:, hs], 0)
            softmax_update(0)
            accumulate(vn_ref[0, hs, :], 0)
        else:
            new_k = lambda j: kn_ref[pl.ds(pl.multiple_of(j * TM, TM), TM), hs]
            scores(new_k(0), 0)

            def body(i, carry):
                scores(new_k(2 * i + 1), 1)
                softmax_update(0)
                accumulate(vn_ref[2 * i, hs, :], 0)
                scores(new_k(2 * i + 2), 0)
                softmax_update(1)
                accumulate(vn_ref[2 * i + 1, hs, :], 1)
                return carry
            lax.fori_loop(0, n_new // 2 - 1, body, 0)
            scores(kn_ref[(n_new - 1) * TM:, hs], 1)
            softmax_update(0)
            accumulate(vn_ref[n_new - 2, hs, :], 0)
            scores(kc_ref[:, hs], 0)
            softmax_update(1)
            accumulate(vn_ref[n_new - 1, hs, :], 1)
            softmax_update(0)
            accumulate(vc_ref[0, hs, :], 0)

        for cb in range(ncb):
            cols = slice(cb * 128, (cb + 1) * 128)
            o = acc_ref[0, :, cols] / l_ref[0, :, cols] - lam * (acc_ref[1, :, cols] / l_ref[1, :, cols])
            y = o * lax.rsqrt(jnp.mean(o * o, axis=0, keepdims=True) + 1e-6)
            o_ref[cols, hs] = (y.T * sub_ref[...] * (1.0 - lam_init)).astype(BF16)


def _attn_scratch(tq):
    ncb = tq // 128
    return [pltpu.VMEM((2, 128, tq), BF16), pltpu.VMEM((2, 2, ncb, TM, 128), F32),
            pltpu.VMEM((2, 2, ncb, TM, 128), BF16),
            pltpu.VMEM((2, 1, tq), F32), pltpu.VMEM((2, 1, tq), F32), pltpu.VMEM((2, 2, 1, tq), F32),
            pltpu.VMEM((2, 128, tq), F32)]


def _attention(l, qt, kb, vt, ck, cvt, lambda_qk_l, subln_l):
    lam_init = 0.8 - 0.6 * math.exp(-0.3 * l)
    sub = subln_l.reshape(1, 2 * DH_C)
    o_ctx = pl.pallas_call(
        functools.partial(_attn_kernel, lam_init, 1, False, H_C),
        grid=(BATCH,),
        in_specs=[pl.BlockSpec((D_C, SEQ), lambda b: (0, b)),
                  pl.BlockSpec((SEQ, D_C), lambda b: (b, 0)),
                  pl.BlockSpec((1, D_C, TM), lambda b: (b, 0, 0)),
                  pl.BlockSpec((4, DH_C), lambda b: (0, 0)),
                  pl.BlockSpec((1, 2 * DH_C), lambda b: (0, 0)),
                  pl.BlockSpec(memory_space=pl.ANY)],
        out_specs=pl.BlockSpec((SEQ, D_C), lambda b: (b, 0)),
        out_shape=jax.ShapeDtypeStruct((T, D_C), BF16),
        input_output_aliases={5: 0},
        scratch_shapes=_attn_scratch(SEQ),
        compiler_params=_cparams(("arbitrary",)),
        name="attn_ctx",
    )(qt, kb, vt, lambda_qk_l, sub, jnp.zeros((T, D_C), BF16))

    nq = DEC_SEQ // TQ
    base_q = BATCH * SEQ // TQ
    base_kv = BATCH * SEQ // DEC_SEQ
    o_smp = pl.pallas_call(
        functools.partial(_attn_kernel, lam_init, SMP_TILES, True, 1),
        grid=(DEC_BATCH, H_C, nq),
        in_specs=[pl.BlockSpec((128, TQ), lambda b, h, j: (h, base_q + b * nq + j)),
                  pl.BlockSpec((PAST, 128), lambda b, h, j: (b, h)),
                  pl.BlockSpec((1, 128, PAST), lambda b, h, j: (b, h, 0)),
                  pl.BlockSpec((DEC_SEQ, 128), lambda b, h, j: (base_kv + b, h)),
                  pl.BlockSpec((SMP_TILES, 128, TM), lambda b, h, j: (base_kv + b, h, 0)),
                  pl.BlockSpec((4, DH_C), lambda b, h, j: (0, 0)),
                  pl.BlockSpec((1, 2 * DH_C), lambda b, h, j: (0, 0)),
                  pl.BlockSpec(memory_space=pl.ANY)],
        out_specs=pl.BlockSpec((TQ, 128), lambda b, h, j: (base_q + b * nq + j, h)),
        out_shape=jax.ShapeDtypeStruct((T, D_C), BF16),
        input_output_aliases={7: 0},
        scratch_shapes=_attn_scratch(TQ),
        compiler_params=_cparams(("arbitrary", "arbitrary", "arbitrary")),
        name="attn_smp",
    )(qt, ck, cvt, kb, vt, lambda_qk_l, sub, o_ctx)
    return o_smp


def _merge_ffn_kernel(nx, *refs):
    (a_ref, bf_ref, bb_ref, hg_ref, hn_ref, c_ref, ga_ref, gb_ref, gc_ref, g1_ref, wb_ref, wo_ref,
     n2_ref, sc_ref, sh_ref, g2_ref, wi_ref, wd_ref) = refs[nx:nx + 18]
    o_refs = refs[nx + 18:]
    hb = []
    for h in range(H_B):
        hs = slice(h * DK_B, (h + 1) * DK_B)
        o = bf_ref[:, hs] + bb_ref[:, hs]
        y = o * lax.rsqrt(jnp.mean(o * o, axis=-1, keepdims=True) + 1e-6) * hn_ref[...]
        hb.append((y * _silu(hg_ref[:, hs])).astype(BF16))
    ya = _dot(a_ref[...], wb_ref[0:D_A, :])
    yb = _dot(jnp.concatenate(hb, axis=1), wb_ref[D_A:D_A + D_B, :])
    yc = _dot(c_ref[...], wb_ref[D_A + D_B:, :])
    m = _sigmoid(ga_ref[...]) * ya + _sigmoid(gb_ref[...]) * yb + _sigmoid(gc_ref[...]) * yc
    x1 = _x_tile(refs[:nx]) + g1_ref[0] * _dot(m.astype(BF16), wo_ref[...])
    h = _rms_mod(x1, n2_ref[...], sc_ref[0], sh_ref[0]).astype(BF16)
    gu = _dot(h, wi_ref[...])
    ff = _silu(gu[:, :D_FF]) * gu[:, D_FF:]
    y = x1 + g2_ref[0] * _dot(ff.astype(BF16), wd_ref[...])
    if len(o_refs) == 1:
        o_refs[0][...] = y
    else:
        @pl.when(pl.program_id(0) < CTX_TILES)
        def _():
            o_refs[0][...] = y

        @pl.when(pl.program_id(0) >= CTX_TILES)
        def _():
            o_refs[1][...] = y


def _merge_ffn(ya, of, ob, hgrn_norm_l, yc, u, x, modt, w_branch_l, w_out_l, norm2_l, w_ffn_in_l, w_ffn_out_l, l,
               split_out):
    row = lambda w: pl.BlockSpec((TM, w), lambda i: (i, 0))
    gate0 = C_GT * CB // D
    gate = lambda k: pl.BlockSpec((TM, D), lambda i: (i, gate0 + k))
    resident = lambda shape: pl.BlockSpec(shape, lambda i: (0, 0), pipeline_mode=pl.Buffered(1))
    x_specs, xs = _x_specs(x)
    if split_out:
        out_specs, _ = _x_specs((None, None))
        out_shape = [jax.ShapeDtypeStruct((BATCH * SEQ, D), F32), jax.ShapeDtypeStruct((DEC_BATCH * DEC_SEQ, D), F32)]
    else:
        out_specs = row(D)
        out_shape = jax.ShapeDtypeStruct((T, D), F32)
    return pl.pallas_call(
        functools.partial(_merge_ffn_kernel, len(xs)),
        grid=(NT,),
        in_specs=x_specs + [row(D_A), row(D_B), row(D_B), pl.BlockSpec((TM, CB), lambda i: (i, C_HG)),
                            pl.BlockSpec((1, DK_B), lambda i: (0, 0)), row(D_C),
                            gate(0), gate(1), gate(2), _mod_spec(l, 2),
                            resident((D_A + D_B + D_C, D)), resident((D, D)),
                            pl.BlockSpec((1, D), lambda i: (0, 0)),
                            _mod_spec(l, 4), _mod_spec(l, 3), _mod_spec(l, 5),
                            resident((D, 2 * D_FF)), resident((D_FF, D))],
        out_specs=out_specs,
        out_shape=out_shape,
        compiler_params=_cparams(("arbitrary",)),
        name="merge_ffn",
    )(*xs, ya, of, ob, u, hgrn_norm_l.reshape(1, DK_B), yc, u, u, u, modt, w_branch_l, w_out_l,
      norm2_l.reshape(1, D), modt, modt, modt, w_ffn_in_l, w_ffn_out_l)


def _layer(l, x, modt, cache_k, cache_v, state_hgrn, tabs, p, outs):
    (norm1, norm2, w_in, conv_w, conv_b, conv_ln_g, conv_ln_b, hgrn_lb, hgrn_norm, q_norm, k_norm, lambda_qk,
     subln, w_branch, w_out, w_ffn_in, w_ffn_out) = p
    cos_t, sin_t, bd, maps_f, maps_b = tabs
    new_k, new_v, new_s = outs
    u = _in_proj(x, modt, norm1[l], w_in[l].astype(BF16), l)
    ya = _conv_branch(u, conv_w[l], conv_b[l], conv_ln_g[l], conv_ln_b[l])
    lbp = hgrn_lb.reshape(DEPTH * 2, D_B)
    ob, new_s = _hgrn_pass(True, l, u, lbp, state_hgrn[:, l, 1], maps_b, new_s)
    of, new_s = _hgrn_pass(False, l, u, lbp, state_hgrn[:, l, 0], maps_f, new_s)
    qt, kb, vt, new_k, new_v = _qk_prep(l, u, q_norm[l], k_norm[l], cos_t, sin_t, bd, new_k, new_v)
    ck = cache_k[:, l].reshape(DEC_BATCH * PAST, D_C).astype(BF16)
    cvt = cache_v[:, l].reshape(DEC_BATCH, PAST, D_C).astype(BF16).transpose(0, 2, 1)
    yc = _attention(l, qt, kb, vt, ck, cvt, lambda_qk[l], subln[l])
    x = _merge_ffn(ya, of, ob, hgrn_norm[l], yc, u, x, modt, w_branch[l].astype(BF16), w_out[l].astype(BF16), norm2[l],
                   w_ffn_in[l].astype(BF16), w_ffn_out[l].astype(BF16), l, split_out=l == DEPTH - 1)
    return x, (new_k, new_v, new_s)


def kernel(x_prompt, x_sample, cache_k, cache_v, state_hgrn, c, c_ctx, w_mod, b_mod, norm1, norm2, w_in, conv_w,
           conv_b, conv_ln_g, conv_ln_b, hgrn_lb, hgrn_norm, q_norm, k_norm, lambda_qk, subln, w_branch, w_out,
           w_ffn_in, w_ffn_out):
    x = (x_prompt.reshape(BATCH * SEQ, D), x_sample.reshape(DEC_BATCH * DEC_SEQ, D))
    cv8 = jnp.concatenate([c_ctx[None, :], c, jnp.zeros((8 - 1 - DEC_BATCH, D), F32)], axis=0)
    mod = _modulation(cv8, w_mod, b_mod)
    modt = mod.reshape(DEPTH, 8, 6, D).transpose(0, 2, 1, 3).reshape(DEPTH * 6 * 8, 1, D)
    cos_t, sin_t = _rope_tables()
    seg = np.arange(D_C) // DH_C
    bd = jnp.asarray(seg[:, None] == seg[None, :], BF16)
    maps = lambda rev: (jnp.asarray(_level_map(rev)), jnp.asarray(_low_map(rev)))
    tabs = (cos_t, sin_t, bd, maps(False), maps(True))
    p = (norm1, norm2, w_in, conv_w, conv_b, conv_ln_g, conv_ln_b, hgrn_lb, hgrn_norm, q_norm, k_norm, lambda_qk,
         subln, w_branch, w_out, w_ffn_in, w_ffn_out)
    outs = (jnp.zeros((BATCH, DEPTH, SEQ, D_C), F32), jnp.zeros((BATCH, DEPTH, SEQ, D_C), F32),
            jnp.zeros((BATCH, DEPTH, 2, H_B, DK_B, DK_B), F32))
    for l in range(DEPTH):
        x, outs = _layer(l, x, modt, cache_k, cache_v, state_hgrn, tabs, p, outs)
    new_k, new_v, new_s = outs
    y_prompt = x[0].reshape(BATCH, SEQ, D)
    y_sample = x[1].reshape(DEC_BATCH, DEC_SEQ, D)
    return (y_prompt, y_sample, new_k.reshape(BATCH, DEPTH, SEQ, H_C, 2, DH_C),
            new_v.reshape(BATCH, DEPTH, SEQ, H_C, 2 * DH_C), new_s)
```

```python
import functools
import math

import numpy as np
import jax
import jax.numpy as jnp
from jax import lax
from jax.experimental import pallas as pl
from jax.experimental.pallas import tpu as pltpu

F32 = jnp.float32
BF16 = jnp.bfloat16

D = 1024
BATCH, SEQ = 32, 256
DEPTH = 2
DEC_BATCH, DEC_SEQ = 2, 4096
PAST = 256
GRID_W = 64
D_A, CONV_K = 512, 31
H_B, DK_B = 4, 128
D_B = 512
H_C, DH_C = 4, 64
D_C = 512
ROPE_BASE = 10000.0
D_FF = 2816
IN_COLS = 8192

TM = 256
CTX_TILES = BATCH * SEQ // TM
SMP_TILES = DEC_SEQ // TM
T = BATCH * SEQ + DEC_BATCH * DEC_SEQ
NT = T // TM
CB = 512
HALO = 16
TQ = 512
VMEM_LIMIT = 56 * 1024 * 1024

C_A, C_AG, C_HQ, C_HI, C_HFF, C_HFB, C_HG, C_AQ, C_AK, C_AV, C_GT = 0, 1, 2, 3, 4, 5, 6, 7, 8, 9, 10


def _cparams(sem):
    return pltpu.CompilerParams(dimension_semantics=sem, vmem_limit_bytes=VMEM_LIMIT)


def _mod_row(i):
    return jnp.where(i < CTX_TILES, 0, 1 + (i - CTX_TILES) // SMP_TILES)


def _sigmoid(x):
    return 0.5 * jnp.tanh(0.5 * x) + 0.5


def _sigmoid_small_accurate(x):
    return jnp.exp(-jnp.log(1.0 + jnp.exp(-x)))


def _silu(x):
    return x * _sigmoid(x)


def _dot(a, b):
    return jnp.dot(a, b, preferred_element_type=F32)


def _dot_nt(a, b):
    return lax.dot_general(a, b, (((1,), (1,)), ((), ())), preferred_element_type=F32)


def _dot_tn(a, b):
    return lax.dot_general(a, b, (((0,), (0,)), ((), ())), preferred_element_type=F32)


def _split_bf16(x):
    hi = x.astype(BF16)
    lo = (x - hi.astype(F32)).astype(BF16)
    return hi, lo


MOD_TN = 1536


def _mod_kernel(cv_ref, w_ref, b_ref, o_ref):
    a = _silu(cv_ref[...])
    a_hi, a_lo = _split_bf16(a)
    w_hi, w_lo = _split_bf16(w_ref[0])
    acc = _dot(a_hi, w_hi) + _dot(a_lo, w_hi) + _dot(a_hi, w_lo)
    o_ref[0] = acc + b_ref[0]


def _modulation(cv8, w_mod, b_mod):
    return pl.pallas_call(
        _mod_kernel,
        grid=(DEPTH, 6 * D // MOD_TN),
        in_specs=[
            pl.BlockSpec((8, D), lambda l, j: (0, 0)),
            pl.BlockSpec((1, D, MOD_TN), lambda l, j: (l, 0, j)),
            pl.BlockSpec((1, 1, MOD_TN), lambda l, j: (l, 0, j)),
        ],
        out_specs=pl.BlockSpec((1, 8, MOD_TN), lambda l, j: (l, 0, j)),
        out_shape=jax.ShapeDtypeStruct((DEPTH, 8, 6 * D), F32),
        compiler_params=_cparams(("arbitrary", "arbitrary")),
        name="adaln_mod",
    )(cv8, w_mod, b_mod.reshape(DEPTH, 1, 6 * D))


def _mod_spec(l, which):
    return pl.BlockSpec((1, 1, D), lambda i: ((l * 6 + which) * 8 + _mod_row(i), 0, 0))


def _rms_mod(x, g, sc, sh):
    y = x * lax.rsqrt(jnp.mean(x * x, axis=-1, keepdims=True) + 1e-6) * g
    return y * (1.0 + sc) + sh


def _x_specs(x):
    if isinstance(x, tuple):
        return [pl.BlockSpec((TM, D), lambda i: (jnp.minimum(i, CTX_TILES - 1), 0)),
                pl.BlockSpec((TM, D), lambda i: (jnp.maximum(i - CTX_TILES, 0), 0))], list(x)
    return [pl.BlockSpec((TM, D), lambda i: (i, 0))], [x]


def _x_tile(x_refs):
    if len(x_refs) == 2:
        return jnp.where(pl.program_id(0) < CTX_TILES, x_refs[0][...], x_refs[1][...])
    return x_refs[0][...]


IN_TM = 512
IN_SPLIT = 2


IN_HALF = IN_COLS // IN_SPLIT
Q_OFF = C_AQ * CB
K_OFF = C_AK * CB - IN_HALF
assert IN_SPLIT == 2 and Q_OFF + CB == IN_HALF and K_OFF == 0 and C_AV == C_AK + 1


def _in_kernel(nx, *refs):
    (g_ref, sc_ref, sh_ref, w_ref, qn_ref, kn_ref, cos_ref, sin_ref, bd_ref, _k, _v,
     o_ref, qt_ref, kb_ref, vt_ref, nk_ref, nv_ref) = refs[nx:]
    j = pl.program_id(0)
    i = pl.program_id(1)
    n_ctx = BATCH * SEQ // IN_TM
    if nx == 2:
        x = jnp.where(i < n_ctx, refs[0][...], refs[1][...])
    else:
        x = refs[0][...]
    h = _rms_mod(x, g_ref[...], sc_ref[0], sh_ref[0]).astype(BF16)
    bd = bd_ref[...]
    lane = lax.broadcasted_iota(jnp.int32, (IN_TM, 128), 1)
    even = (lane & 1) == 0
    cos = cos_ref[...]
    sin = sin_ref[...]

    def seg_norm(xq, gain):
        ss = _dot((xq * xq).astype(BF16), bd)
        return xq * lax.rsqrt(ss * (1.0 / DH_C) + 1e-6) * gain

    def rope(xq):
        parts = []
        for s in range(D_C // 128):
            xs = xq[:, s * 128:(s + 1) * 128]
            sw = jnp.where(even, pltpu.roll(xs, 127, 1), pltpu.roll(xs, 1, 1))
            parts.append(xs * cos + sw * sin)
        return jnp.concatenate(parts, axis=1)

    @pl.when(j == 0)
    def _():
        aq = _dot(h, w_ref[:, Q_OFF:])
        o_ref[:, Q_OFF:] = aq
        o_ref[:, :Q_OFF] = _dot(h, w_ref[:, :Q_OFF])
        q = seg_norm(aq, qn_ref[...])
        qt_ref[...] = (rope(q) * (DH_C ** -0.5 * math.log2(math.e))).T.astype(BF16)

    @pl.when(j == 1)
    def _():
        akv = _dot(h, w_ref[:, :2 * CB])
        o_ref[:, :2 * CB] = akv
        o_ref[:, 2 * CB:] = _dot(h, w_ref[:, 2 * CB:])
        av = akv[:, CB:]
        k = seg_norm(akv[:, :CB], kn_ref[...])
        kb_ref[...] = rope(k).astype(BF16)
        for t in range(IN_TM // TM):
            vt_ref[t] = av[t * TM:(t + 1) * TM, :].T.astype(BF16)

        @pl.when(i < n_ctx)
        def _():
            for t in range(IN_TM // SEQ):
                nk_ref[t, 0] = k[t * SEQ:(t + 1) * SEQ, :]
                nv_ref[t, 0] = av[t * SEQ:(t + 1) * SEQ, :]


def _in_proj(x, modt, norm1_l, w_in_l, l, q_norm_l, k_norm_l, cos_t, sin_t, bd, new_k, new_v):
    n_ctx = BATCH * SEQ // IN_TM
    per_seq = DEC_SEQ // IN_TM
    n_i = T // IN_TM
    mod_row = lambda i: jnp.where(i < n_ctx, 0, 1 + (i - n_ctx) // per_seq)
    mod = lambda which: pl.BlockSpec((1, 1, D), lambda j, i: ((l * 6 + which) * 8 + mod_row(i), 0, 0))
    if isinstance(x, tuple):
        x_specs = [pl.BlockSpec((IN_TM, D), lambda j, i: (jnp.minimum(i, n_ctx - 1), 0)),
                   pl.BlockSpec((IN_TM, D), lambda j, i: (jnp.maximum(i - n_ctx, 0), 0))]
        xs = list(x)
    else:
        x_specs, xs = [pl.BlockSpec((IN_TM, D), lambda j, i: (i, 0))], [x]
    vec = pl.BlockSpec((1, D_C), lambda j, i: (0, 0))
    tab = pl.BlockSpec((IN_TM, 128), lambda j, i: (jnp.where(i < n_ctx, 0, 1 + (i - n_ctx) % per_seq), 0))
    any_spec = pl.BlockSpec(memory_space=pl.ANY)
    q_i = lambda j, i: jnp.where(j == 0, i, n_i - 1)
    kv_i = lambda j, i: jnp.where(j == 1, i, 0)
    seqs = IN_TM // SEQ
    cache = pl.BlockSpec((seqs, 1, SEQ, D_C), lambda j, i: (jnp.minimum(kv_i(j, i), n_ctx - 1), l, 0, 0))
    n_in = len(xs) + 9
    return pl.pallas_call(
        functools.partial(_in_kernel, len(xs)),
        grid=(IN_SPLIT, n_i),
        in_specs=x_specs + [
            pl.BlockSpec((1, D), lambda j, i: (0, 0)),
            mod(1),
            mod(0),
            pl.BlockSpec((D, IN_HALF), lambda j, i: (0, j)),
            vec, vec, tab, tab,
            pl.BlockSpec((D_C, D_C), lambda j, i: (0, 0)),
            any_spec, any_spec,
        ],
        out_specs=[pl.BlockSpec((IN_TM, IN_HALF), lambda j, i: (i, j)),
                   pl.BlockSpec((D_C, IN_TM), lambda j, i: (0, q_i(j, i))),
                   pl.BlockSpec((IN_TM, D_C), lambda j, i: (kv_i(j, i), 0)),
                   pl.BlockSpec((IN_TM // TM, D_C, TM), lambda j, i: (kv_i(j, i), 0, 0)),
                   cache, cache],
        out_shape=[jax.ShapeDtypeStruct((T, IN_COLS), F32), jax.ShapeDtypeStruct((D_C, T), BF16),
                   jax.ShapeDtypeStruct((T, D_C), BF16), jax.ShapeDtypeStruct((NT, D_C, TM), BF16),
                   jax.ShapeDtypeStruct(new_k.shape, F32), jax.ShapeDtypeStruct(new_v.shape, F32)],
        input_output_aliases={n_in: 4, n_in + 1: 5},
        compiler_params=_cparams(("arbitrary", "arbitrary")),
        name="in_proj",
    )(*xs, norm1_l.reshape(1, D), modt, modt, w_in_l,
      jnp.tile(q_norm_l, D_C // DH_C).reshape(1, D_C), jnp.tile(k_norm_l, D_C // DH_C).reshape(1, D_C),
      cos_t, sin_t, bd, new_k, new_v)


CONV_RB = 32


def _conv_kernel(ac_ref, gc_ref, ap_ref, gp_ref, an_ref, gn_ref, w_ref, b_ref, lg_ref, lb_ref, o_ref, sh_ref,
                 cv_ref):
    i = pl.program_id(0)
    in_smp = i >= CTX_TILES
    pos = i % SMP_TILES
    has_prev = jnp.logical_and(in_smp, pos != 0)
    has_next = jnp.logical_and(in_smp, pos != SMP_TILES - 1)
    n = TM + 2 * HALO
    sh_ref[0, HALO:HALO + TM, :] = ac_ref[...] * _sigmoid(gc_ref[...])
    sh_ref[0, 0:HALO, :] = jnp.where(has_prev, ap_ref[...] * _sigmoid(gp_ref[...]), 0.0)
    sh_ref[0, HALO + TM:, :] = jnp.where(has_next, an_ref[...] * _sigmoid(gn_ref[...]), 0.0)
    for r in range(1, 8):
        sh_ref[r, 0:n - 8, :] = sh_ref[0, r:r + n - 8, :]
    off = HALO - CONV_K // 2

    def body(rb, carry):
        base = pl.multiple_of(rb * CONV_RB, CONV_RB)
        acc = jnp.zeros((CONV_RB, D_A), F32)
        for k in range(CONV_K):
            r = (off + k) % 8
            acc = acc + sh_ref[r, pl.ds(base + (off + k - r), CONV_RB), :] * w_ref[k:k + 1, :]
        cv_ref[pl.ds(base, CONV_RB), :] = acc
        return carry
    lax.fori_loop(0, TM // CONV_RB, body, 0)
    acc = cv_ref[...] + b_ref[...]
    mu = jnp.mean(acc, axis=-1, keepdims=True)
    xc = acc - mu
    y = xc * lax.rsqrt(jnp.mean(xc * xc, axis=-1, keepdims=True) + 1e-5)
    y = y * lg_ref[...] + lb_ref[...]
    o_ref[...] = _silu(y).astype(BF16)


def _conv_branch(u, conv_w_l, conv_b_l, ln_g_l, ln_b_l):
    r = TM // HALO
    last = T // HALO - 1
    cur = lambda c: pl.BlockSpec((TM, CB), lambda i: (i, c))
    prev = lambda c: pl.BlockSpec((HALO, CB), lambda i: (jnp.maximum(i * r - 1, 0), c))
    nxt = lambda c: pl.BlockSpec((HALO, CB), lambda i: (jnp.minimum((i + 1) * r, last), c))
    vec = pl.BlockSpec((1, D_A), lambda i: (0, 0))
    return pl.pallas_call(
        _conv_kernel,
        grid=(NT,),
        in_specs=[cur(C_A), cur(C_AG), prev(C_A), prev(C_AG), nxt(C_A), nxt(C_AG),
                  pl.BlockSpec((CONV_K, D_A), lambda i: (0, 0)), vec, vec, vec],
        out_specs=pl.BlockSpec((TM, D_A), lambda i: (i, 0)),
        out_shape=jax.ShapeDtypeStruct((T, D_A), BF16),
        scratch_shapes=[pltpu.VMEM((8, TM + 2 * HALO, D_A), F32), pltpu.VMEM((TM, D_A), F32)],
        compiler_params=_cparams(("arbitrary",)),
        name="conv_branch",
    )(u, u, u, u, u, u, conv_w_l, conv_b_l.reshape(1, D_A), ln_g_l.reshape(1, D_A), ln_b_l.reshape(1, D_A))


N_LEVELS = 8


def _level_map(reverse):
    t = np.arange(TM)[:, None]
    s = np.arange(TM)[None, :]
    x = t ^ s
    lv = np.zeros((TM, TM), np.int32)
    for bit in range(N_LEVELS):
        lv[(x >> bit) == 1] = bit + 1
    ok = (t < s) if reverse else (t > s)
    return np.where(ok, lv, 0).astype(np.int32)


def _shift_rows(x, k):
    return pltpu.roll(x, k % TM, 0)


def _cumsum_rows(x, row, reverse):
    sh = 1
    while sh < TM:
        if reverse:
            x = x + jnp.where(row < TM - sh, _shift_rows(x, -sh), 0.0)
        else:
            x = x + jnp.where(row >= sh, _shift_rows(x, sh), 0.0)
        sh *= 2
    return x


def _anchors(cum, row, reverse):
    out = []
    z = cum
    for bit in range(N_LEVELS):
        h = 1 << bit
        if 2 * h >= 16:
            c3 = cum.reshape(TM // (2 * h), 2 * h, DK_B)
            p = h if reverse else h - 1
            out.append(jnp.broadcast_to(c3[:, p:p + 1, :], c3.shape).reshape(TM, DK_B))
            continue
        hi = (row & h) != 0
        if reverse:
            out.append(jnp.where(hi, z, _shift_rows(z, -h)))
            z = jnp.where(hi, _shift_rows(z, h), z)
        else:
            out.append(jnp.where(hi, _shift_rows(z, h), z))
            z = jnp.where(hi, z, _shift_rows(z, -h))
    return out


LOW_BITS = 5
SAFE_RANGE = 60.0


def _low_map(reverse):
    t = np.arange(TM)[:, None]
    s = np.arange(TM)[None, :]
    same = (t >> LOW_BITS) == (s >> LOW_BITS)
    ok = (t <= s) if reverse else (t >= s)
    return (same & ok).astype(np.int32)


def _block_anchor(cum, reverse):
    n = 1 << LOW_BITS
    c3 = cum.reshape(TM // n, n, DK_B)
    p = n - 1 if reverse else 0
    return jnp.broadcast_to(c3[:, p:p + 1, :], c3.shape).reshape(TM, DK_B)


def _hgrn_kernel(reverse, l, *refs):
    (q_ref, v_ref, f_ref, lbp_ref, s0_ref, lv_ref, lo_ref, _, o_ref, ns_ref) = refs[:10]
    st_ref, qs_ref, ks_ref, cs_ref, os_ref, as_ref = refs[-6:]
    g = pl.program_id(0)
    i = NT - 1 - g if reverse else g
    is_ctx = i < CTX_TILES
    pos = i % SMP_TILES
    seq_start = jnp.logical_and(i >= CTX_TILES, pos == (SMP_TILES - 1 if reverse else 0))

    @pl.when(is_ctx)
    def _():
        st_ref[...] = jnp.zeros_like(st_ref)

    @pl.when(seq_start)
    def _():
        for h in range(H_B):
            st_ref[h] = s0_ref[0, h].T

    d = 1 if reverse else 0
    rows = [lbp_ref[k * 2 + d:k * 2 + d + 1, :] for k in range(DEPTH)]
    mx = functools.reduce(jnp.maximum, rows)
    ex = [jnp.exp(r - mx) for r in rows]
    den = functools.reduce(jnp.add, ex)
    lb_all = jnp.zeros_like(mx)
    for k in range(1, l + 1):
        lb_all = lb_all + ex[k] / den

    row = lax.broadcasted_iota(jnp.int32, (TM, DK_B), 0)
    spread = jnp.zeros((1, DK_B), F32)
    for h in range(H_B):
        sl = slice(h * DK_B, (h + 1) * DK_B)
        lb = lb_all[:, sl]
        fg = lb + (1.0 - lb) * _sigmoid_small_accurate(f_ref[:, sl])
        cum = _cumsum_rows(jnp.log(fg), row, reverse)
        ks_ref[h] = 1.0 - fg
        qs_ref[h] = _silu(q_ref[:, sl])
        cs_ref[h] = cum
        spread = jnp.maximum(spread, jnp.max(_block_anchor(cum, reverse) - cum, axis=0, keepdims=True))
    single_anchor_ok = jnp.max(spread) < SAFE_RANGE

    half = TM // 2
    halves = (slice(0, half), slice(half, TM))
    late, early = (halves[0], halves[1]) if reverse else (halves[1], halves[0])

    lv = lv_ref[0:half, 0:half]

    def operands(h, fq, fk):
        return (qs_ref[h] * fq).astype(BF16), (ks_ref[h] * fk).astype(BF16)

    for h in range(H_B):
        sl = slice(h * DK_B, (h + 1) * DK_B)
        q, kg, cum = qs_ref[h], ks_ref[h], cs_ref[h]
        vb = v_ref[:, sl].astype(BF16)
        tot = cum[0:1, :] if reverse else cum[TM - 1:TM, :]
        st = st_ref[h]
        o = _dot_nt((q * jnp.exp(cum)).astype(BF16), st.astype(BF16))
        anchors = _anchors(cum, row, reverse)
        fac = jnp.exp(-jnp.abs(cum - anchors[N_LEVELS - 1]))
        qf, kf = operands(h, fac, fac)
        o_top = _dot(_dot_nt(qf[late], kf[early]).astype(BF16), vb[early])
        zero = jnp.zeros_like(o_top)
        os_ref[h] = o + jnp.concatenate([o_top, zero] if reverse else [zero, o_top], axis=0)
        a = [jnp.zeros((half, half), F32) for _ in halves]
        for bit in range(LOW_BITS, N_LEVELS - 1):
            fac = jnp.exp(-jnp.abs(cum - anchors[bit]))
            qf, kf = operands(h, fac, fac)
            a = [jnp.where(lv == bit + 1, _dot_nt(qf[r], kf[r]), a[b]) for b, r in enumerate(halves)]
        for b in range(2):
            as_ref[h, b] = a[b]
        kbar = (kg * jnp.exp(tot - cum)).astype(BF16)
        st_ref[h] = jnp.exp(tot) * st + _dot_tn(vb, kbar)

    def heads(single_anchor):
        for h in range(H_B):
            sl = slice(h * DK_B, (h + 1) * DK_B)
            cum = cs_ref[h]
            v = v_ref[:, sl]
            vb = v.astype(BF16)
            o = os_ref[h]
            a = [as_ref[h, b] for b in range(2)]
            if single_anchor:
                e = _block_anchor(cum, reverse)
                qf, kf = operands(h, jnp.exp(cum - e), jnp.exp(e - cum))
                lo = lo_ref[0:half, 0:half] != 0
                a = [jnp.where(lo, _dot_nt(qf[r], kf[r]), a[b]) for b, r in enumerate(halves)]
            else:
                o = o + jnp.sum(qs_ref[h] * ks_ref[h], axis=-1, keepdims=True) * v
                anchors = _anchors(cum, row, reverse)
                for bit in range(LOW_BITS):
                    fac = jnp.exp(-jnp.abs(cum - anchors[bit]))
                    qf, kf = operands(h, fac, fac)
                    a = [jnp.where(lv == bit + 1, _dot_nt(qf[r], kf[r]), a[b]) for b, r in enumerate(halves)]
            o_ref[:, sl] = o + jnp.concatenate([_dot(a[b].astype(BF16), vb[r]) for b, r in enumerate(halves)],
                                               axis=0)

    @pl.when(single_anchor_ok)
    def _():
        heads(True)

    @pl.when(jnp.logical_not(single_anchor_ok))
    def _():
        heads(False)

    @pl.when(is_ctx)
    def _():
        for h in range(H_B):
            ns_ref[0, 0, 0, h] = st_ref[h].T


def _hgrn_pass(reverse, l, u, hgrn_lb, s0, maps, new_state):
    tile = (lambda g: NT - 1 - g) if reverse else (lambda g: g)
    col = lambda c: pl.BlockSpec((TM, CB), lambda g: (tile(g), c))
    s0_spec = pl.BlockSpec((1, H_B, DK_B, DK_B),
                           lambda g: (jnp.clip((tile(g) - CTX_TILES) // SMP_TILES, 0, DEC_BATCH - 1), 0, 0, 0))
    lbp_spec = pl.BlockSpec((DEPTH * 2, D_B), lambda g: (0, 0))
    map_spec = pl.BlockSpec((TM, TM), lambda g: (0, 0))
    d = 1 if reverse else 0
    ns_spec = pl.BlockSpec((1, 1, 1, H_B, DK_B, DK_B),
                           lambda g: (jnp.minimum(tile(g), CTX_TILES - 1), l, d, 0, 0, 0))
    any_spec = pl.BlockSpec(memory_space=pl.ANY)
    f_col = C_HFB if reverse else C_HFF
    in_specs = [col(C_HQ), col(C_HI), col(f_col), lbp_spec, s0_spec, map_spec, map_spec, any_spec]
    args = (u, u, u, hgrn_lb, s0, *maps, new_state)
    o_shape = jax.ShapeDtypeStruct((T, D_B), F32)
    head_tile = pltpu.VMEM((H_B, TM, DK_B), F32)
    return pl.pallas_call(
        functools.partial(_hgrn_kernel, reverse, l),
        grid=(NT,),
        in_specs=in_specs,
        out_specs=[pl.BlockSpec((TM, D_B), lambda g: (tile(g), 0)), ns_spec],
        out_shape=[o_shape, jax.ShapeDtypeStruct(new_state.shape, F32)],
        input_output_aliases={len(args) - 1: 1},
        scratch_shapes=[pltpu.VMEM((H_B, DK_B, DK_B), F32), head_tile, head_tile, head_tile, head_tile,
                        pltpu.VMEM((H_B, 2, TM // 2, TM // 2), F32)],
        compiler_params=_cparams(("arbitrary",)),
        name="hgrn_bwd" if reverse else "hgrn_fwd",
    )(*args)


def _rope_tables():
    rows = DEC_SEQ // GRID_W
    row = jnp.broadcast_to(jnp.arange(rows)[:, None], (rows, GRID_W)).reshape(-1).astype(F32)
    colp = jnp.broadcast_to(jnp.arange(GRID_W)[None, :], (rows, GRID_W)).reshape(-1).astype(F32)
    half = DH_C // 2
    inv = ROPE_BASE ** (-jnp.arange(0, half, 2, dtype=F32) / half)
    ang = jnp.concatenate([row[:, None] * inv, colp[:, None] * inv], axis=-1)
    cos = jnp.repeat(jnp.cos(ang), 2, axis=-1)
    sin = jnp.repeat(jnp.sin(ang), 2, axis=-1) * jnp.tile(jnp.array([-1.0, 1.0], F32), half)
    cos = jnp.concatenate([jnp.ones((IN_TM, DH_C), F32), cos], axis=0)
    sin = jnp.concatenate([jnp.zeros((IN_TM, DH_C), F32), sin], axis=0)
    return jnp.tile(cos, (1, 2)), jnp.tile(sin, (1, 2))


def _attn_kernel(lam_init, n_new, has_cache, n_heads, *refs):
    if has_cache:
        qt_ref, kc_ref, vc_ref, kn_ref, vn_ref, lam_ref, sub_ref, _, o_ref = refs[:9]
    else:
        qt_ref, kn_ref, vn_ref, lam_ref, sub_ref, _, o_ref = refs[:7]
    qm_ref, s_ref, p_ref, m_ref, l_ref, a_ref, acc_ref = refs[-7:]
    tq = qt_ref.shape[1]
    ncb = tq // 128
    lq = lam_ref[...]
    lam = (jnp.exp(jnp.sum(lq[0:1] * lq[1:2], axis=-1, keepdims=True))
           - jnp.exp(jnp.sum(lq[2:3] * lq[3:4], axis=-1, keepdims=True)) + lam_init)

    def scores(k, buf):
        for c in range(2):
            s = _dot(k, qm_ref[c])
            for cb in range(ncb):
                s_ref[buf, c, cb] = s[:, cb * 128:(cb + 1) * 128]

    def softmax_update(buf):
        for c in range(2):
            for cb in range(ncb):
                cols = slice(cb * 128, (cb + 1) * 128)
                s = s_ref[buf, c, cb]
                m_old = m_ref[c, :, cols]
                m_new = jnp.maximum(m_old, jnp.max(s, axis=0, keepdims=True))
                alpha = jnp.exp2(m_old - m_new)
                p = jnp.exp2(s - m_new)
                l_ref[c, :, cols] = alpha * l_ref[c, :, cols] + jnp.sum(p, axis=0, keepdims=True)
                m_ref[c, :, cols] = m_new
                a_ref[buf, c, :, cols] = alpha
                p_ref[buf, c, cb] = p.astype(BF16)

    def accumulate(vt, buf):
        for c in range(2):
            p = jnp.concatenate([p_ref[buf, c, cb] for cb in range(ncb)], axis=1)
            acc_ref[c] = a_ref[buf, c] * acc_ref[c] + _dot(vt, p)

    for h in range(n_heads):
        hs = slice(h * 128, (h + 1) * 128)
        qt = qt_ref[hs, :]
        sub_i = lax.broadcasted_iota(jnp.int32, qt.shape, 0)
        qm_ref[0] = jnp.where(sub_i < DH_C, qt, jnp.zeros_like(qt))
        qm_ref[1] = jnp.where(sub_i >= DH_C, qt, jnp.zeros_like(qt))
        m_ref[...] = jnp.full(m_ref.shape, -1e30, F32)
        l_ref[...] = jnp.zeros_like(l_ref)
        acc_ref[...] = jnp.zeros_like(acc_ref)

        if n_new == 1:
            scores(kn_ref[:, hs], 0)
            softmax_update(0)
            accumulate(vn_ref[0, hs, :], 0)
        else:
            new_k = lambda j: kn_ref[pl.ds(pl.multiple_of(j * TM, TM), TM), hs]
            scores(new_k(0), 0)

            def body(i, carry):
                scores(new_k(2 * i + 1), 1)
                softmax_update(0)
                accumulate(vn_ref[2 * i, hs, :], 0)
                scores(new_k(2 * i + 2), 0)
                softmax_update(1)
                accumulate(vn_ref[2 * i + 1, hs, :], 1)
                return carry
            lax.fori_loop(0, n_new // 2 - 1, body, 0)
            scores(kn_ref[(n_new - 1) * TM:, hs], 1)
            softmax_update(0)
            accumulate(vn_ref[n_new - 2, hs, :], 0)
            scores(kc_ref[:, hs], 0)
            softmax_update(1)
            accumulate(vn_ref[n_new - 1, hs, :], 1)
            softmax_update(0)
            accumulate(vc_ref[0, hs, :], 0)

        for cb in range(ncb):
            cols = slice(cb * 128, (cb + 1) * 128)
            o = acc_ref[0, :, cols] / l_ref[0, :, cols] - lam * (acc_ref[1, :, cols] / l_ref[1, :, cols])
            y = o * lax.rsqrt(jnp.mean(o * o, axis=0, keepdims=True) + 1e-6)
            o_ref[cols, hs] = (y.T * sub_ref[...] * (1.0 - lam_init)).astype(BF16)


def _attn_scratch(tq):
    ncb = tq // 128
    return [pltpu.VMEM((2, 128, tq), BF16), pltpu.VMEM((2, 2, ncb, TM, 128), F32),
            pltpu.VMEM((2, 2, ncb, TM, 128), BF16),
            pltpu.VMEM((2, 1, tq), F32), pltpu.VMEM((2, 1, tq), F32), pltpu.VMEM((2, 2, 1, tq), F32),
            pltpu.VMEM((2, 128, tq), F32)]


def _attention(l, qt, kb, vt, ck, cvt, lambda_qk_l, subln_l):
    lam_init = 0.8 - 0.6 * math.exp(-0.3 * l)
    sub = subln_l.reshape(1, 2 * DH_C)
    o_ctx = pl.pallas_call(
        functools.partial(_attn_kernel, lam_init, 1, False, H_C),
        grid=(BATCH,),
        in_specs=[pl.BlockSpec((D_C, SEQ), lambda b: (0, b)),
                  pl.BlockSpec((SEQ, D_C), lambda b: (b, 0)),
                  pl.BlockSpec((1, D_C, TM), lambda b: (b, 0, 0)),
                  pl.BlockSpec((4, DH_C), lambda b: (0, 0)),
                  pl.BlockSpec((1, 2 * DH_C), lambda b: (0, 0)),
                  pl.BlockSpec(memory_space=pl.ANY)],
        out_specs=pl.BlockSpec((SEQ, D_C), lambda b: (b, 0)),
        out_shape=jax.ShapeDtypeStruct((T, D_C), BF16),
        input_output_aliases={5: 0},
        scratch_shapes=_attn_scratch(SEQ),
        compiler_params=_cparams(("arbitrary",)),
        name="attn_ctx",
    )(qt, kb, vt, lambda_qk_l, sub, jnp.zeros((T, D_C), BF16))

    nq = DEC_SEQ // TQ
    base_q = BATCH * SEQ // TQ
    base_kv = BATCH * SEQ // DEC_SEQ
    o_smp = pl.pallas_call(
        functools.partial(_attn_kernel, lam_init, SMP_TILES, True, 1),
        grid=(DEC_BATCH, H_C, nq),
        in_specs=[pl.BlockSpec((128, TQ), lambda b, h, j: (h, base_q + b * nq + j)),
                  pl.BlockSpec((PAST, 128), lambda b, h, j: (b, h)),
                  pl.BlockSpec((1, 128, PAST), lambda b, h, j: (b, h, 0)),
                  pl.BlockSpec((DEC_SEQ, 128), lambda b, h, j: (base_kv + b, h)),
                  pl.BlockSpec((SMP_TILES, 128, TM), lambda b, h, j: (base_kv + b, h, 0)),
                  pl.BlockSpec((4, DH_C), lambda b, h, j: (0, 0)),
                  pl.BlockSpec((1, 2 * DH_C), lambda b, h, j: (0, 0)),
                  pl.BlockSpec(memory_space=pl.ANY)],
        out_specs=pl.BlockSpec((TQ, 128), lambda b, h, j: (base_q + b * nq + j, h)),
        out_shape=jax.ShapeDtypeStruct((T, D_C), BF16),
        input_output_aliases={7: 0},
        scratch_shapes=_attn_scratch(TQ),
        compiler_params=_cparams(("arbitrary", "arbitrary", "arbitrary")),
        name="attn_smp",
    )(qt, ck, cvt, kb, vt, lambda_qk_l, sub, o_ctx)
    return o_smp


def _merge_ffn_kernel(nx, *refs):
    (a_ref, bf_ref, bb_ref, hg_ref, hn_ref, c_ref, ga_ref, gb_ref, gc_ref, g1_ref, wb_ref, wo_ref,
     n2_ref, sc_ref, sh_ref, g2_ref, wi_ref, wd_ref) = refs[nx:nx + 18]
    o_refs = refs[nx + 18:]
    hb = []
    for h in range(H_B):
        hs = slice(h * DK_B, (h + 1) * DK_B)
        o = bf_ref[:, hs] + bb_ref[:, hs]
        y = o * lax.rsqrt(jnp.mean(o * o, axis=-1, keepdims=True) + 1e-6) * hn_ref[...]
        hb.append((y * _silu(hg_ref[:, hs])).astype(BF16))
    ya = _dot(a_ref[...], wb_ref[0:D_A, :])
    yb = _dot(jnp.concatenate(hb, axis=1), wb_ref[D_A:D_A + D_B, :])
    yc = _dot(c_ref[...], wb_ref[D_A + D_B:, :])
    m = _sigmoid(ga_ref[...]) * ya + _sigmoid(gb_ref[...]) * yb + _sigmoid(gc_ref[...]) * yc
    x1 = _x_tile(refs[:nx]) + g1_ref[0] * _dot(m.astype(BF16), wo_ref[...])
    h = _rms_mod(x1, n2_ref[...], sc_ref[0], sh_ref[0]).astype(BF16)
    gu = _dot(h, wi_ref[...])
    ff = _silu(gu[:, :D_FF]) * gu[:, D_FF:]
    y = x1 + g2_ref[0] * _dot(ff.astype(BF16), wd_ref[...])
    if len(o_refs) == 1:
        o_refs[0][...] = y
    else:
        @pl.when(pl.program_id(0) < CTX_TILES)
        def _():
            o_refs[0][...] = y

        @pl.when(pl.program_id(0) >= CTX_TILES)
        def _():
            o_refs[1][...] = y


def _merge_ffn(ya, of, ob, hgrn_norm_l, yc, u, x, modt, w_branch_l, w_out_l, norm2_l, w_ffn_in_l, w_ffn_out_l, l,
               split_out):
    row = lambda w: pl.BlockSpec((TM, w), lambda i: (i, 0))
    gate0 = C_GT * CB // D
    gate = lambda k: pl.BlockSpec((TM, D), lambda i: (i, gate0 + k))
    resident = lambda shape: pl.BlockSpec(shape, lambda i: (0, 0), pipeline_mode=pl.Buffered(1))
    x_specs, xs = _x_specs(x)
    if split_out:
        out_specs, _ = _x_specs((None, None))
        out_shape = [jax.ShapeDtypeStruct((BATCH * SEQ, D), F32), jax.ShapeDtypeStruct((DEC_BATCH * DEC_SEQ, D), F32)]
    else:
        out_specs = row(D)
        out_shape = jax.ShapeDtypeStruct((T, D), F32)
    return pl.pallas_call(
        functools.partial(_merge_ffn_kernel, len(xs)),
        grid=(NT,),
        in_specs=x_specs + [row(D_A), row(D_B), row(D_B), pl.BlockSpec((TM, CB), lambda i: (i, C_HG)),
                            pl.BlockSpec((1, DK_B), lambda i: (0, 0)), row(D_C),
                            gate(0), gate(1), gate(2), _mod_spec(l, 2),
                            resident((D_A + D_B + D_C, D)), resident((D, D)),
                            pl.BlockSpec((1, D), lambda i: (0, 0)),
                            _mod_spec(l, 4), _mod_spec(l, 3), _mod_spec(l, 5),
                            resident((D, 2 * D_FF)), resident((D_FF, D))],
        out_specs=out_specs,
        out_shape=out_shape,
        compiler_params=_cparams(("arbitrary",)),
        name="merge_ffn",
    )(*xs, ya, of, ob, u, hgrn_norm_l.reshape(1, DK_B), yc, u, u, u, modt, w_branch_l, w_out_l,
      norm2_l.reshape(1, D), modt, modt, modt, w_ffn_in_l, w_ffn_out_l)


def _layer(l, x, modt, cache_k, cache_v, state_hgrn, tabs, p, outs):
    (norm1, norm2, w_in, conv_w, conv_b, conv_ln_g, conv_ln_b, hgrn_lb, hgrn_norm, q_norm, k_norm, lambda_qk,
     subln, w_branch, w_out, w_ffn_in, w_ffn_out) = p
    cos_t, sin_t, bd, maps_f, maps_b = tabs
    new_k, new_v, new_s = outs
    u, qt, kb, vt, new_k, new_v = _in_proj(x, modt, norm1[l], w_in[l].astype(BF16), l, q_norm[l], k_norm[l],
                                           cos_t, sin_t, bd, new_k, new_v)
    ya = _conv_branch(u, conv_w[l], conv_b[l], conv_ln_g[l], conv_ln_b[l])
    lbp = hgrn_lb.reshape(DEPTH * 2, D_B)
    ob, new_s = _hgrn_pass(True, l, u, lbp, state_hgrn[:, l, 1], maps_b, new_s)
    of, new_s = _hgrn_pass(False, l, u, lbp, state_hgrn[:, l, 0], maps_f, new_s)
    ck = cache_k[:, l].reshape(DEC_BATCH * PAST, D_C).astype(BF16)
    cvt = cache_v[:, l].reshape(DEC_BATCH, PAST, D_C).astype(BF16).transpose(0, 2, 1)
    yc = _attention(l, qt, kb, vt, ck, cvt, lambda_qk[l], subln[l])
    x = _merge_ffn(ya, of, ob, hgrn_norm[l], yc, u, x, modt, w_branch[l].astype(BF16), w_out[l].astype(BF16), norm2[l],
                   w_ffn_in[l].astype(BF16), w_ffn_out[l].astype(BF16), l, split_out=l == DEPTH - 1)
    return x, (new_k, new_v, new_s)


def kernel(x_prompt, x_sample, cache_k, cache_v, state_hgrn, c, c_ctx, w_mod, b_mod, norm1, norm2, w_in, conv_w,
           conv_b, conv_ln_g, conv_ln_b, hgrn_lb, hgrn_norm, q_norm, k_norm, lambda_qk, subln, w_branch, w_out,
           w_ffn_in, w_ffn_out):
    x = (x_prompt.reshape(BATCH * SEQ, D), x_sample.reshape(DEC_BATCH * DEC_SEQ, D))
    cv8 = jnp.concatenate([c_ctx[None, :], c, jnp.zeros((8 - 1 - DEC_BATCH, D), F32)], axis=0)
    mod = _modulation(cv8, w_mod, b_mod)
    modt = mod.reshape(DEPTH, 8, 6, D).transpose(0, 2, 1, 3).reshape(DEPTH * 6 * 8, 1, D)
    cos_t, sin_t = _rope_tables()
    seg = np.arange(D_C) // DH_C
    bd = jnp.asarray(seg[:, None] == seg[None, :], BF16)
    maps = lambda rev: (jnp.asarray(_level_map(rev)), jnp.asarray(_low_map(rev)))
    tabs = (cos_t, sin_t, bd, maps(False), maps(True))
    p = (norm1, norm2, w_in, conv_w, conv_b, conv_ln_g, conv_ln_b, hgrn_lb, hgrn_norm, q_norm, k_norm, lambda_qk,
         subln, w_branch, w_out, w_ffn_in, w_ffn_out)
    outs = (jnp.zeros((BATCH, DEPTH, SEQ, D_C), F32), jnp.zeros((BATCH, DEPTH, SEQ, D_C), F32),
            jnp.zeros((BATCH, DEPTH, 2, H_B, DK_B, DK_B), F32))
    for l in range(DEPTH):
        x, outs = _layer(l, x, modt, cache_k, cache_v, state_hgrn, tabs, p, outs)
    new_k, new_v, new_s = outs
    y_prompt = x[0].reshape(BATCH, SEQ, D)
    y_sample = x[1].reshape(DEC_BATCH, DEC_SEQ, D)
    return (y_prompt, y_sample, new_k.reshape(BATCH, DEPTH, SEQ, H_C, 2, DH_C),
            new_v.reshape(BATCH, DEPTH, SEQ, H_C, 2 * DH_C), new_s)
```

```python
import functools
import math

import numpy as np
import jax
import jax.numpy as jnp
from jax import lax
from jax.experimental import pallas as pl
from jax.experimental.pallas import tpu as pltpu

F32 = jnp.float32
BF16 = jnp.bfloat16

D = 1024
BATCH, SEQ = 32, 256
DEPTH = 2
DEC_BATCH, DEC_SEQ = 2, 4096
PAST = 256
GRID_W = 64
D_A, CONV_K = 512, 31
H_B, DK_B = 4, 128
D_B = 512
H_C, DH_C = 4, 64
D_C = 512
ROPE_BASE = 10000.0
D_FF = 2816
IN_COLS = 8192

TM = 256
CTX_TILES = BATCH * SEQ // TM
SMP_TILES = DEC_SEQ // TM
T = BATCH * SEQ + DEC_BATCH * DEC_SEQ
NT = T // TM
CB = 512
HALO = 16
TQ = 512
VMEM_LIMIT = 56 * 1024 * 1024

C_A, C_AG, C_HQ, C_HI, C_HFF, C_HFB, C_HG, C_AQ, C_AK, C_AV, C_GT = 0, 1, 2, 3, 4, 5, 6, 7, 8, 9, 10


def _cparams(sem):
    return pltpu.CompilerParams(dimension_semantics=sem, vmem_limit_bytes=VMEM_LIMIT)


def _mod_row(i):
    return jnp.where(i < CTX_TILES, 0, 1 + (i - CTX_TILES) // SMP_TILES)


def _sigmoid(x):
    return 0.5 * jnp.tanh(0.5 * x) + 0.5


def _sigmoid_small_accurate(x):
    return jnp.exp(-jnp.log(1.0 + jnp.exp(-x)))


def _silu(x):
    return x * _sigmoid(x)


def _dot(a, b):
    return jnp.dot(a, b, preferred_element_type=F32)


def _dot_nt(a, b):
    return lax.dot_general(a, b, (((1,), (1,)), ((), ())), preferred_element_type=F32)


def _dot_tn(a, b):
    return lax.dot_general(a, b, (((0,), (0,)), ((), ())), preferred_element_type=F32)


def _split_bf16(x):
    hi = x.astype(BF16)
    lo = (x - hi.astype(F32)).astype(BF16)
    return hi, lo


MOD_TN = 1536


def _mod_kernel(cv_ref, w_ref, b_ref, o_ref):
    a = _silu(cv_ref[...])
    a_hi, a_lo = _split_bf16(a)
    w_hi, w_lo = _split_bf16(w_ref[0])
    acc = _dot(a_hi, w_hi) + _dot(a_lo, w_hi) + _dot(a_hi, w_lo)
    o_ref[0] = acc + b_ref[0]


def _modulation(cv8, w_mod, b_mod):
    return pl.pallas_call(
        _mod_kernel,
        grid=(DEPTH, 6 * D // MOD_TN),
        in_specs=[
            pl.BlockSpec((8, D), lambda l, j: (0, 0)),
            pl.BlockSpec((1, D, MOD_TN), lambda l, j: (l, 0, j)),
            pl.BlockSpec((1, 1, MOD_TN), lambda l, j: (l, 0, j)),
        ],
        out_specs=pl.BlockSpec((1, 8, MOD_TN), lambda l, j: (l, 0, j)),
        out_shape=jax.ShapeDtypeStruct((DEPTH, 8, 6 * D), F32),
        compiler_params=_cparams(("arbitrary", "arbitrary")),
        name="adaln_mod",
    )(cv8, w_mod, b_mod.reshape(DEPTH, 1, 6 * D))


def _mod_spec(l, which):
    return pl.BlockSpec((1, 1, D), lambda i: ((l * 6 + which) * 8 + _mod_row(i), 0, 0))


def _rms_mod(x, g, sc, sh):
    y = x * lax.rsqrt(jnp.mean(x * x, axis=-1, keepdims=True) + 1e-6) * g
    return y * (1.0 + sc) + sh


def _x_specs(x):
    if isinstance(x, tuple):
        return [pl.BlockSpec((TM, D), lambda i: (jnp.minimum(i, CTX_TILES - 1), 0)),
                pl.BlockSpec((TM, D), lambda i: (jnp.maximum(i - CTX_TILES, 0), 0))], list(x)
    return [pl.BlockSpec((TM, D), lambda i: (i, 0))], [x]


def _x_tile(x_refs):
    if len(x_refs) == 2:
        return jnp.where(pl.program_id(0) < CTX_TILES, x_refs[0][...], x_refs[1][...])
    return x_refs[0][...]


IN_TM = 512
IN_SPLIT = 2


IN_HALF = IN_COLS // IN_SPLIT
Q_OFF = C_AQ * CB
K_OFF = C_AK * CB - IN_HALF
assert IN_SPLIT == 2 and Q_OFF + CB == IN_HALF and K_OFF == 0 and C_AV == C_AK + 1


def _in_kernel(nx, *refs):
    (g_ref, sc_ref, sh_ref, w_ref, qn_ref, kn_ref, cos_ref, sin_ref, bd_ref, _k, _v,
     o_ref, qt_ref, kb_ref, vt_ref, nk_ref, nv_ref) = refs[nx:]
    j = pl.program_id(0)
    i = pl.program_id(1)
    n_ctx = BATCH * SEQ // IN_TM
    if nx == 2:
        x = jnp.where(i < n_ctx, refs[0][...], refs[1][...])
    else:
        x = refs[0][...]
    h = _rms_mod(x, g_ref[...], sc_ref[0], sh_ref[0]).astype(BF16)
    bd = bd_ref[...]
    lane = lax.broadcasted_iota(jnp.int32, (IN_TM, 128), 1)
    even = (lane & 1) == 0
    cos = cos_ref[...]
    sin = sin_ref[...]

    def seg_norm(xq, gain):
        ss = _dot((xq * xq).astype(BF16), bd)
        return xq * lax.rsqrt(ss * (1.0 / DH_C) + 1e-6) * gain

    def rope(xq):
        parts = []
        for s in range(D_C // 128):
            xs = xq[:, s * 128:(s + 1) * 128]
            sw = jnp.where(even, pltpu.roll(xs, 127, 1), pltpu.roll(xs, 1, 1))
            parts.append(xs * cos + sw * sin)
        return jnp.concatenate(parts, axis=1)

    @pl.when(j == 0)
    def _():
        aq = _dot(h, w_ref[:, Q_OFF:])
        o_ref[:, Q_OFF:] = aq
        q = seg_norm(aq, qn_ref[...])
        qt_ref[...] = (rope(q) * (DH_C ** -0.5 * math.log2(math.e))).T.astype(BF16)
        o_ref[:, :Q_OFF] = _dot(h, w_ref[:, :Q_OFF])

    @pl.when(j == 1)
    def _():
        akv = _dot(h, w_ref[:, :2 * CB])
        o_ref[:, :2 * CB] = akv
        av = akv[:, CB:]
        k = seg_norm(akv[:, :CB], kn_ref[...])
        kb_ref[...] = rope(k).astype(BF16)
        for t in range(IN_TM // TM):
            vt_ref[t] = av[t * TM:(t + 1) * TM, :].T.astype(BF16)
        o_ref[:, 2 * CB:] = _dot(h, w_ref[:, 2 * CB:])

        @pl.when(i < n_ctx)
        def _():
            for t in range(IN_TM // SEQ):
                nk_ref[t, 0] = k[t * SEQ:(t + 1) * SEQ, :]
                nv_ref[t, 0] = av[t * SEQ:(t + 1) * SEQ, :]


def _in_proj(x, modt, norm1_l, w_in_l, l, q_norm_l, k_norm_l, cos_t, sin_t, bd, new_k, new_v):
    n_ctx = BATCH * SEQ // IN_TM
    per_seq = DEC_SEQ // IN_TM
    n_i = T // IN_TM
    mod_row = lambda i: jnp.where(i < n_ctx, 0, 1 + (i - n_ctx) // per_seq)
    mod = lambda which: pl.BlockSpec((1, 1, D), lambda j, i: ((l * 6 + which) * 8 + mod_row(i), 0, 0))
    if isinstance(x, tuple):
        x_specs = [pl.BlockSpec((IN_TM, D), lambda j, i: (jnp.minimum(i, n_ctx - 1), 0)),
                   pl.BlockSpec((IN_TM, D), lambda j, i: (jnp.maximum(i - n_ctx, 0), 0))]
        xs = list(x)
    else:
        x_specs, xs = [pl.BlockSpec((IN_TM, D), lambda j, i: (i, 0))], [x]
    vec = pl.BlockSpec((1, D_C), lambda j, i: (0, 0))
    tab = pl.BlockSpec((IN_TM, 128), lambda j, i: (jnp.where(i < n_ctx, 0, 1 + (i - n_ctx) % per_seq), 0))
    any_spec = pl.BlockSpec(memory_space=pl.ANY)
    q_i = lambda j, i: jnp.where(j == 0, i, n_i - 1)
    kv_i = lambda j, i: jnp.where(j == 1, i, 0)
    seqs = IN_TM // SEQ
    cache = pl.BlockSpec((seqs, 1, SEQ, D_C), lambda j, i: (jnp.minimum(kv_i(j, i), n_ctx - 1), l, 0, 0))
    n_in = len(xs) + 9
    return pl.pallas_call(
        functools.partial(_in_kernel, len(xs)),
        grid=(IN_SPLIT, n_i),
        in_specs=x_specs + [
            pl.BlockSpec((1, D), lambda j, i: (0, 0)),
            mod(1),
            mod(0),
            pl.BlockSpec((D, IN_HALF), lambda j, i: (0, j)),
            vec, vec, tab, tab,
            pl.BlockSpec((D_C, D_C), lambda j, i: (0, 0)),
            any_spec, any_spec,
        ],
        out_specs=[pl.BlockSpec((IN_TM, IN_HALF), lambda j, i: (i, j)),
                   pl.BlockSpec((D_C, IN_TM), lambda j, i: (0, q_i(j, i))),
                   pl.BlockSpec((IN_TM, D_C), lambda j, i: (kv_i(j, i), 0)),
                   pl.BlockSpec((IN_TM // TM, D_C, TM), lambda j, i: (kv_i(j, i), 0, 0)),
                   cache, cache],
        out_shape=[jax.ShapeDtypeStruct((T, IN_COLS), F32), jax.ShapeDtypeStruct((D_C, T), BF16),
                   jax.ShapeDtypeStruct((T, D_C), BF16), jax.ShapeDtypeStruct((NT, D_C, TM), BF16),
                   jax.ShapeDtypeStruct(new_k.shape, F32), jax.ShapeDtypeStruct(new_v.shape, F32)],
        input_output_aliases={n_in: 4, n_in + 1: 5},
        compiler_params=_cparams(("arbitrary", "arbitrary")),
        name="in_proj",
    )(*xs, norm1_l.reshape(1, D), modt, modt, w_in_l,
      jnp.tile(q_norm_l, D_C // DH_C).reshape(1, D_C), jnp.tile(k_norm_l, D_C // DH_C).reshape(1, D_C),
      cos_t, sin_t, bd, new_k, new_v)


CONV_RB = 32


def _conv_kernel(ac_ref, gc_ref, ap_ref, gp_ref, an_ref, gn_ref, w_ref, b_ref, lg_ref, lb_ref, o_ref, sh_ref,
                 cv_ref):
    i = pl.program_id(0)
    in_smp = i >= CTX_TILES
    pos = i % SMP_TILES
    has_prev = jnp.logical_and(in_smp, pos != 0)
    has_next = jnp.logical_and(in_smp, pos != SMP_TILES - 1)
    n = TM + 2 * HALO
    sh_ref[0, HALO:HALO + TM, :] = ac_ref[...] * _sigmoid(gc_ref[...])
    sh_ref[0, 0:HALO, :] = jnp.where(has_prev, ap_ref[...] * _sigmoid(gp_ref[...]), 0.0)
    sh_ref[0, HALO + TM:, :] = jnp.where(has_next, an_ref[...] * _sigmoid(gn_ref[...]), 0.0)
    for r in range(1, 8):
        sh_ref[r, 0:n - 8, :] = sh_ref[0, r:r + n - 8, :]
    off = HALO - CONV_K // 2

    def body(rb, carry):
        base = pl.multiple_of(rb * CONV_RB, CONV_RB)
        acc = jnp.zeros((CONV_RB, D_A), F32)
        for k in range(CONV_K):
            r = (off + k) % 8
            acc = acc + sh_ref[r, pl.ds(base + (off + k - r), CONV_RB), :] * w_ref[k:k + 1, :]
        cv_ref[pl.ds(base, CONV_RB), :] = acc
        return carry
    lax.fori_loop(0, TM // CONV_RB, body, 0)
    acc = cv_ref[...] + b_ref[...]
    mu = jnp.mean(acc, axis=-1, keepdims=True)
    xc = acc - mu
    y = xc * lax.rsqrt(jnp.mean(xc * xc, axis=-1, keepdims=True) + 1e-5)
    y = y * lg_ref[...] + lb_ref[...]
    o_ref[...] = _silu(y).astype(BF16)


def _conv_branch(u, conv_w_l, conv_b_l, ln_g_l, ln_b_l):
    r = TM // HALO
    last = T // HALO - 1
    cur = lambda c: pl.BlockSpec((TM, CB), lambda i: (i, c))
    prev = lambda c: pl.BlockSpec((HALO, CB), lambda i: (jnp.maximum(i * r - 1, 0), c))
    nxt = lambda c: pl.BlockSpec((HALO, CB), lambda i: (jnp.minimum((i + 1) * r, last), c))
    vec = pl.BlockSpec((1, D_A), lambda i: (0, 0))
    return pl.pallas_call(
        _conv_kernel,
        grid=(NT,),
        in_specs=[cur(C_A), cur(C_AG), prev(C_A), prev(C_AG), nxt(C_A), nxt(C_AG),
                  pl.BlockSpec((CONV_K, D_A), lambda i: (0, 0)), vec, vec, vec],
        out_specs=pl.BlockSpec((TM, D_A), lambda i: (i, 0)),
        out_shape=jax.ShapeDtypeStruct((T, D_A), BF16),
        scratch_shapes=[pltpu.VMEM((8, TM + 2 * HALO, D_A), F32), pltpu.VMEM((TM, D_A), F32)],
        compiler_params=_cparams(("arbitrary",)),
        name="conv_branch",
    )(u, u, u, u, u, u, conv_w_l, conv_b_l.reshape(1, D_A), ln_g_l.reshape(1, D_A), ln_b_l.reshape(1, D_A))


N_LEVELS = 8


def _level_map(reverse):
    t = np.arange(TM)[:, None]
    s = np.arange(TM)[None, :]
    x = t ^ s
    lv = np.zeros((TM, TM), np.int32)
    for bit in range(N_LEVELS):
        lv[(x >> bit) == 1] = bit + 1
    ok = (t < s) if reverse else (t > s)
    return np.where(ok, lv, 0).astype(np.int32)


def _shift_rows(x, k):
    return pltpu.roll(x, k % TM, 0)


def _cumsum_rows(x, row, reverse):
    sh = 1
    while sh < TM:
        if reverse:
            x = x + jnp.where(row < TM - sh, _shift_rows(x, -sh), 0.0)
        else:
            x = x + jnp.where(row >= sh, _shift_rows(x, sh), 0.0)
        sh *= 2
    return x


def _anchors(cum, row, reverse):
    out = []
    z = cum
    for bit in range(N_LEVELS):
        h = 1 << bit
        if 2 * h >= 16:
            c3 = cum.reshape(TM // (2 * h), 2 * h, DK_B)
            p = h if reverse else h - 1
            out.append(jnp.broadcast_to(c3[:, p:p + 1, :], c3.shape).reshape(TM, DK_B))
            continue
        hi = (row & h) != 0
        if reverse:
            out.append(jnp.where(hi, z, _shift_rows(z, -h)))
            z = jnp.where(hi, _shift_rows(z, h), z)
        else:
            out.append(jnp.where(hi, _shift_rows(z, h), z))
            z = jnp.where(hi, z, _shift_rows(z, -h))
    return out


LOW_BITS = 5
SAFE_RANGE = 60.0


def _low_map(reverse):
    t = np.arange(TM)[:, None]
    s = np.arange(TM)[None, :]
    same = (t >> LOW_BITS) == (s >> LOW_BITS)
    ok = (t <= s) if reverse else (t >= s)
    return (same & ok).astype(np.int32)


def _block_anchor(cum, reverse):
    n = 1 << LOW_BITS
    c3 = cum.reshape(TM // n, n, DK_B)
    p = n - 1 if reverse else 0
    return jnp.broadcast_to(c3[:, p:p + 1, :], c3.shape).reshape(TM, DK_B)


def _hgrn_kernel(reverse, l, *refs):
    (q_ref, v_ref, f_ref, lbp_ref, s0_ref, lv_ref, lo_ref, _, o_ref, ns_ref) = refs[:10]
    st_ref, qs_ref, ks_ref, cs_ref, os_ref, as_ref = refs[-6:]
    g = pl.program_id(0)
    i = NT - 1 - g if reverse else g
    is_ctx = i < CTX_TILES
    pos = i % SMP_TILES
    seq_start = jnp.logical_and(i >= CTX_TILES, pos == (SMP_TILES - 1 if reverse else 0))

    @pl.when(is_ctx)
    def _():
        st_ref[...] = jnp.zeros_like(st_ref)

    @pl.when(seq_start)
    def _():
        for h in range(H_B):
            st_ref[h] = s0_ref[0, h].T

    d = 1 if reverse else 0
    rows = [lbp_ref[k * 2 + d:k * 2 + d + 1, :] for k in range(DEPTH)]
    mx = functools.reduce(jnp.maximum, rows)
    ex = [jnp.exp(r - mx) for r in rows]
    den = functools.reduce(jnp.add, ex)
    lb_all = jnp.zeros_like(mx)
    for k in range(1, l + 1):
        lb_all = lb_all + ex[k] / den

    row = lax.broadcasted_iota(jnp.int32, (TM, DK_B), 0)
    spread = jnp.zeros((1, DK_B), F32)
    for h in range(H_B):
        sl = slice(h * DK_B, (h + 1) * DK_B)
        lb = lb_all[:, sl]
        fg = lb + (1.0 - lb) * _sigmoid_small_accurate(f_ref[:, sl])
        cum = _cumsum_rows(jnp.log(fg), row, reverse)
        ks_ref[h] = 1.0 - fg
        qs_ref[h] = _silu(q_ref[:, sl])
        cs_ref[h] = cum
        spread = jnp.maximum(spread, jnp.max(_block_anchor(cum, reverse) - cum, axis=0, keepdims=True))
    single_anchor_ok = jnp.max(spread) < SAFE_RANGE

    half = TM // 2
    halves = (slice(0, half), slice(half, TM))
    late, early = (halves[0], halves[1]) if reverse else (halves[1], halves[0])

    lv = lv_ref[0:half, 0:half]

    def operands(h, fq, fk):
        return (qs_ref[h] * fq).astype(BF16), (ks_ref[h] * fk).astype(BF16)

    for h in range(H_B):
        sl = slice(h * DK_B, (h + 1) * DK_B)
        q, kg, cum = qs_ref[h], ks_ref[h], cs_ref[h]
        vb = v_ref[:, sl].astype(BF16)
        tot = cum[0:1, :] if reverse else cum[TM - 1:TM, :]
        st = st_ref[h]
        o = _dot_nt((q * jnp.exp(cum)).astype(BF16), st.astype(BF16))
        anchors = _anchors(cum, row, reverse)
        fac = jnp.exp(-jnp.abs(cum - anchors[N_LEVELS - 1]))
        qf, kf = operands(h, fac, fac)
        o_top = _dot(_dot_nt(qf[late], kf[early]).astype(BF16), vb[early])
        zero = jnp.zeros_like(o_top)
        os_ref[h] = o + jnp.concatenate([o_top, zero] if reverse else [zero, o_top], axis=0)
        a = [jnp.zeros((half, half), F32) for _ in halves]
        for bit in range(LOW_BITS, N_LEVELS - 1):
            fac = jnp.exp(-jnp.abs(cum - anchors[bit]))
            qf, kf = operands(h, fac, fac)
            a = [jnp.where(lv == bit + 1, _dot_nt(qf[r], kf[r]), a[b]) for b, r in enumerate(halves)]
        for b in range(2):
            as_ref[h, b] = a[b]
        kbar = (kg * jnp.exp(tot - cum)).astype(BF16)
        st_ref[h] = jnp.exp(tot) * st + _dot_tn(vb, kbar)

    def heads(single_anchor):
        for h in range(H_B):
            sl = slice(h * DK_B, (h + 1) * DK_B)
            cum = cs_ref[h]
            v = v_ref[:, sl]
            vb = v.astype(BF16)
            o = os_ref[h]
            a = [as_ref[h, b] for b in range(2)]
            if single_anchor:
                e = _block_anchor(cum, reverse)
                qf, kf = operands(h, jnp.exp(cum - e), jnp.exp(e - cum))
                lo = lo_ref[0:half, 0:half] != 0
                a = [jnp.where(lo, _dot_nt(qf[r], kf[r]), a[b]) for b, r in enumerate(halves)]
            else:
                o = o + jnp.sum(qs_ref[h] * ks_ref[h], axis=-1, keepdims=True) * v
                anchors = _anchors(cum, row, reverse)
                for bit in range(LOW_BITS):
                    fac = jnp.exp(-jnp.abs(cum - anchors[bit]))
                    qf, kf = operands(h, fac, fac)
                    a = [jnp.where(lv == bit + 1, _dot_nt(qf[r], kf[r]), a[b]) for b, r in enumerate(halves)]
            o_ref[:, sl] = o + jnp.concatenate([_dot(a[b].astype(BF16), vb[r]) for b, r in enumerate(halves)],
                                               axis=0)

    @pl.when(single_anchor_ok)
    def _():
        heads(True)

    @pl.when(jnp.logical_not(single_anchor_ok))
    def _():
        heads(False)

    @pl.when(is_ctx)
    def _():
        for h in range(H_B):
            ns_ref[0, 0, 0, h] = st_ref[h].T


def _hgrn_pass(reverse, l, u, hgrn_lb, s0, maps, new_state):
    tile = (lambda g: NT - 1 - g) if reverse else (lambda g: g)
    col = lambda c: pl.BlockSpec((TM, CB), lambda g: (tile(g), c))
    s0_spec = pl.BlockSpec((1, H_B, DK_B, DK_B),
                           lambda g: (jnp.clip((tile(g) - CTX_TILES) // SMP_TILES, 0, DEC_BATCH - 1), 0, 0, 0))
    lbp_spec = pl.BlockSpec((DEPTH * 2, D_B), lambda g: (0, 0))
    map_spec = pl.BlockSpec((TM, TM), lambda g: (0, 0))
    d = 1 if reverse else 0
    ns_spec = pl.BlockSpec((1, 1, 1, H_B, DK_B, DK_B),
                           lambda g: (jnp.minimum(tile(g), CTX_TILES - 1), l, d, 0, 0, 0))
    any_spec = pl.BlockSpec(memory_space=pl.ANY)
    f_col = C_HFB if reverse else C_HFF
    in_specs = [col(C_HQ), col(C_HI), col(f_col), lbp_spec, s0_spec, map_spec, map_spec, any_spec]
    args = (u, u, u, hgrn_lb, s0, *maps, new_state)
    o_shape = jax.ShapeDtypeStruct((T, D_B), F32)
    head_tile = pltpu.VMEM((H_B, TM, DK_B), F32)
    return pl.pallas_call(
        functools.partial(_hgrn_kernel, reverse, l),
        grid=(NT,),
        in_specs=in_specs,
        out_specs=[pl.BlockSpec((TM, D_B), lambda g: (tile(g), 0)), ns_spec],
        out_shape=[o_shape, jax.ShapeDtypeStruct(new_state.shape, F32)],
        input_output_aliases={len(args) - 1: 1},
        scratch_shapes=[pltpu.VMEM((H_B, DK_B, DK_B), F32), head_tile, head_tile, head_tile, head_tile,
                        pltpu.VMEM((H_B, 2, TM // 2, TM // 2), F32)],
        compiler_params=_cparams(("arbitrary",)),
        name="hgrn_bwd" if reverse else "hgrn_fwd",
    )(*args)


def _rope_tables():
    rows = DEC_SEQ // GRID_W
    row = jnp.broadcast_to(jnp.arange(rows)[:, None], (rows, GRID_W)).reshape(-1).astype(F32)
    colp = jnp.broadcast_to(jnp.arange(GRID_W)[None, :], (rows, GRID_W)).reshape(-1).astype(F32)
    half = DH_C // 2
    inv = ROPE_BASE ** (-jnp.arange(0, half, 2, dtype=F32) / half)
    ang = jnp.concatenate([row[:, None] * inv, colp[:, None] * inv], axis=-1)
    cos = jnp.repeat(jnp.cos(ang), 2, axis=-1)
    sin = jnp.repeat(jnp.sin(ang), 2, axis=-1) * jnp.tile(jnp.array([-1.0, 1.0], F32), half)
    cos = jnp.concatenate([jnp.ones((IN_TM, DH_C), F32), cos], axis=0)
    sin = jnp.concatenate([jnp.zeros((IN_TM, DH_C), F32), sin], axis=0)
    return jnp.tile(cos, (1, 2)), jnp.tile(sin, (1, 2))


def _attn_kernel(lam_init, n_new, has_cache, n_heads, *refs):
    if has_cache:
        qt_ref, kc_ref, vc_ref, kn_ref, vn_ref, lam_ref, sub_ref, _, o_ref = refs[:9]
    else:
        qt_ref, kn_ref, vn_ref, lam_ref, sub_ref, _, o_ref = refs[:7]
    qm_ref, s_ref, p_ref, m_ref, l_ref, a_ref, acc_ref = refs[-7:]
    tq = qt_ref.shape[1]
    ncb = tq // 128
    lq = lam_ref[...]
    lam = (jnp.exp(jnp.sum(lq[0:1] * lq[1:2], axis=-1, keepdims=True))
           - jnp.exp(jnp.sum(lq[2:3] * lq[3:4], axis=-1, keepdims=True)) + lam_init)

    def scores(k, buf):
        for c in range(2):
            s = _dot(k, qm_ref[c])
            for cb in range(ncb):
                s_ref[buf, c, cb] = s[:, cb * 128:(cb + 1) * 128]

    def softmax_update(buf):
        for c in range(2):
            for cb in range(ncb):
                cols = slice(cb * 128, (cb + 1) * 128)
                s = s_ref[buf, c, cb]
                m_old = m_ref[c, :, cols]
                m_new = jnp.maximum(m_old, jnp.max(s, axis=0, keepdims=True))
                alpha = jnp.exp2(m_old - m_new)
                p = jnp.exp2(s - m_new)
                l_ref[c, :, cols] = alpha * l_ref[c, :, cols] + jnp.sum(p, axis=0, keepdims=True)
                m_ref[c, :, cols] = m_new
                a_ref[buf, c, :, cols] = alpha
                p_ref[buf, c, cb] = p.astype(BF16)

    def accumulate(vt, buf):
        for c in range(2):
            p = jnp.concatenate([p_ref[buf, c, cb] for cb in range(ncb)], axis=1)
            acc_ref[c] = a_ref[buf, c] * acc_ref[c] + _dot(vt, p)

    for h in range(n_heads):
        hs = slice(h * 128, (h + 1) * 128)
        qt = qt_ref[hs, :]
        sub_i = lax.broadcasted_iota(jnp.int32, qt.shape, 0)
        qm_ref[0] = jnp.where(sub_i < DH_C, qt, jnp.zeros_like(qt))
        qm_ref[1] = jnp.where(sub_i >= DH_C, qt, jnp.zeros_like(qt))
        m_ref[...] = jnp.full(m_ref.shape, -1e30, F32)
        l_ref[...] = jnp.zeros_like(l_ref)
        acc_ref[...] = jnp.zeros_like(acc_ref)

        if n_new == 1:
            scores(kn_ref[:, hs], 0)
            softmax_update(0)
            accumulate(vn_ref[0, hs, :], 0)
        else:
            new_k = lambda j: kn_ref[pl.ds(pl.multiple_of(j * TM, TM), TM), hs]
            scores(new_k(0), 0)

            def body(i, carry):
                scores(new_k(2 * i + 1), 1)
                softmax_update(0)
                accumulate(vn_ref[2 * i, hs, :], 0)
                scores(new_k(2 * i + 2), 0)
                softmax_update(1)
                accumulate(vn_ref[2 * i + 1, hs, :], 1)
                return carry
            lax.fori_loop(0, n_new // 2 - 1, body, 0)
            scores(kn_ref[(n_new - 1) * TM:, hs], 1)
            softmax_update(0)
            accumulate(vn_ref[n_new - 2, hs, :], 0)
            scores(kc_ref[:, hs], 0)
            softmax_update(1)
            accumulate(vn_ref[n_new - 1, hs, :], 1)
            softmax_update(0)
            accumulate(vc_ref[0, hs, :], 0)

        for cb in range(ncb):
            cols = slice(cb * 128, (cb + 1) * 128)
            o = acc_ref[0, :, cols] / l_ref[0, :, cols] - lam * (acc_ref[1, :, cols] / l_ref[1, :, cols])
            y = o * lax.rsqrt(jnp.mean(o * o, axis=0, keepdims=True) + 1e-6)
            o_ref[cols, hs] = (y.T * sub_ref[...] * (1.0 - lam_init)).astype(BF16)


def _attn_scratch(tq):
    ncb = tq // 128
    return [pltpu.VMEM((2, 128, tq), BF16), pltpu.VMEM((2, 2, ncb, TM, 128), F32),
            pltpu.VMEM((2, 2, ncb, TM, 128), BF16),
            pltpu.VMEM((2, 1, tq), F32), pltpu.VMEM((2, 1, tq), F32), pltpu.VMEM((2, 2, 1, tq), F32),
            pltpu.VMEM((2, 128, tq), F32)]


def _attention(l, qt, kb, vt, ck, cvt, lambda_qk_l, subln_l):
    lam_init = 0.8 - 0.6 * math.exp(-0.3 * l)
    sub = subln_l.reshape(1, 2 * DH_C)
    o_ctx = pl.pallas_call(
        functools.partial(_attn_kernel, lam_init, 1, False, H_C),
        grid=(BATCH,),
        in_specs=[pl.BlockSpec((D_C, SEQ), lambda b: (0, b)),
                  pl.BlockSpec((SEQ, D_C), lambda b: (b, 0)),
                  pl.BlockSpec((1, D_C, TM), lambda b: (b, 0, 0)),
                  pl.BlockSpec((4, DH_C), lambda b: (0, 0)),
                  pl.BlockSpec((1, 2 * DH_C), lambda b: (0, 0)),
                  pl.BlockSpec(memory_space=pl.ANY)],
        out_specs=pl.BlockSpec((SEQ, D_C), lambda b: (b, 0)),
        out_shape=jax.ShapeDtypeStruct((T, D_C), BF16),
        input_output_aliases={5: 0},
        scratch_shapes=_attn_scratch(SEQ),
        compiler_params=_cparams(("arbitrary",)),
        name="attn_ctx",
    )(qt, kb, vt, lambda_qk_l, sub, jnp.zeros((T, D_C), BF16))

    nq = DEC_SEQ // TQ
    base_q = BATCH * SEQ // TQ
    base_kv = BATCH * SEQ // DEC_SEQ
    o_smp = pl.pallas_call(
        functools.partial(_attn_kernel, lam_init, SMP_TILES, True, 1),
        grid=(DEC_BATCH, H_C, nq),
        in_specs=[pl.BlockSpec((128, TQ), lambda b, h, j: (h, base_q + b * nq + j)),
                  pl.BlockSpec((PAST, 128), lambda b, h, j: (b, h)),
                  pl.BlockSpec((1, 128, PAST), lambda b, h, j: (b, h, 0)),
                  pl.BlockSpec((DEC_SEQ, 128), lambda b, h, j: (base_kv + b, h)),
                  pl.BlockSpec((SMP_TILES, 128, TM), lambda b, h, j: (base_kv + b, h, 0)),
                  pl.BlockSpec((4, DH_C), lambda b, h, j: (0, 0)),
                  pl.BlockSpec((1, 2 * DH_C), lambda b, h, j: (0, 0)),
                  pl.BlockSpec(memory_space=pl.ANY)],
        out_specs=pl.BlockSpec((TQ, 128), lambda b, h, j: (base_q + b * nq + j, h)),
        out_shape=jax.ShapeDtypeStruct((T, D_C), BF16),
        input_output_aliases={7: 0},
        scratch_shapes=_attn_scratch(TQ),
        compiler_params=_cparams(("arbitrary", "arbitrary", "arbitrary")),
        name="attn_smp",
    )(qt, ck, cvt, kb, vt, lambda_qk_l, sub, o_ctx)
    return o_smp


def _merge_ffn_kernel(nx, *refs):
    (a_ref, bf_ref, bb_ref, hg_ref, hn_ref, c_ref, ga_ref, gb_ref, gc_ref, g1_ref, wb_ref, wo_ref,
     n2_ref, sc_ref, sh_ref, g2_ref, wi_ref, wd_ref) = refs[nx:nx + 18]
    o_refs = refs[nx + 18:]
    hb = []
    for h in range(H_B):
        hs = slice(h * DK_B, (h + 1) * DK_B)
        o = bf_ref[:, hs] + bb_ref[:, hs]
        y = o * lax.rsqrt(jnp.mean(o * o, axis=-1, keepdims=True) + 1e-6) * hn_ref[...]
        hb.append((y * _silu(hg_ref[:, hs])).astype(BF16))
    ya = _dot(a_ref[...], wb_ref[0:D_A, :])
    yb = _dot(jnp.concatenate(hb, axis=1), wb_ref[D_A:D_A + D_B, :])
    yc = _dot(c_ref[...], wb_ref[D_A + D_B:, :])
    m = _sigmoid(ga_ref[...]) * ya + _sigmoid(gb_ref[...]) * yb + _sigmoid(gc_ref[...]) * yc
    x1 = _x_tile(refs[:nx]) + g1_ref[0] * _dot(m.astype(BF16), wo_ref[...])
    h = _rms_mod(x1, n2_ref[...], sc_ref[0], sh_ref[0]).astype(BF16)
    gu = _dot(h, wi_ref[...])
    ff = _silu(gu[:, :D_FF]) * gu[:, D_FF:]
    y = x1 + g2_ref[0] * _dot(ff.astype(BF16), wd_ref[...])
    if len(o_refs) == 1:
        o_refs[0][...] = y
    else:
        @pl.when(pl.program_id(0) < CTX_TILES)
        def _():
            o_refs[0][...] = y

        @pl.when(pl.program_id(0) >= CTX_TILES)
        def _():
            o_refs[1][...] = y


def _merge_ffn(ya, of, ob, hgrn_norm_l, yc, u, x, modt, w_branch_l, w_out_l, norm2_l, w_ffn_in_l, w_ffn_out_l, l,
               split_out):
    row = lambda w: pl.BlockSpec((TM, w), lambda i: (i, 0))
    gate0 = C_GT * CB // D
    gate = lambda k: pl.BlockSpec((TM, D), lambda i: (i, gate0 + k))
    resident = lambda shape: pl.BlockSpec(shape, lambda i: (0, 0), pipeline_mode=pl.Buffered(1))
    x_specs, xs = _x_specs(x)
    if split_out:
        out_specs, _ = _x_specs((None, None))
        out_shape = [jax.ShapeDtypeStruct((BATCH * SEQ, D), F32), jax.ShapeDtypeStruct((DEC_BATCH * DEC_SEQ, D), F32)]
    else:
        out_specs = row(D)
        out_shape = jax.ShapeDtypeStruct((T, D), F32)
    return pl.pallas_call(
        functools.partial(_merge_ffn_kernel, len(xs)),
        grid=(NT,),
        in_specs=x_specs + [row(D_A), row(D_B), row(D_B), pl.BlockSpec((TM, CB), lambda i: (i, C_HG)),
                            pl.BlockSpec((1, DK_B), lambda i: (0, 0)), row(D_C),
                            gate(0), gate(1), gate(2), _mod_spec(l, 2),
                            resident((D_A + D_B + D_C, D)), resident((D, D)),
                            pl.BlockSpec((1, D), lambda i: (0, 0)),
                            _mod_spec(l, 4), _mod_spec(l, 3), _mod_spec(l, 5),
                            resident((D, 2 * D_FF)), resident((D_FF, D))],
        out_specs=out_specs,
        out_shape=out_shape,
        compiler_params=_cparams(("arbitrary",)),
        name="merge_ffn",
    )(*xs, ya, of, ob, u, hgrn_norm_l.reshape(1, DK_B), yc, u, u, u, modt, w_branch_l, w_out_l,
      norm2_l.reshape(1, D), modt, modt, modt, w_ffn_in_l, w_ffn_out_l)


def _layer(l, x, modt, cache_k, cache_v, state_hgrn, tabs, p, outs):
    (norm1, norm2, w_in, conv_w, conv_b, conv_ln_g, conv_ln_b, hgrn_lb, hgrn_norm, q_norm, k_norm, lambda_qk,
     subln, w_branch, w_out, w_ffn_in, w_ffn_out) = p
    cos_t, sin_t, bd, maps_f, maps_b = tabs
    new_k, new_v, new_s = outs
    u, qt, kb, vt, new_k, new_v = _in_proj(x, modt, norm1[l], w_in[l].astype(BF16), l, q_norm[l], k_norm[l],
                                           cos_t, sin_t, bd, new_k, new_v)
    ya = _conv_branch(u, conv_w[l], conv_b[l], conv_ln_g[l], conv_ln_b[l])
    lbp = hgrn_lb.reshape(DEPTH * 2, D_B)
    ob, new_s = _hgrn_pass(True, l, u, lbp, state_hgrn[:, l, 1], maps_b, new_s)
    of, new_s = _hgrn_pass(False, l, u, lbp, state_hgrn[:, l, 0], maps_f, new_s)
    ck = cache_k[:, l].reshape(DEC_BATCH * PAST, D_C).astype(BF16)
    cvt = cache_v[:, l].reshape(DEC_BATCH, PAST, D_C).astype(BF16).transpose(0, 2, 1)
    yc = _attention(l, qt, kb, vt, ck, cvt, lambda_qk[l], subln[l])
    x = _merge_ffn(ya, of, ob, hgrn_norm[l], yc, u, x, modt, w_branch[l].astype(BF16), w_out[l].astype(BF16), norm2[l],
                   w_ffn_in[l].astype(BF16), w_ffn_out[l].astype(BF16), l, split_out=l == DEPTH - 1)
    return x, (new_k, new_v, new_s)


def kernel(x_prompt, x_sample, cache_k, cache_v, state_hgrn, c, c_ctx, w_mod, b_mod, norm1, norm2, w_in, conv_w,
           conv_b, conv_ln_g, conv_ln_b, hgrn_lb, hgrn_norm, q_norm, k_norm, lambda_qk, subln, w_branch, w_out,
           w_ffn_in, w_ffn_out):
    x = (x_prompt.reshape(BATCH * SEQ, D), x_sample.reshape(DEC_BATCH * DEC_SEQ, D))
    cv8 = jnp.concatenate([c_ctx[None, :], c, jnp.zeros((8 - 1 - DEC_BATCH, D), F32)], axis=0)
    mod = _modulation(cv8, w_mod, b_mod)
    modt = mod.reshape(DEPTH, 8, 6, D).transpose(0, 2, 1, 3).reshape(DEPTH * 6 * 8, 1, D)
    cos_t, sin_t = _rope_tables()
    seg = np.arange(D_C) // DH_C
    bd = jnp.asarray(seg[:, None] == seg[None, :], BF16)
    maps = lambda rev: (jnp.asarray(_level_map(rev)), jnp.asarray(_low_map(rev)))
    tabs = (cos_t, sin_t, bd, maps(False), maps(True))
    p = (norm1, norm2, w_in, conv_w, conv_b, conv_ln_g, conv_ln_b, hgrn_lb, hgrn_norm, q_norm, k_norm, lambda_qk,
         subln, w_branch, w_out, w_ffn_in, w_ffn_out)
    outs = (jnp.zeros((BATCH, DEPTH, SEQ, D_C), F32), jnp.zeros((BATCH, DEPTH, SEQ, D_C), F32),
            jnp.zeros((BATCH, DEPTH, 2, H_B, DK_B, DK_B), F32))
    for l in range(DEPTH):
        x, outs = _layer(l, x, modt, cache_k, cache_v, state_hgrn, tabs, p, outs)
    new_k, new_v, new_s = outs
    y_prompt = x[0].reshape(BATCH, SEQ, D)
    y_sample = x[1].reshape(DEC_BATCH, DEC_SEQ, D)
    return (y_prompt, y_sample, new_k.reshape(BATCH, DEPTH, SEQ, H_C, 2, DH_C),
            new_v.reshape(BATCH, DEPTH, SEQ, H_C, 2 * DH_C), new_s)
```

```python
import functools
import math

import numpy as np
import jax
import jax.numpy as jnp
from jax import lax
from jax.experimental import pallas as pl
from jax.experimental.pallas import tpu as pltpu

F32 = jnp.float32
BF16 = jnp.bfloat16

D = 1024
BATCH, SEQ = 32, 256
DEPTH = 2
DEC_BATCH, DEC_SEQ = 2, 4096
PAST = 256
GRID_W = 64
D_A, CONV_K = 512, 31
H_B, DK_B = 4, 128
D_B = 512
H_C, DH_C = 4, 64
D_C = 512
ROPE_BASE = 10000.0
D_FF = 2816
IN_COLS = 8192

TM = 256
CTX_TILES = BATCH * SEQ // TM
SMP_TILES = DEC_SEQ // TM
T = BATCH * SEQ + DEC_BATCH * DEC_SEQ
NT = T // TM
CB = 512
HALO = 16
TQ = 512
VMEM_LIMIT = 56 * 1024 * 1024

C_A, C_AG, C_HQ, C_HI, C_HFF, C_HFB, C_HG, C_AQ, C_AK, C_AV, C_GT = 0, 1, 2, 3, 4, 5, 6, 7, 8, 9, 10


def _cparams(sem):
    return pltpu.CompilerParams(dimension_semantics=sem, vmem_limit_bytes=VMEM_LIMIT)


def _mod_row(i):
    return jnp.where(i < CTX_TILES, 0, 1 + (i - CTX_TILES) // SMP_TILES)


def _sigmoid(x):
    return 0.5 * jnp.tanh(0.5 * x) + 0.5


def _sigmoid_small_accurate(x):
    return jnp.exp(-jnp.log(1.0 + jnp.exp(-x)))


def _silu(x):
    return x * _sigmoid(x)


def _dot(a, b):
    return jnp.dot(a, b, preferred_element_type=F32)


def _dot_nt(a, b):
    return lax.dot_general(a, b, (((1,), (1,)), ((), ())), preferred_element_type=F32)


def _dot_tn(a, b):
    return lax.dot_general(a, b, (((0,), (0,)), ((), ())), preferred_element_type=F32)


def _split_bf16(x):
    hi = x.astype(BF16)
    lo = (x - hi.astype(F32)).astype(BF16)
    return hi, lo


MOD_TN = 1536


def _mod_kernel(cv_ref, w_ref, b_ref, o_ref):
    a = _silu(cv_ref[...])
    a_hi, a_lo = _split_bf16(a)
    w_hi, w_lo = _split_bf16(w_ref[0])
    acc = _dot(a_hi, w_hi) + _dot(a_lo, w_hi) + _dot(a_hi, w_lo)
    o_ref[0] = acc + b_ref[0]


def _modulation(cv8, w_mod, b_mod):
    return pl.pallas_call(
        _mod_kernel,
        grid=(DEPTH, 6 * D // MOD_TN),
        in_specs=[
            pl.BlockSpec((8, D), lambda l, j: (0, 0)),
            pl.BlockSpec((1, D, MOD_TN), lambda l, j: (l, 0, j)),
            pl.BlockSpec((1, 1, MOD_TN), lambda l, j: (l, 0, j)),
        ],
        out_specs=pl.BlockSpec((1, 8, MOD_TN), lambda l, j: (l, 0, j)),
        out_shape=jax.ShapeDtypeStruct((DEPTH, 8, 6 * D), F32),
        compiler_params=_cparams(("arbitrary", "arbitrary")),
        name="adaln_mod",
    )(cv8, w_mod, b_mod.reshape(DEPTH, 1, 6 * D))


def _mod_spec(l, which):
    return pl.BlockSpec((1, 1, D), lambda i: ((l * 6 + which) * 8 + _mod_row(i), 0, 0))


def _rms_mod(x, g, sc, sh):
    y = x * lax.rsqrt(jnp.mean(x * x, axis=-1, keepdims=True) + 1e-6) * g
    return y * (1.0 + sc) + sh


def _x_specs(x):
    if isinstance(x, tuple):
        return [pl.BlockSpec((TM, D), lambda i: (jnp.minimum(i, CTX_TILES - 1), 0)),
                pl.BlockSpec((TM, D), lambda i: (jnp.maximum(i - CTX_TILES, 0), 0))], list(x)
    return [pl.BlockSpec((TM, D), lambda i: (i, 0))], [x]


def _x_tile(x_refs):
    if len(x_refs) == 2:
        return jnp.where(pl.program_id(0) < CTX_TILES, x_refs[0][...], x_refs[1][...])
    return x_refs[0][...]


IN_TM = 512
IN_SPLIT = 2


IN_HALF = IN_COLS // IN_SPLIT
Q_OFF = C_AQ * CB
K_OFF = C_AK * CB - IN_HALF
assert IN_SPLIT == 2 and Q_OFF + CB == IN_HALF and K_OFF == 0 and C_AV == C_AK + 1


def _in_kernel(nx, *refs):
    (g_ref, sc_ref, sh_ref, w_ref, qn_ref, kn_ref, cos_ref, sin_ref, bd_ref, _k, _v,
     o_ref, qt_ref, kb_ref, vt_ref, nk_ref, nv_ref) = refs[nx:]
    j = pl.program_id(0)
    i = pl.program_id(1)
    n_ctx = BATCH * SEQ // IN_TM
    subs = [slice(t * TM, (t + 1) * TM) for t in range(IN_TM // TM)]
    bd = bd_ref[...]
    lane = lax.broadcasted_iota(jnp.int32, (TM, 128), 1)
    even = (lane & 1) == 0

    def hidden(r):
        x = jnp.where(i < n_ctx, refs[0][r, :], refs[1][r, :]) if nx == 2 else refs[0][r, :]
        return _rms_mod(x, g_ref[...], sc_ref[0], sh_ref[0]).astype(BF16)

    def seg_norm(xq, gain):
        ss = _dot((xq * xq).astype(BF16), bd)
        return xq * lax.rsqrt(ss * (1.0 / DH_C) + 1e-6) * gain

    def rope(xq, r):
        cos, sin = cos_ref[r, :], sin_ref[r, :]
        parts = []
        for s in range(D_C // 128):
            xs = xq[:, s * 128:(s + 1) * 128]
            sw = jnp.where(even, pltpu.roll(xs, 127, 1), pltpu.roll(xs, 1, 1))
            parts.append(xs * cos + sw * sin)
        return jnp.concatenate(parts, axis=1)

    @pl.when(j == 0)
    def _():
        hs = []
        for r in subs:
            hs.append(hidden(r))
            aq = _dot(hs[-1], w_ref[:, Q_OFF:])
            o_ref[r, Q_OFF:] = aq
            q = seg_norm(aq, qn_ref[...])
            qt_ref[:, r] = (rope(q, r) * (DH_C ** -0.5 * math.log2(math.e))).T.astype(BF16)
        for h, r in zip(hs, subs):
            o_ref[r, :Q_OFF] = _dot(h, w_ref[:, :Q_OFF])

    @pl.when(j == 1)
    def _():
        hs, ks, avs = [], [], []
        for t, r in enumerate(subs):
            hs.append(hidden(r))
            akv = _dot(hs[-1], w_ref[:, :2 * CB])
            o_ref[r, :2 * CB] = akv
            avs.append(akv[:, CB:])
            ks.append(seg_norm(akv[:, :CB], kn_ref[...]))
            kb_ref[r, :] = rope(ks[-1], r).astype(BF16)
            vt_ref[t] = avs[-1].T.astype(BF16)
        for h, r in zip(hs, subs):
            o_ref[r, 2 * CB:] = _dot(h, w_ref[:, 2 * CB:])

        @pl.when(i < n_ctx)
        def _():
            for t in range(IN_TM // SEQ):
                nk_ref[t, 0] = ks[t]
                nv_ref[t, 0] = avs[t]


def _in_proj(x, modt, norm1_l, w_in_l, l, q_norm_l, k_norm_l, cos_t, sin_t, bd, new_k, new_v):
    n_ctx = BATCH * SEQ // IN_TM
    per_seq = DEC_SEQ // IN_TM
    n_i = T // IN_TM
    mod_row = lambda i: jnp.where(i < n_ctx, 0, 1 + (i - n_ctx) // per_seq)
    mod = lambda which: pl.BlockSpec((1, 1, D), lambda j, i: ((l * 6 + which) * 8 + mod_row(i), 0, 0))
    if isinstance(x, tuple):
        x_specs = [pl.BlockSpec((IN_TM, D), lambda j, i: (jnp.minimum(i, n_ctx - 1), 0)),
                   pl.BlockSpec((IN_TM, D), lambda j, i: (jnp.maximum(i - n_ctx, 0), 0))]
        xs = list(x)
    else:
        x_specs, xs = [pl.BlockSpec((IN_TM, D), lambda j, i: (i, 0))], [x]
    vec = pl.BlockSpec((1, D_C), lambda j, i: (0, 0))
    tab = pl.BlockSpec((IN_TM, 128), lambda j, i: (jnp.where(i < n_ctx, 0, 1 + (i - n_ctx) % per_seq), 0))
    any_spec = pl.BlockSpec(memory_space=pl.ANY)
    q_i = lambda j, i: jnp.where(j == 0, i, n_i - 1)
    kv_i = lambda j, i: jnp.where(j == 1, i, 0)
    seqs = IN_TM // SEQ
    cache = pl.BlockSpec((seqs, 1, SEQ, D_C), lambda j, i: (jnp.minimum(kv_i(j, i), n_ctx - 1), l, 0, 0))
    n_in = len(xs) + 9
    return pl.pallas_call(
        functools.partial(_in_kernel, len(xs)),
        grid=(IN_SPLIT, n_i),
        in_specs=x_specs + [
            pl.BlockSpec((1, D), lambda j, i: (0, 0)),
            mod(1),
            mod(0),
            pl.BlockSpec((D, IN_HALF), lambda j, i: (0, j)),
            vec, vec, tab, tab,
            pl.BlockSpec((D_C, D_C), lambda j, i: (0, 0)),
            any_spec, any_spec,
        ],
        out_specs=[pl.BlockSpec((IN_TM, IN_HALF), lambda j, i: (i, j)),
                   pl.BlockSpec((D_C, IN_TM), lambda j, i: (0, q_i(j, i))),
                   pl.BlockSpec((IN_TM, D_C), lambda j, i: (kv_i(j, i), 0)),
                   pl.BlockSpec((IN_TM // TM, D_C, TM), lambda j, i: (kv_i(j, i), 0, 0)),
                   cache, cache],
        out_shape=[jax.ShapeDtypeStruct((T, IN_COLS), F32), jax.ShapeDtypeStruct((D_C, T), BF16),
                   jax.ShapeDtypeStruct((T, D_C), BF16), jax.ShapeDtypeStruct((NT, D_C, TM), BF16),
                   jax.ShapeDtypeStruct(new_k.shape, F32), jax.ShapeDtypeStruct(new_v.shape, F32)],
        input_output_aliases={n_in: 4, n_in + 1: 5},
        compiler_params=_cparams(("arbitrary", "arbitrary")),
        name="in_proj",
    )(*xs, norm1_l.reshape(1, D), modt, modt, w_in_l,
      jnp.tile(q_norm_l, D_C // DH_C).reshape(1, D_C), jnp.tile(k_norm_l, D_C // DH_C).reshape(1, D_C),
      cos_t, sin_t, bd, new_k, new_v)


CONV_RB = 32


def _conv_kernel(ac_ref, gc_ref, ap_ref, gp_ref, an_ref, gn_ref, w_ref, b_ref, lg_ref, lb_ref, o_ref, sh_ref,
                 cv_ref):
    i = pl.program_id(0)
    in_smp = i >= CTX_TILES
    pos = i % SMP_TILES
    has_prev = jnp.logical_and(in_smp, pos != 0)
    has_next = jnp.logical_and(in_smp, pos != SMP_TILES - 1)
    n = TM + 2 * HALO
    sh_ref[0, HALO:HALO + TM, :] = ac_ref[...] * _sigmoid(gc_ref[...])
    sh_ref[0, 0:HALO, :] = jnp.where(has_prev, ap_ref[...] * _sigmoid(gp_ref[...]), 0.0)
    sh_ref[0, HALO + TM:, :] = jnp.where(has_next, an_ref[...] * _sigmoid(gn_ref[...]), 0.0)
    for r in range(1, 8):
        sh_ref[r, 0:n - 8, :] = sh_ref[0, r:r + n - 8, :]
    off = HALO - CONV_K // 2

    def body(rb, carry):
        base = pl.multiple_of(rb * CONV_RB, CONV_RB)
        acc = jnp.zeros((CONV_RB, D_A), F32)
        for k in range(CONV_K):
            r = (off + k) % 8
            acc = acc + sh_ref[r, pl.ds(base + (off + k - r), CONV_RB), :] * w_ref[k:k + 1, :]
        cv_ref[pl.ds(base, CONV_RB), :] = acc
        return carry
    lax.fori_loop(0, TM // CONV_RB, body, 0)
    acc = cv_ref[...] + b_ref[...]
    mu = jnp.mean(acc, axis=-1, keepdims=True)
    xc = acc - mu
    y = xc * lax.rsqrt(jnp.mean(xc * xc, axis=-1, keepdims=True) + 1e-5)
    y = y * lg_ref[...] + lb_ref[...]
    o_ref[...] = _silu(y).astype(BF16)


def _conv_branch(u, conv_w_l, conv_b_l, ln_g_l, ln_b_l):
    r = TM // HALO
    last = T // HALO - 1
    cur = lambda c: pl.BlockSpec((TM, CB), lambda i: (i, c))
    prev = lambda c: pl.BlockSpec((HALO, CB), lambda i: (jnp.maximum(i * r - 1, 0), c))
    nxt = lambda c: pl.BlockSpec((HALO, CB), lambda i: (jnp.minimum((i + 1) * r, last), c))
    vec = pl.BlockSpec((1, D_A), lambda i: (0, 0))
    return pl.pallas_call(
        _conv_kernel,
        grid=(NT,),
        in_specs=[cur(C_A), cur(C_AG), prev(C_A), prev(C_AG), nxt(C_A), nxt(C_AG),
                  pl.BlockSpec((CONV_K, D_A), lambda i: (0, 0)), vec, vec, vec],
        out_specs=pl.BlockSpec((TM, D_A), lambda i: (i, 0)),
        out_shape=jax.ShapeDtypeStruct((T, D_A), BF16),
        scratch_shapes=[pltpu.VMEM((8, TM + 2 * HALO, D_A), F32), pltpu.VMEM((TM, D_A), F32)],
        compiler_params=_cparams(("arbitrary",)),
        name="conv_branch",
    )(u, u, u, u, u, u, conv_w_l, conv_b_l.reshape(1, D_A), ln_g_l.reshape(1, D_A), ln_b_l.reshape(1, D_A))


N_LEVELS = 8


def _level_map(reverse):
    t = np.arange(TM)[:, None]
    s = np.arange(TM)[None, :]
    x = t ^ s
    lv = np.zeros((TM, TM), np.int32)
    for bit in range(N_LEVELS):
        lv[(x >> bit) == 1] = bit + 1
    ok = (t < s) if reverse else (t > s)
    return np.where(ok, lv, 0).astype(np.int32)


def _shift_rows(x, k):
    return pltpu.roll(x, k % TM, 0)


def _cumsum_rows(x, row, reverse):
    sh = 1
    while sh < TM:
        if reverse:
            x = x + jnp.where(row < TM - sh, _shift_rows(x, -sh), 0.0)
        else:
            x = x + jnp.where(row >= sh, _shift_rows(x, sh), 0.0)
        sh *= 2
    return x


def _anchors(cum, row, reverse):
    out = []
    z = cum
    for bit in range(N_LEVELS):
        h = 1 << bit
        if 2 * h >= 16:
            c3 = cum.reshape(TM // (2 * h), 2 * h, DK_B)
            p = h if reverse else h - 1
            out.append(jnp.broadcast_to(c3[:, p:p + 1, :], c3.shape).reshape(TM, DK_B))
            continue
        hi = (row & h) != 0
        if reverse:
            out.append(jnp.where(hi, z, _shift_rows(z, -h)))
            z = jnp.where(hi, _shift_rows(z, h), z)
        else:
            out.append(jnp.where(hi, _shift_rows(z, h), z))
            z = jnp.where(hi, z, _shift_rows(z, -h))
    return out


LOW_BITS = 5
SAFE_RANGE = 60.0


def _low_map(reverse):
    t = np.arange(TM)[:, None]
    s = np.arange(TM)[None, :]
    same = (t >> LOW_BITS) == (s >> LOW_BITS)
    ok = (t <= s) if reverse else (t >= s)
    return (same & ok).astype(np.int32)


def _block_anchor(cum, reverse):
    n = 1 << LOW_BITS
    c3 = cum.reshape(TM // n, n, DK_B)
    p = n - 1 if reverse else 0
    return jnp.broadcast_to(c3[:, p:p + 1, :], c3.shape).reshape(TM, DK_B)


def _hgrn_kernel(reverse, l, *refs):
    (q_ref, v_ref, f_ref, lbp_ref, s0_ref, lv_ref, lo_ref, _, o_ref, ns_ref) = refs[:10]
    st_ref, qs_ref, ks_ref, cs_ref, os_ref, as_ref = refs[-6:]
    g = pl.program_id(0)
    i = NT - 1 - g if reverse else g
    is_ctx = i < CTX_TILES
    pos = i % SMP_TILES
    seq_start = jnp.logical_and(i >= CTX_TILES, pos == (SMP_TILES - 1 if reverse else 0))

    @pl.when(is_ctx)
    def _():
        st_ref[...] = jnp.zeros_like(st_ref)

    @pl.when(seq_start)
    def _():
        for h in range(H_B):
            st_ref[h] = s0_ref[0, h].T

    d = 1 if reverse else 0
    rows = [lbp_ref[k * 2 + d:k * 2 + d + 1, :] for k in range(DEPTH)]
    mx = functools.reduce(jnp.maximum, rows)
    ex = [jnp.exp(r - mx) for r in rows]
    den = functools.reduce(jnp.add, ex)
    lb_all = jnp.zeros_like(mx)
    for k in range(1, l + 1):
        lb_all = lb_all + ex[k] / den

    row = lax.broadcasted_iota(jnp.int32, (TM, DK_B), 0)
    spread = jnp.zeros((1, DK_B), F32)
    for h in range(H_B):
        sl = slice(h * DK_B, (h + 1) * DK_B)
        lb = lb_all[:, sl]
        fg = lb + (1.0 - lb) * _sigmoid_small_accurate(f_ref[:, sl])
        cum = _cumsum_rows(jnp.log(fg), row, reverse)
        ks_ref[h] = 1.0 - fg
        qs_ref[h] = _silu(q_ref[:, sl])
        cs_ref[h] = cum
        spread = jnp.maximum(spread, jnp.max(_block_anchor(cum, reverse) - cum, axis=0, keepdims=True))
    single_anchor_ok = jnp.max(spread) < SAFE_RANGE

    half = TM // 2
    halves = (slice(0, half), slice(half, TM))
    late, early = (halves[0], halves[1]) if reverse else (halves[1], halves[0])

    lv = lv_ref[0:half, 0:half]

    def operands(h, fq, fk):
        return (qs_ref[h] * fq).astype(BF16), (ks_ref[h] * fk).astype(BF16)

    for h in range(H_B):
        sl = slice(h * DK_B, (h + 1) * DK_B)
        q, kg, cum = qs_ref[h], ks_ref[h], cs_ref[h]
        vb = v_ref[:, sl].astype(BF16)
        tot = cum[0:1, :] if reverse else cum[TM - 1:TM, :]
        st = st_ref[h]
        o = _dot_nt((q * jnp.exp(cum)).astype(BF16), st.astype(BF16))
        anchors = _anchors(cum, row, reverse)
        fac = jnp.exp(-jnp.abs(cum - anchors[N_LEVELS - 1]))
        qf, kf = operands(h, fac, fac)
        o_top = _dot(_dot_nt(qf[late], kf[early]).astype(BF16), vb[early])
        zero = jnp.zeros_like(o_top)
        os_ref[h] = o + jnp.concatenate([o_top, zero] if reverse else [zero, o_top], axis=0)
        a = [jnp.zeros((half, half), F32) for _ in halves]
        for bit in range(LOW_BITS, N_LEVELS - 1):
            fac = jnp.exp(-jnp.abs(cum - anchors[bit]))
            qf, kf = operands(h, fac, fac)
            a = [jnp.where(lv == bit + 1, _dot_nt(qf[r], kf[r]), a[b]) for b, r in enumerate(halves)]
        for b in range(2):
            as_ref[h, b] = a[b]
        kbar = (kg * jnp.exp(tot - cum)).astype(BF16)
        st_ref[h] = jnp.exp(tot) * st + _dot_tn(vb, kbar)

    def heads(single_anchor):
        for h in range(H_B):
            sl = slice(h * DK_B, (h + 1) * DK_B)
            cum = cs_ref[h]
            v = v_ref[:, sl]
            vb = v.astype(BF16)
            o = os_ref[h]
            a = [as_ref[h, b] for b in range(2)]
            if single_anchor:
                e = _block_anchor(cum, reverse)
                qf, kf = operands(h, jnp.exp(cum - e), jnp.exp(e - cum))
                lo = lo_ref[0:half, 0:half] != 0
                a = [jnp.where(lo, _dot_nt(qf[r], kf[r]), a[b]) for b, r in enumerate(halves)]
            else:
                o = o + jnp.sum(qs_ref[h] * ks_ref[h], axis=-1, keepdims=True) * v
                anchors = _anchors(cum, row, reverse)
                for bit in range(LOW_BITS):
                    fac = jnp.exp(-jnp.abs(cum - anchors[bit]))
                    qf, kf = operands(h, fac, fac)
                    a = [jnp.where(lv == bit + 1, _dot_nt(qf[r], kf[r]), a[b]) for b, r in enumerate(halves)]
            o_ref[:, sl] = o + jnp.concatenate([_dot(a[b].astype(BF16), vb[r]) for b, r in enumerate(halves)],
                                               axis=0)

    @pl.when(single_anchor_ok)
    def _():
        heads(True)

    @pl.when(jnp.logical_not(single_anchor_ok))
    def _():
        heads(False)

    @pl.when(is_ctx)
    def _():
        for h in range(H_B):
            ns_ref[0, 0, 0, h] = st_ref[h].T


def _hgrn_pass(reverse, l, u, hgrn_lb, s0, maps, new_state):
    tile = (lambda g: NT - 1 - g) if reverse else (lambda g: g)
    col = lambda c: pl.BlockSpec((TM, CB), lambda g: (tile(g), c))
    s0_spec = pl.BlockSpec((1, H_B, DK_B, DK_B),
                           lambda g: (jnp.clip((tile(g) - CTX_TILES) // SMP_TILES, 0, DEC_BATCH - 1), 0, 0, 0))
    lbp_spec = pl.BlockSpec((DEPTH * 2, D_B), lambda g: (0, 0))
    map_spec = pl.BlockSpec((TM, TM), lambda g: (0, 0))
    d = 1 if reverse else 0
    ns_spec = pl.BlockSpec((1, 1, 1, H_B, DK_B, DK_B),
                           lambda g: (jnp.minimum(tile(g), CTX_TILES - 1), l, d, 0, 0, 0))
    any_spec = pl.BlockSpec(memory_space=pl.ANY)
    f_col = C_HFB if reverse else C_HFF
    in_specs = [col(C_HQ), col(C_HI), col(f_col), lbp_spec, s0_spec, map_spec, map_spec, any_spec]
    args = (u, u, u, hgrn_lb, s0, *maps, new_state)
    o_shape = jax.ShapeDtypeStruct((T, D_B), F32)
    head_tile = pltpu.VMEM((H_B, TM, DK_B), F32)
    return pl.pallas_call(
        functools.partial(_hgrn_kernel, reverse, l),
        grid=(NT,),
        in_specs=in_specs,
        out_specs=[pl.BlockSpec((TM, D_B), lambda g: (tile(g), 0)), ns_spec],
        out_shape=[o_shape, jax.ShapeDtypeStruct(new_state.shape, F32)],
        input_output_aliases={len(args) - 1: 1},
        scratch_shapes=[pltpu.VMEM((H_B, DK_B, DK_B), F32), head_tile, head_tile, head_tile, head_tile,
                        pltpu.VMEM((H_B, 2, TM // 2, TM // 2), F32)],
        compiler_params=_cparams(("arbitrary",)),
        name="hgrn_bwd" if reverse else "hgrn_fwd",
    )(*args)


def _rope_tables():
    rows = DEC_SEQ // GRID_W
    row = jnp.broadcast_to(jnp.arange(rows)[:, None], (rows, GRID_W)).reshape(-1).astype(F32)
    colp = jnp.broadcast_to(jnp.arange(GRID_W)[None, :], (rows, GRID_W)).reshape(-1).astype(F32)
    half = DH_C // 2
    inv = ROPE_BASE ** (-jnp.arange(0, half, 2, dtype=F32) / half)
    ang = jnp.concatenate([row[:, None] * inv, colp[:, None] * inv], axis=-1)
    cos = jnp.repeat(jnp.cos(ang), 2, axis=-1)
    sin = jnp.repeat(jnp.sin(ang), 2, axis=-1) * jnp.tile(jnp.array([-1.0, 1.0], F32), half)
    cos = jnp.concatenate([jnp.ones((IN_TM, DH_C), F32), cos], axis=0)
    sin = jnp.concatenate([jnp.zeros((IN_TM, DH_C), F32), sin], axis=0)
    return jnp.tile(cos, (1, 2)), jnp.tile(sin, (1, 2))


def _attn_kernel(lam_init, n_new, has_cache, n_heads, *refs):
    if has_cache:
        qt_ref, kc_ref, vc_ref, kn_ref, vn_ref, lam_ref, sub_ref, _, o_ref = refs[:9]
    else:
        qt_ref, kn_ref, vn_ref, lam_ref, sub_ref, _, o_ref = refs[:7]
    qm_ref, s_ref, p_ref, m_ref, l_ref, a_ref, acc_ref = refs[-7:]
    tq = qt_ref.shape[1]
    ncb = tq // 128
    lq = lam_ref[...]
    lam = (jnp.exp(jnp.sum(lq[0:1] * lq[1:2], axis=-1, keepdims=True))
           - jnp.exp(jnp.sum(lq[2:3] * lq[3:4], axis=-1, keepdims=True)) + lam_init)

    def scores(k, buf):
        for c in range(2):
            s = _dot(k, qm_ref[c])
            for cb in range(ncb):
                s_ref[buf, c, cb] = s[:, cb * 128:(cb + 1) * 128]

    def softmax_update(buf):
        for c in range(2):
            for cb in range(ncb):
                cols = slice(cb * 128, (cb + 1) * 128)
                s = s_ref[buf, c, cb]
                m_old = m_ref[c, :, cols]
                m_new = jnp.maximum(m_old, jnp.max(s, axis=0, keepdims=True))
                alpha = jnp.exp2(m_old - m_new)
                p = jnp.exp2(s - m_new)
                l_ref[c, :, cols] = alpha * l_ref[c, :, cols] + jnp.sum(p, axis=0, keepdims=True)
                m_ref[c, :, cols] = m_new
                a_ref[buf, c, :, cols] = alpha
                p_ref[buf, c, cb] = p.astype(BF16)

    def accumulate(vt, buf):
        for c in range(2):
            p = jnp.concatenate([p_ref[buf, c, cb] for cb in range(ncb)], axis=1)
            acc_ref[c] = a_ref[buf, c] * acc_ref[c] + _dot(vt, p)

    for h in range(n_heads):
        hs = slice(h * 128, (h + 1) * 128)
        qt = qt_ref[hs, :]
        sub_i = lax.broadcasted_iota(jnp.int32, qt.shape, 0)
        qm_ref[0] = jnp.where(sub_i < DH_C, qt, jnp.zeros_like(qt))
        qm_ref[1] = jnp.where(sub_i >= DH_C, qt, jnp.zeros_like(qt))
        m_ref[...] = jnp.full(m_ref.shape, -1e30, F32)
        l_ref[...] = jnp.zeros_like(l_ref)
        acc_ref[...] = jnp.zeros_like(acc_ref)

        if n_new == 1:
            scores(kn_ref[:, hs], 0)
            softmax_update(0)
            accumulate(vn_ref[0, hs, :], 0)
        else:
            new_k = lambda j: kn_ref[pl.ds(pl.multiple_of(j * TM, TM), TM), hs]
            scores(new_k(0), 0)

            def body(i, carry):
                scores(new_k(2 * i + 1), 1)
                softmax_update(0)
                accumulate(vn_ref[2 * i, hs, :], 0)
                scores(new_k(2 * i + 2), 0)
                softmax_update(1)
                accumulate(vn_ref[2 * i + 1, hs, :], 1)
                return carry
            lax.fori_loop(0, n_new // 2 - 1, body, 0)
            scores(kn_ref[(n_new - 1) * TM:, hs], 1)
            softmax_update(0)
            accumulate(vn_ref[n_new - 2, hs, :], 0)
            scores(kc_ref[:, hs], 0)
            softmax_update(1)
            accumulate(vn_ref[n_new - 1, hs, :], 1)
            softmax_update(0)
            accumulate(vc_ref[0, hs, :], 0)

        for cb in range(ncb):
            cols = slice(cb * 128, (cb + 1) * 128)
            o = acc_ref[0, :, cols] / l_ref[0, :, cols] - lam * (acc_ref[1, :, cols] / l_ref[1, :, cols])
            y = o * lax.rsqrt(jnp.mean(o * o, axis=0, keepdims=True) + 1e-6)
            o_ref[cols, hs] = (y.T * sub_ref[...] * (1.0 - lam_init)).astype(BF16)


def _attn_scratch(tq):
    ncb = tq // 128
    return [pltpu.VMEM((2, 128, tq), BF16), pltpu.VMEM((2, 2, ncb, TM, 128), F32),
            pltpu.VMEM((2, 2, ncb, TM, 128), BF16),
            pltpu.VMEM((2, 1, tq), F32), pltpu.VMEM((2, 1, tq), F32), pltpu.VMEM((2, 2, 1, tq), F32),
            pltpu.VMEM((2, 128, tq), F32)]


def _attention(l, qt, kb, vt, ck, cvt, lambda_qk_l, subln_l):
    lam_init = 0.8 - 0.6 * math.exp(-0.3 * l)
    sub = subln_l.reshape(1, 2 * DH_C)
    o_ctx = pl.pallas_call(
        functools.partial(_attn_kernel, lam_init, 1, False, H_C),
        grid=(BATCH,),
        in_specs=[pl.BlockSpec((D_C, SEQ), lambda b: (0, b)),
                  pl.BlockSpec((SEQ, D_C), lambda b: (b, 0)),
                  pl.BlockSpec((1, D_C, TM), lambda b: (b, 0, 0)),
                  pl.BlockSpec((4, DH_C), lambda b: (0, 0)),
                  pl.BlockSpec((1, 2 * DH_C), lambda b: (0, 0)),
                  pl.BlockSpec(memory_space=pl.ANY)],
        out_specs=pl.BlockSpec((SEQ, D_C), lambda b: (b, 0)),
        out_shape=jax.ShapeDtypeStruct((T, D_C), BF16),
        input_output_aliases={5: 0},
        scratch_shapes=_attn_scratch(SEQ),
        compiler_params=_cparams(("arbitrary",)),
        name="attn_ctx",
    )(qt, kb, vt, lambda_qk_l, sub, jnp.zeros((T, D_C), BF16))

    nq = DEC_SEQ // TQ
    base_q = BATCH * SEQ // TQ
    base_kv = BATCH * SEQ // DEC_SEQ
    o_smp = pl.pallas_call(
        functools.partial(_attn_kernel, lam_init, SMP_TILES, True, 1),
        grid=(DEC_BATCH, H_C, nq),
        in_specs=[pl.BlockSpec((128, TQ), lambda b, h, j: (h, base_q + b * nq + j)),
                  pl.BlockSpec((PAST, 128), lambda b, h, j: (b, h)),
                  pl.BlockSpec((1, 128, PAST), lambda b, h, j: (b, h, 0)),
                  pl.BlockSpec((DEC_SEQ, 128), lambda b, h, j: (base_kv + b, h)),
                  pl.BlockSpec((SMP_TILES, 128, TM), lambda b, h, j: (base_kv + b, h, 0)),
                  pl.BlockSpec((4, DH_C), lambda b, h, j: (0, 0)),
                  pl.BlockSpec((1, 2 * DH_C), lambda b, h, j: (0, 0)),
                  pl.BlockSpec(memory_space=pl.ANY)],
        out_specs=pl.BlockSpec((TQ, 128), lambda b, h, j: (base_q + b * nq + j, h)),
        out_shape=jax.ShapeDtypeStruct((T, D_C), BF16),
        input_output_aliases={7: 0},
        scratch_shapes=_attn_scratch(TQ),
        compiler_params=_cparams(("arbitrary", "arbitrary", "arbitrary")),
        name="attn_smp",
    )(qt, ck, cvt, kb, vt, lambda_qk_l, sub, o_ctx)
    return o_smp


def _merge_ffn_kernel(nx, *refs):
    (a_ref, bf_ref, bb_ref, hg_ref, hn_ref, c_ref, ga_ref, gb_ref, gc_ref, g1_ref, wb_ref, wo_ref,
     n2_ref, sc_ref, sh_ref, g2_ref, wi_ref, wd_ref) = refs[nx:nx + 18]
    o_refs = refs[nx + 18:]
    hb = []
    for h in range(H_B):
        hs = slice(h * DK_B, (h + 1) * DK_B)
        o = bf_ref[:, hs] + bb_ref[:, hs]
        y = o * lax.rsqrt(jnp.mean(o * o, axis=-1, keepdims=True) + 1e-6) * hn_ref[...]
        hb.append((y * _silu(hg_ref[:, hs])).astype(BF16))
    ya = _dot(a_ref[...], wb_ref[0:D_A, :])
    yb = _dot(jnp.concatenate(hb, axis=1), wb_ref[D_A:D_A + D_B, :])
    yc = _dot(c_ref[...], wb_ref[D_A + D_B:, :])
    m = _sigmoid(ga_ref[...]) * ya + _sigmoid(gb_ref[...]) * yb + _sigmoid(gc_ref[...]) * yc
    x1 = _x_tile(refs[:nx]) + g1_ref[0] * _dot(m.astype(BF16), wo_ref[...])
    h = _rms_mod(x1, n2_ref[...], sc_ref[0], sh_ref[0]).astype(BF16)
    gu = _dot(h, wi_ref[...])
    ff = _silu(gu[:, :D_FF]) * gu[:, D_FF:]
    y = x1 + g2_ref[0] * _dot(ff.astype(BF16), wd_ref[...])
    if len(o_refs) == 1:
        o_refs[0][...] = y
    else:
        @pl.when(pl.program_id(0) < CTX_TILES)
        def _():
            o_refs[0][...] = y

        @pl.when(pl.program_id(0) >= CTX_TILES)
        def _():
            o_refs[1][...] = y


def _merge_ffn(ya, of, ob, hgrn_norm_l, yc, u, x, modt, w_branch_l, w_out_l, norm2_l, w_ffn_in_l, w_ffn_out_l, l,
               split_out):
    row = lambda w: pl.BlockSpec((TM, w), lambda i: (i, 0))
    gate0 = C_GT * CB // D
    gate = lambda k: pl.BlockSpec((TM, D), lambda i: (i, gate0 + k))
    resident = lambda shape: pl.BlockSpec(shape, lambda i: (0, 0), pipeline_mode=pl.Buffered(1))
    x_specs, xs = _x_specs(x)
    if split_out:
        out_specs, _ = _x_specs((None, None))
        out_shape = [jax.ShapeDtypeStruct((BATCH * SEQ, D), F32), jax.ShapeDtypeStruct((DEC_BATCH * DEC_SEQ, D), F32)]
    else:
        out_specs = row(D)
        out_shape = jax.ShapeDtypeStruct((T, D), F32)
    return pl.pallas_call(
        functools.partial(_merge_ffn_kernel, len(xs)),
        grid=(NT,),
        in_specs=x_specs + [row(D_A), row(D_B), row(D_B), pl.BlockSpec((TM, CB), lambda i: (i, C_HG)),
                            pl.BlockSpec((1, DK_B), lambda i: (0, 0)), row(D_C),
                            gate(0), gate(1), gate(2), _mod_spec(l, 2),
                            resident((D_A + D_B + D_C, D)), resident((D, D)),
                            pl.BlockSpec((1, D), lambda i: (0, 0)),
                            _mod_spec(l, 4), _mod_spec(l, 3), _mod_spec(l, 5),
                            resident((D, 2 * D_FF)), resident((D_FF, D))],
        out_specs=out_specs,
        out_shape=out_shape,
        compiler_params=_cparams(("arbitrary",)),
        name="merge_ffn",
    )(*xs, ya, of, ob, u, hgrn_norm_l.reshape(1, DK_B), yc, u, u, u, modt, w_branch_l, w_out_l,
      norm2_l.reshape(1, D), modt, modt, modt, w_ffn_in_l, w_ffn_out_l)


def _layer(l, x, modt, cache_k, cache_v, state_hgrn, tabs, p, outs):
    (norm1, norm2, w_in, conv_w, conv_b, conv_ln_g, conv_ln_b, hgrn_lb, hgrn_norm, q_norm, k_norm, lambda_qk,
     subln, w_branch, w_out, w_ffn_in, w_ffn_out) = p
    cos_t, sin_t, bd, maps_f, maps_b = tabs
    new_k, new_v, new_s = outs
    u, qt, kb, vt, new_k, new_v = _in_proj(x, modt, norm1[l], w_in[l].astype(BF16), l, q_norm[l], k_norm[l],
                                           cos_t, sin_t, bd, new_k, new_v)
    ya = _conv_branch(u, conv_w[l], conv_b[l], conv_ln_g[l], conv_ln_b[l])
    lbp = hgrn_lb.reshape(DEPTH * 2, D_B)
    ob, new_s = _hgrn_pass(True, l, u, lbp, state_hgrn[:, l, 1], maps_b, new_s)
    of, new_s = _hgrn_pass(False, l, u, lbp, state_hgrn[:, l, 0], maps_f, new_s)
    ck = cache_k[:, l].reshape(DEC_BATCH * PAST, D_C).astype(BF16)
    cvt = cache_v[:, l].reshape(DEC_BATCH, PAST, D_C).astype(BF16).transpose(0, 2, 1)
    yc = _attention(l, qt, kb, vt, ck, cvt, lambda_qk[l], subln[l])
    x = _merge_ffn(ya, of, ob, hgrn_norm[l], yc, u, x, modt, w_branch[l].astype(BF16), w_out[l].astype(BF16), norm2[l],
                   w_ffn_in[l].astype(BF16), w_ffn_out[l].astype(BF16), l, split_out=l == DEPTH - 1)
    return x, (new_k, new_v, new_s)


def kernel(x_prompt, x_sample, cache_k, cache_v, state_hgrn, c, c_ctx, w_mod, b_mod, norm1, norm2, w_in, conv_w,
           conv_b, conv_ln_g, conv_ln_b, hgrn_lb, hgrn_norm, q_norm, k_norm, lambda_qk, subln, w_branch, w_out,
           w_ffn_in, w_ffn_out):
    x = (x_prompt.reshape(BATCH * SEQ, D), x_sample.reshape(DEC_BATCH * DEC_SEQ, D))
    cv8 = jnp.concatenate([c_ctx[None, :], c, jnp.zeros((8 - 1 - DEC_BATCH, D), F32)], axis=0)
    mod = _modulation(cv8, w_mod, b_mod)
    modt = mod.reshape(DEPTH, 8, 6, D).transpose(0, 2, 1, 3).reshape(DEPTH * 6 * 8, 1, D)
    cos_t, sin_t = _rope_tables()
    seg = np.arange(D_C) // DH_C
    bd = jnp.asarray(seg[:, None] == seg[None, :], BF16)
    maps = lambda rev: (jnp.asarray(_level_map(rev)), jnp.asarray(_low_map(rev)))
    tabs = (cos_t, sin_t, bd, maps(False), maps(True))
    p = (norm1, norm2, w_in, conv_w, conv_b, conv_ln_g, conv_ln_b, hgrn_lb, hgrn_norm, q_norm, k_norm, lambda_qk,
         subln, w_branch, w_out, w_ffn_in, w_ffn_out)
    outs = (jnp.zeros((BATCH, DEPTH, SEQ, D_C), F32), jnp.zeros((BATCH, DEPTH, SEQ, D_C), F32),
            jnp.zeros((BATCH, DEPTH, 2, H_B, DK_B, DK_B), F32))
    for l in range(DEPTH):
        x, outs = _layer(l, x, modt, cache_k, cache_v, state_hgrn, tabs, p, outs)
    new_k, new_v, new_s = outs
    y_prompt = x[0].reshape(BATCH, SEQ, D)
    y_sample = x[1].reshape(DEC_BATCH, DEC_SEQ, D)
    return (y_prompt, y_sample, new_k.reshape(BATCH, DEPTH, SEQ, H_C, 2, DH_C),
            new_v.reshape(BATCH, DEPTH, SEQ, H_C, 2 * DH_C), new_s)
```

```python
import functools
import math

import numpy as np
import jax
import jax.numpy as jnp
from jax import lax
from jax.experimental import pallas as pl
from jax.experimental.pallas import tpu as pltpu

F32 = jnp.float32
BF16 = jnp.bfloat16

D = 1024
BATCH, SEQ = 32, 256
DEPTH = 2
DEC_BATCH, DEC_SEQ = 2, 4096
PAST = 256
GRID_W = 64
D_A, CONV_K = 512, 31
H_B, DK_B = 4, 128
D_B = 512
H_C, DH_C = 4, 64
D_C = 512
ROPE_BASE = 10000.0
D_FF = 2816
IN_COLS = 8192

TM = 256
CTX_TILES = BATCH * SEQ // TM
SMP_TILES = DEC_SEQ // TM
T = BATCH * SEQ + DEC_BATCH * DEC_SEQ
NT = T // TM
CB = 512
HALO = 16
TQ = 512
VMEM_LIMIT = 56 * 1024 * 1024

C_A, C_AG, C_HQ, C_HI, C_HFF, C_HFB, C_HG, C_AQ, C_AK, C_AV, C_GT = 0, 1, 2, 3, 4, 5, 6, 7, 8, 9, 10


def _cparams(sem):
    return pltpu.CompilerParams(dimension_semantics=sem, vmem_limit_bytes=VMEM_LIMIT)


def _mod_row(i):
    return jnp.where(i < CTX_TILES, 0, 1 + (i - CTX_TILES) // SMP_TILES)


def _sigmoid(x):
    return 0.5 * jnp.tanh(0.5 * x) + 0.5


def _sigmoid_small_accurate(x):
    return jnp.exp(-jnp.log(1.0 + jnp.exp(-x)))


def _silu(x):
    return x * _sigmoid(x)


def _dot(a, b):
    return jnp.dot(a, b, preferred_element_type=F32)


def _dot_nt(a, b):
    return lax.dot_general(a, b, (((1,), (1,)), ((), ())), preferred_element_type=F32)


def _dot_tn(a, b):
    return lax.dot_general(a, b, (((0,), (0,)), ((), ())), preferred_element_type=F32)


def _split_bf16(x):
    hi = x.astype(BF16)
    lo = (x - hi.astype(F32)).astype(BF16)
    return hi, lo


MOD_TN = 1536


def _mod_kernel(cv_ref, w_ref, b_ref, o_ref):
    a = _silu(cv_ref[...])
    a_hi, a_lo = _split_bf16(a)
    w_hi, w_lo = _split_bf16(w_ref[0])
    acc = _dot(a_hi, w_hi) + _dot(a_lo, w_hi) + _dot(a_hi, w_lo)
    o_ref[0] = acc + b_ref[0]


def _modulation(cv8, w_mod, b_mod):
    return pl.pallas_call(
        _mod_kernel,
        grid=(DEPTH, 6 * D // MOD_TN),
        in_specs=[
            pl.BlockSpec((8, D), lambda l, j: (0, 0)),
            pl.BlockSpec((1, D, MOD_TN), lambda l, j: (l, 0, j)),
            pl.BlockSpec((1, 1, MOD_TN), lambda l, j: (l, 0, j)),
        ],
        out_specs=pl.BlockSpec((1, 8, MOD_TN), lambda l, j: (l, 0, j)),
        out_shape=jax.ShapeDtypeStruct((DEPTH, 8, 6 * D), F32),
        compiler_params=_cparams(("arbitrary", "arbitrary")),
        name="adaln_mod",
    )(cv8, w_mod, b_mod.reshape(DEPTH, 1, 6 * D))


def _mod_spec(l, which):
    return pl.BlockSpec((1, 1, D), lambda i: ((l * 6 + which) * 8 + _mod_row(i), 0, 0))


def _rms_mod(x, g, sc, sh):
    y = x * lax.rsqrt(jnp.mean(x * x, axis=-1, keepdims=True) + 1e-6) * g
    return y * (1.0 + sc) + sh


def _x_specs(x):
    if isinstance(x, tuple):
        return [pl.BlockSpec((TM, D), lambda i: (jnp.minimum(i, CTX_TILES - 1), 0)),
                pl.BlockSpec((TM, D), lambda i: (jnp.maximum(i - CTX_TILES, 0), 0))], list(x)
    return [pl.BlockSpec((TM, D), lambda i: (i, 0))], [x]


def _x_tile(x_refs):
    if len(x_refs) == 2:
        return jnp.where(pl.program_id(0) < CTX_TILES, x_refs[0][...], x_refs[1][...])
    return x_refs[0][...]


IN_TM = 512
IN_SPLIT = 2


IN_HALF = IN_COLS // IN_SPLIT
Q_OFF = C_AQ * CB
K_OFF = C_AK * CB - IN_HALF
assert IN_SPLIT == 2 and Q_OFF + CB == IN_HALF and K_OFF == 0 and C_AV == C_AK + 1


def _in_kernel(nx, *refs):
    (g_ref, sc_ref, sh_ref, w_ref, qn_ref, kn_ref, cos_ref, sin_ref, bd_ref, _k, _v,
     o_ref, qt_ref, kb_ref, vt_ref, nk_ref, nv_ref) = refs[nx:]
    j = pl.program_id(0)
    i = pl.program_id(1)
    n_ctx = BATCH * SEQ // IN_TM
    subs = [slice(t * TM, (t + 1) * TM) for t in range(IN_TM // TM)]
    bd = bd_ref[...]
    lane = lax.broadcasted_iota(jnp.int32, (TM, 128), 1)
    even = (lane & 1) == 0

    def hidden(r):
        x = jnp.where(i < n_ctx, refs[0][r, :], refs[1][r, :]) if nx == 2 else refs[0][r, :]
        return _rms_mod(x, g_ref[...], sc_ref[0], sh_ref[0]).astype(BF16)

    def seg_norm(xq, gain):
        ss = _dot((xq * xq).astype(BF16), bd)
        return xq * lax.rsqrt(ss * (1.0 / DH_C) + 1e-6) * gain

    def rope(xq, r):
        cos, sin = cos_ref[r, :], sin_ref[r, :]
        parts = []
        for s in range(D_C // 128):
            xs = xq[:, s * 128:(s + 1) * 128]
            sw = jnp.where(even, pltpu.roll(xs, 127, 1), pltpu.roll(xs, 1, 1))
            parts.append(xs * cos + sw * sin)
        return jnp.concatenate(parts, axis=1)

    @pl.when(j == 0)
    def _():
        hs = []
        for r in subs:
            hs.append(hidden(r))
            aq = _dot(hs[-1], w_ref[:, Q_OFF:])
            o_ref[r, Q_OFF:] = aq
            q = seg_norm(aq, qn_ref[...])
            qt_ref[:, r] = (rope(q, r) * (DH_C ** -0.5 * math.log2(math.e))).T.astype(BF16)
        for h, r in zip(hs, subs):
            o_ref[r, :Q_OFF] = _dot(h, w_ref[:, :Q_OFF])

    @pl.when(j == 1)
    def _():
        hs, ks, avs = [], [], []
        for t, r in enumerate(subs):
            hs.append(hidden(r))
            akv = _dot(hs[-1], w_ref[:, :2 * CB])
            o_ref[r, :2 * CB] = akv
            avs.append(akv[:, CB:])
            ks.append(seg_norm(akv[:, :CB], kn_ref[...]))
            kb_ref[r, :] = rope(ks[-1], r).astype(BF16)
            vt_ref[t] = avs[-1].T.astype(BF16)
        for h, r in zip(hs, subs):
            o_ref[r, 2 * CB:] = _dot(h, w_ref[:, 2 * CB:])

        @pl.when(i < n_ctx)
        def _():
            for t in range(IN_TM // SEQ):
                nk_ref[t, 0] = ks[t]
                nv_ref[t, 0] = avs[t]


def _in_proj(x, modt, norm1_l, w_in_l, l, q_norm_l, k_norm_l, cos_t, sin_t, bd, new_k, new_v):
    n_ctx = BATCH * SEQ // IN_TM
    per_seq = DEC_SEQ // IN_TM
    n_i = T // IN_TM
    mod_row = lambda i: jnp.where(i < n_ctx, 0, 1 + (i - n_ctx) // per_seq)
    mod = lambda which: pl.BlockSpec((1, 1, D), lambda j, i: ((l * 6 + which) * 8 + mod_row(i), 0, 0))
    if isinstance(x, tuple):
        x_specs = [pl.BlockSpec((IN_TM, D), lambda j, i: (jnp.minimum(i, n_ctx - 1), 0)),
                   pl.BlockSpec((IN_TM, D), lambda j, i: (jnp.maximum(i - n_ctx, 0), 0))]
        xs = list(x)
    else:
        x_specs, xs = [pl.BlockSpec((IN_TM, D), lambda j, i: (i, 0))], [x]
    vec = pl.BlockSpec((1, D_C), lambda j, i: (0, 0))
    tab = pl.BlockSpec((IN_TM, 128), lambda j, i: (jnp.where(i < n_ctx, 0, 1 + (i - n_ctx) % per_seq), 0))
    any_spec = pl.BlockSpec(memory_space=pl.ANY)
    q_i = lambda j, i: jnp.where(j == 0, i, n_i - 1)
    kv_i = lambda j, i: jnp.where(j == 1, i, 0)
    seqs = IN_TM // SEQ
    cache = pl.BlockSpec((seqs, 1, SEQ, D_C), lambda j, i: (jnp.minimum(kv_i(j, i), n_ctx - 1), l, 0, 0))
    n_in = len(xs) + 9
    return pl.pallas_call(
        functools.partial(_in_kernel, len(xs)),
        grid=(IN_SPLIT, n_i),
        in_specs=x_specs + [
            pl.BlockSpec((1, D), lambda j, i: (0, 0)),
            mod(1),
            mod(0),
            pl.BlockSpec((D, IN_HALF), lambda j, i: (0, j)),
            vec, vec, tab, tab,
            pl.BlockSpec((D_C, D_C), lambda j, i: (0, 0)),
            any_spec, any_spec,
        ],
        out_specs=[pl.BlockSpec((IN_TM, IN_HALF), lambda j, i: (i, j)),
                   pl.BlockSpec((D_C, IN_TM), lambda j, i: (0, q_i(j, i))),
                   pl.BlockSpec((IN_TM, D_C), lambda j, i: (kv_i(j, i), 0)),
                   pl.BlockSpec((IN_TM // TM, D_C, TM), lambda j, i: (kv_i(j, i), 0, 0)),
                   cache, cache],
        out_shape=[jax.ShapeDtypeStruct((T, IN_COLS), F32), jax.ShapeDtypeStruct((D_C, T), BF16),
                   jax.ShapeDtypeStruct((T, D_C), BF16), jax.ShapeDtypeStruct((NT, D_C, TM), BF16),
                   jax.ShapeDtypeStruct(new_k.shape, F32), jax.ShapeDtypeStruct(new_v.shape, F32)],
        input_output_aliases={n_in: 4, n_in + 1: 5},
        compiler_params=_cparams(("arbitrary", "arbitrary")),
        name="in_proj",
    )(*xs, norm1_l.reshape(1, D), modt, modt, w_in_l,
      jnp.tile(q_norm_l, D_C // DH_C).reshape(1, D_C), jnp.tile(k_norm_l, D_C // DH_C).reshape(1, D_C),
      cos_t, sin_t, bd, new_k, new_v)


CONV_RB = 32


def _conv_kernel(ac_ref, gc_ref, ap_ref, gp_ref, an_ref, gn_ref, w_ref, b_ref, lg_ref, lb_ref, o_ref, sh_ref,
                 cv_ref):
    i = pl.program_id(0)
    in_smp = i >= CTX_TILES
    pos = i % SMP_TILES
    has_prev = jnp.logical_and(in_smp, pos != 0)
    has_next = jnp.logical_and(in_smp, pos != SMP_TILES - 1)
    n = TM + 2 * HALO
    sh_ref[0, HALO:HALO + TM, :] = ac_ref[...] * _sigmoid(gc_ref[...])
    sh_ref[0, 0:HALO, :] = jnp.where(has_prev, ap_ref[...] * _sigmoid(gp_ref[...]), 0.0)
    sh_ref[0, HALO + TM:, :] = jnp.where(has_next, an_ref[...] * _sigmoid(gn_ref[...]), 0.0)
    for r in range(1, 8):
        sh_ref[r, 0:n - 8, :] = sh_ref[0, r:r + n - 8, :]
    off = HALO - CONV_K // 2

    def body(rb, carry):
        base = pl.multiple_of(rb * CONV_RB, CONV_RB)
        acc = jnp.zeros((CONV_RB, D_A), F32)
        for k in range(CONV_K):
            r = (off + k) % 8
            acc = acc + sh_ref[r, pl.ds(base + (off + k - r), CONV_RB), :] * w_ref[k:k + 1, :]
        cv_ref[pl.ds(base, CONV_RB), :] = acc
        return carry
    lax.fori_loop(0, TM // CONV_RB, body, 0)
    acc = cv_ref[...] + b_ref[...]
    mu = jnp.mean(acc, axis=-1, keepdims=True)
    xc = acc - mu
    y = xc * lax.rsqrt(jnp.mean(xc * xc, axis=-1, keepdims=True) + 1e-5)
    y = y * lg_ref[...] + lb_ref[...]
    o_ref[...] = _silu(y).astype(BF16)


def _conv_branch(u, conv_w_l, conv_b_l, ln_g_l, ln_b_l):
    r = TM // HALO
    last = T // HALO - 1
    cur = lambda c: pl.BlockSpec((TM, CB), lambda i: (i, c))
    prev = lambda c: pl.BlockSpec((HALO, CB), lambda i: (jnp.maximum(i * r - 1, 0), c))
    nxt = lambda c: pl.BlockSpec((HALO, CB), lambda i: (jnp.minimum((i + 1) * r, last), c))
    vec = pl.BlockSpec((1, D_A), lambda i: (0, 0))
    return pl.pallas_call(
        _conv_kernel,
        grid=(NT,),
        in_specs=[cur(C_A), cur(C_AG), prev(C_A), prev(C_AG), nxt(C_A), nxt(C_AG),
                  pl.BlockSpec((CONV_K, D_A), lambda i: (0, 0)), vec, vec, vec],
        out_specs=pl.BlockSpec((TM, D_A), lambda i: (i, 0)),
        out_shape=jax.ShapeDtypeStruct((T, D_A), BF16),
        scratch_shapes=[pltpu.VMEM((8, TM + 2 * HALO, D_A), F32), pltpu.VMEM((TM, D_A), F32)],
        compiler_params=_cparams(("arbitrary",)),
        name="conv_branch",
    )(u, u, u, u, u, u, conv_w_l, conv_b_l.reshape(1, D_A), ln_g_l.reshape(1, D_A), ln_b_l.reshape(1, D_A))


N_LEVELS = 8


def _level_map(reverse):
    t = np.arange(TM)[:, None]
    s = np.arange(TM)[None, :]
    x = t ^ s
    lv = np.zeros((TM, TM), np.int32)
    for bit in range(N_LEVELS):
        lv[(x >> bit) == 1] = bit + 1
    ok = (t < s) if reverse else (t > s)
    return np.where(ok, lv, 0).astype(np.int32)


def _shift_rows(x, k):
    return pltpu.roll(x, k % TM, 0)


def _cumsum_rows(x, row, reverse):
    sh = 1
    while sh < TM:
        if reverse:
            x = x + jnp.where(row < TM - sh, _shift_rows(x, -sh), 0.0)
        else:
            x = x + jnp.where(row >= sh, _shift_rows(x, sh), 0.0)
        sh *= 2
    return x


def _anchors(cum, row, reverse):
    out = []
    z = cum
    for bit in range(N_LEVELS):
        h = 1 << bit
        if 2 * h >= 16:
            c3 = cum.reshape(TM // (2 * h), 2 * h, DK_B)
            p = h if reverse else h - 1
            out.append(jnp.broadcast_to(c3[:, p:p + 1, :], c3.shape).reshape(TM, DK_B))
            continue
        hi = (row & h) != 0
        if reverse:
            out.append(jnp.where(hi, z, _shift_rows(z, -h)))
            z = jnp.where(hi, _shift_rows(z, h), z)
        else:
            out.append(jnp.where(hi, _shift_rows(z, h), z))
            z = jnp.where(hi, z, _shift_rows(z, -h))
    return out


LOW_BITS = 5
SAFE_RANGE = 60.0


def _low_map(reverse):
    t = np.arange(TM)[:, None]
    s = np.arange(TM)[None, :]
    same = (t >> LOW_BITS) == (s >> LOW_BITS)
    ok = (t <= s) if reverse else (t >= s)
    return (same & ok).astype(np.int32)


def _block_anchor(cum, reverse):
    n = 1 << LOW_BITS
    c3 = cum.reshape(TM // n, n, DK_B)
    p = n - 1 if reverse else 0
    return jnp.broadcast_to(c3[:, p:p + 1, :], c3.shape).reshape(TM, DK_B)


def _hgrn_kernel(reverse, l, *refs):
    (q_ref, v_ref, f_ref, lbp_ref, s0_ref, lv_ref, lo_ref, _, o_ref, ns_ref) = refs[:10]
    st_ref, qs_ref, ks_ref, cs_ref, os_ref, as_ref = refs[-6:]
    g = pl.program_id(0)
    i = NT - 1 - g if reverse else g
    is_ctx = i < CTX_TILES
    pos = i % SMP_TILES
    seq_start = jnp.logical_and(i >= CTX_TILES, pos == (SMP_TILES - 1 if reverse else 0))

    @pl.when(is_ctx)
    def _():
        st_ref[...] = jnp.zeros_like(st_ref)

    @pl.when(seq_start)
    def _():
        for h in range(H_B):
            st_ref[h] = s0_ref[0, h].T

    d = 1 if reverse else 0
    rows = [lbp_ref[k * 2 + d:k * 2 + d + 1, :] for k in range(DEPTH)]
    mx = functools.reduce(jnp.maximum, rows)
    ex = [jnp.exp(r - mx) for r in rows]
    den = functools.reduce(jnp.add, ex)
    lb_all = jnp.zeros_like(mx)
    for k in range(1, l + 1):
        lb_all = lb_all + ex[k] / den

    row = lax.broadcasted_iota(jnp.int32, (TM, DK_B), 0)
    spread = jnp.zeros((1, DK_B), F32)
    for h in range(H_B):
        sl = slice(h * DK_B, (h + 1) * DK_B)
        lb = lb_all[:, sl]
        fg = lb + (1.0 - lb) * _sigmoid_small_accurate(f_ref[:, sl])
        cum = _cumsum_rows(jnp.log(fg), row, reverse)
        ks_ref[h] = 1.0 - fg
        qs_ref[h] = _silu(q_ref[:, sl])
        cs_ref[h] = cum
        spread = jnp.maximum(spread, jnp.max(_block_anchor(cum, reverse) - cum, axis=0, keepdims=True))
    single_anchor_ok = jnp.max(spread) < SAFE_RANGE

    half = TM // 2
    halves = (slice(0, half), slice(half, TM))
    late, early = (halves[0], halves[1]) if reverse else (halves[1], halves[0])

    lv = lv_ref[0:half, 0:half]

    def operands(h, fq, fk):
        return (qs_ref[h] * fq).astype(BF16), (ks_ref[h] * fk).astype(BF16)

    for h in range(H_B):
        sl = slice(h * DK_B, (h + 1) * DK_B)
        q, kg, cum = qs_ref[h], ks_ref[h], cs_ref[h]
        vb = v_ref[:, sl].astype(BF16)
        tot = cum[0:1, :] if reverse else cum[TM - 1:TM, :]
        st = st_ref[h]
        o = _dot_nt((q * jnp.exp(cum)).astype(BF16), st.astype(BF16))
        anchors = _anchors(cum, row, reverse)
        fac = jnp.exp(-jnp.abs(cum - anchors[N_LEVELS - 1]))
        qf, kf = operands(h, fac, fac)
        o_top = _dot(_dot_nt(qf[late], kf[early]).astype(BF16), vb[early])
        zero = jnp.zeros_like(o_top)
        os_ref[h] = o + jnp.concatenate([o_top, zero] if reverse else [zero, o_top], axis=0)
        a = [jnp.zeros((half, half), F32) for _ in halves]
        for bit in range(LOW_BITS, N_LEVELS - 1):
            fac = jnp.exp(-jnp.abs(cum - anchors[bit]))
            qf, kf = operands(h, fac, fac)
            a = [jnp.where(lv == bit + 1, _dot_nt(qf[r], kf[r]), a[b]) for b, r in enumerate(halves)]
        for b in range(2):
            as_ref[h, b] = a[b]
        kbar = (kg * jnp.exp(tot - cum)).astype(BF16)
        st_ref[h] = jnp.exp(tot) * st + _dot_tn(vb, kbar)

    def heads(single_anchor):
        for h in range(H_B):
            sl = slice(h * DK_B, (h + 1) * DK_B)
            cum = cs_ref[h]
            v = v_ref[:, sl]
            vb = v.astype(BF16)
            o = os_ref[h]
            a = [as_ref[h, b] for b in range(2)]
            if single_anchor:
                e = _block_anchor(cum, reverse)
                qf, kf = operands(h, jnp.exp(cum - e), jnp.exp(e - cum))
                lo = lo_ref[0:half, 0:half] != 0
                a = [jnp.where(lo, _dot_nt(qf[r], kf[r]), a[b]) for b, r in enumerate(halves)]
            else:
                o = o + jnp.sum(qs_ref[h] * ks_ref[h], axis=-1, keepdims=True) * v
                anchors = _anchors(cum, row, reverse)
                for bit in range(LOW_BITS):
                    fac = jnp.exp(-jnp.abs(cum - anchors[bit]))
                    qf, kf = operands(h, fac, fac)
                    a = [jnp.where(lv == bit + 1, _dot_nt(qf[r], kf[r]), a[b]) for b, r in enumerate(halves)]
            o_ref[:, sl] = o + jnp.concatenate([_dot(a[b].astype(BF16), vb[r]) for b, r in enumerate(halves)],
                                               axis=0)

    @pl.when(single_anchor_ok)
    def _():
        heads(True)

    @pl.when(jnp.logical_not(single_anchor_ok))
    def _():
        heads(False)

    @pl.when(is_ctx)
    def _():
        for h in range(H_B):
            ns_ref[0, 0, 0, h] = st_ref[h].T


def _hgrn_pass(reverse, l, u, hgrn_lb, s0, maps, new_state):
    tile = (lambda g: NT - 1 - g) if reverse else (lambda g: g)
    col = lambda c: pl.BlockSpec((TM, CB), lambda g: (tile(g), c))
    s0_spec = pl.BlockSpec((1, H_B, DK_B, DK_B),
                           lambda g: (jnp.clip((tile(g) - CTX_TILES) // SMP_TILES, 0, DEC_BATCH - 1), 0, 0, 0))
    lbp_spec = pl.BlockSpec((DEPTH * 2, D_B), lambda g: (0, 0))
    map_spec = pl.BlockSpec((TM, TM), lambda g: (0, 0))
    d = 1 if reverse else 0
    ns_spec = pl.BlockSpec((1, 1, 1, H_B, DK_B, DK_B),
                           lambda g: (jnp.minimum(tile(g), CTX_TILES - 1), l, d, 0, 0, 0))
    any_spec = pl.BlockSpec(memory_space=pl.ANY)
    f_col = C_HFB if reverse else C_HFF
    in_specs = [col(C_HQ), col(C_HI), col(f_col), lbp_spec, s0_spec, map_spec, map_spec, any_spec]
    args = (u, u, u, hgrn_lb, s0, *maps, new_state)
    o_shape = jax.ShapeDtypeStruct((T, D_B), F32)
    head_tile = pltpu.VMEM((H_B, TM, DK_B), F32)
    return pl.pallas_call(
        functools.partial(_hgrn_kernel, reverse, l),
        grid=(NT,),
        in_specs=in_specs,
        out_specs=[pl.BlockSpec((TM, D_B), lambda g: (tile(g), 0)), ns_spec],
        out_shape=[o_shape, jax.ShapeDtypeStruct(new_state.shape, F32)],
        input_output_aliases={len(args) - 1: 1},
        scratch_shapes=[pltpu.VMEM((H_B, DK_B, DK_B), F32), head_tile, head_tile, head_tile, head_tile,
                        pltpu.VMEM((H_B, 2, TM // 2, TM // 2), F32)],
        compiler_params=_cparams(("arbitrary",)),
        name="hgrn_bwd" if reverse else "hgrn_fwd",
    )(*args)


def _rope_tables():
    rows = DEC_SEQ // GRID_W
    row = jnp.broadcast_to(jnp.arange(rows)[:, None], (rows, GRID_W)).reshape(-1).astype(F32)
    colp = jnp.broadcast_to(jnp.arange(GRID_W)[None, :], (rows, GRID_W)).reshape(-1).astype(F32)
    half = DH_C // 2
    inv = ROPE_BASE ** (-jnp.arange(0, half, 2, dtype=F32) / half)
    ang = jnp.concatenate([row[:, None] * inv, colp[:, None] * inv], axis=-1)
    cos = jnp.repeat(jnp.cos(ang), 2, axis=-1)
    sin = jnp.repeat(jnp.sin(ang), 2, axis=-1) * jnp.tile(jnp.array([-1.0, 1.0], F32), half)
    cos = jnp.concatenate([jnp.ones((IN_TM, DH_C), F32), cos], axis=0)
    sin = jnp.concatenate([jnp.zeros((IN_TM, DH_C), F32), sin], axis=0)
    return jnp.tile(cos, (1, 2)), jnp.tile(sin, (1, 2))


def _attn_kernel(lam_init, n_new, has_cache, n_heads, *refs):
    if has_cache:
        qt_ref, kc_ref, vc_ref, kn_ref, vn_ref, lam_ref, sub_ref, _, o_ref = refs[:9]
    else:
        qt_ref, kn_ref, vn_ref, lam_ref, sub_ref, _, o_ref = refs[:7]
    qm_ref, s_ref, p_ref, m_ref, l_ref, a_ref, acc_ref = refs[-7:]
    tq = qt_ref.shape[1]
    ncb = tq // 128
    lq = lam_ref[...]
    lam = (jnp.exp(jnp.sum(lq[0:1] * lq[1:2], axis=-1, keepdims=True))
           - jnp.exp(jnp.sum(lq[2:3] * lq[3:4], axis=-1, keepdims=True)) + lam_init)

    def scores(k, buf):
        for c in range(2):
            s = _dot(k, qm_ref[c])
            for cb in range(ncb):
                s_ref[buf, c, cb] = s[:, cb * 128:(cb + 1) * 128]

    def softmax_update(buf):
        for c in range(2):
            for cb in range(ncb):
                cols = slice(cb * 128, (cb + 1) * 128)
                s = s_ref[buf, c, cb]
                m_old = m_ref[c, :, cols]
                m_new = jnp.maximum(m_old, jnp.max(s, axis=0, keepdims=True))
                alpha = jnp.exp2(m_old - m_new)
                p = jnp.exp2(s - m_new)
                l_ref[c, :, cols] = alpha * l_ref[c, :, cols] + jnp.sum(p, axis=0, keepdims=True)
                m_ref[c, :, cols] = m_new
                a_ref[buf, c, :, cols] = alpha
                p_ref[buf, c, cb] = p.astype(BF16)

    def accumulate(vt, buf):
        for c in range(2):
            p = jnp.concatenate([p_ref[buf, c, cb] for cb in range(ncb)], axis=1)
            acc_ref[c] = a_ref[buf, c] * acc_ref[c] + _dot(vt, p)

    for h in range(n_heads):
        hs = slice(h * 128, (h + 1) * 128)
        qt = qt_ref[hs, :]
        sub_i = lax.broadcasted_iota(jnp.int32, qt.shape, 0)
        qm_ref[0] = jnp.where(sub_i < DH_C, qt, jnp.zeros_like(qt))
        qm_ref[1] = jnp.where(sub_i >= DH_C, qt, jnp.zeros_like(qt))
        m_ref[...] = jnp.full(m_ref.shape, -1e30, F32)
        l_ref[...] = jnp.zeros_like(l_ref)
        acc_ref[...] = jnp.zeros_like(acc_ref)

        if n_new == 1:
            scores(kn_ref[:, hs], 0)
            softmax_update(0)
            accumulate(vn_ref[0, hs, :], 0)
        else:
            new_k = lambda j: kn_ref[pl.ds(pl.multiple_of(j * TM, TM), TM), hs]
            scores(new_k(0), 0)

            def body(i, carry):
                scores(new_k(2 * i + 1), 1)
                softmax_update(0)
                accumulate(vn_ref[2 * i, hs, :], 0)
                scores(new_k(2 * i + 2), 0)
                softmax_update(1)
                accumulate(vn_ref[2 * i + 1, hs, :], 1)
                return carry
            lax.fori_loop(0, n_new // 2 - 1, body, 0)
            scores(kn_ref[(n_new - 1) * TM:, hs], 1)
            softmax_update(0)
            accumulate(vn_ref[n_new - 2, hs, :], 0)
            scores(kc_ref[:, hs], 0)
            softmax_update(1)
            accumulate(vn_ref[n_new - 1, hs, :], 1)
            softmax_update(0)
            accumulate(vc_ref[0, hs, :], 0)

        for cb in range(ncb):
            cols = slice(cb * 128, (cb + 1) * 128)
            o = acc_ref[0, :, cols] / l_ref[0, :, cols] - lam * (acc_ref[1, :, cols] / l_ref[1, :, cols])
            y = o * lax.rsqrt(jnp.mean(o * o, axis=0, keepdims=True) + 1e-6)
            o_ref[cols, hs] = (y.T * sub_ref[...] * (1.0 - lam_init)).astype(BF16)


def _attn_scratch(tq):
    ncb = tq // 128
    return [pltpu.VMEM((2, 128, tq), BF16), pltpu.VMEM((2, 2, ncb, TM, 128), F32),
            pltpu.VMEM((2, 2, ncb, TM, 128), BF16),
            pltpu.VMEM((2, 1, tq), F32), pltpu.VMEM((2, 1, tq), F32), pltpu.VMEM((2, 2, 1, tq), F32),
            pltpu.VMEM((2, 128, tq), F32)]


def _attention(l, qt, kb, vt, ck, cvt, lambda_qk_l, subln_l):
    lam_init = 0.8 - 0.6 * math.exp(-0.3 * l)
    sub = subln_l.reshape(1, 2 * DH_C)
    o_ctx = pl.pallas_call(
        functools.partial(_attn_kernel, lam_init, 1, False, H_C),
        grid=(BATCH,),
        in_specs=[pl.BlockSpec((D_C, SEQ), lambda b: (0, b)),
                  pl.BlockSpec((SEQ, D_C), lambda b: (b, 0)),
                  pl.BlockSpec((1, D_C, TM), lambda b: (b, 0, 0)),
                  pl.BlockSpec((4, DH_C), lambda b: (0, 0)),
                  pl.BlockSpec((1, 2 * DH_C), lambda b: (0, 0)),
                  pl.BlockSpec(memory_space=pl.ANY)],
        out_specs=pl.BlockSpec((SEQ, D_C), lambda b: (b, 0)),
        out_shape=jax.ShapeDtypeStruct((T, D_C), BF16),
        input_output_aliases={5: 0},
        scratch_shapes=_attn_scratch(SEQ),
        compiler_params=_cparams(("arbitrary",)),
        name="attn_ctx",
    )(qt, kb, vt, lambda_qk_l, sub, jnp.zeros((T, D_C), BF16))

    nq = DEC_SEQ // TQ
    base_q = BATCH * SEQ // TQ
    base_kv = BATCH * SEQ // DEC_SEQ
    o_smp = pl.pallas_call(
        functools.partial(_attn_kernel, lam_init, SMP_TILES, True, 1),
        grid=(DEC_BATCH, H_C, nq),
        in_specs=[pl.BlockSpec((128, TQ), lambda b, h, j: (h, base_q + b * nq + j)),
                  pl.BlockSpec((PAST, 128), lambda b, h, j: (b, h)),
                  pl.BlockSpec((1, 128, PAST), lambda b, h, j: (b, h, 0)),
                  pl.BlockSpec((DEC_SEQ, 128), lambda b, h, j: (base_kv + b, h)),
                  pl.BlockSpec((SMP_TILES, 128, TM), lambda b, h, j: (base_kv + b, h, 0)),
                  pl.BlockSpec((4, DH_C), lambda b, h, j: (0, 0)),
                  pl.BlockSpec((1, 2 * DH_C), lambda b, h, j: (0, 0)),
                  pl.BlockSpec(memory_space=pl.ANY)],
        out_specs=pl.BlockSpec((TQ, 128), lambda b, h, j: (base_q + b * nq + j, h)),
        out_shape=jax.ShapeDtypeStruct((T, D_C), BF16),
        input_output_aliases={7: 0},
        scratch_shapes=_attn_scratch(TQ),
        compiler_params=_cparams(("arbitrary", "arbitrary", "arbitrary")),
        name="attn_smp",
    )(qt, ck, cvt, kb, vt, lambda_qk_l, sub, o_ctx)
    return o_smp


MF_SUB = 2


def _merge_ffn_kernel(nx, *refs):
    (a_ref, bf_ref, bb_ref, hg_ref, hn_ref, c_ref, ga_ref, gb_ref, gc_ref, g1_ref, wb_ref, wo_ref,
     n2_ref, sc_ref, sh_ref, g2_ref, wi_ref, wd_ref) = refs[nx:nx + 18]
    o_refs = refs[nx + 18:]
    rows = [slice(t * (TM // MF_SUB), (t + 1) * (TM // MF_SUB)) for t in range(MF_SUB)]

    def readout(r):
        hb = []
        for h in range(H_B):
            hs = slice(h * DK_B, (h + 1) * DK_B)
            o = bf_ref[r, hs] + bb_ref[r, hs]
            y = o * lax.rsqrt(jnp.mean(o * o, axis=-1, keepdims=True) + 1e-6) * hn_ref[...]
            hb.append((y * _silu(hg_ref[r, hs])).astype(BF16))
        return jnp.concatenate(hb, axis=1)

    def x_rows(r):
        xr = refs[:nx]
        return jnp.where(pl.program_id(0) < CTX_TILES, xr[0][r, :], xr[1][r, :]) if nx == 2 else xr[0][r, :]

    hb = [readout(r) for r in rows]
    ys = [(_dot(a_ref[r, :], wb_ref[0:D_A, :]), _dot(hb[t], wb_ref[D_A:D_A + D_B, :]),
           _dot(c_ref[r, :], wb_ref[D_A + D_B:, :])) for t, r in enumerate(rows)]
    ms = [(_sigmoid(ga_ref[r, :]) * ys[t][0] + _sigmoid(gb_ref[r, :]) * ys[t][1]
           + _sigmoid(gc_ref[r, :]) * ys[t][2]).astype(BF16) for t, r in enumerate(rows)]
    x1 = [x_rows(r) + g1_ref[0] * _dot(ms[t], wo_ref[...]) for t, r in enumerate(rows)]
    hh = [_rms_mod(x1[t], n2_ref[...], sc_ref[0], sh_ref[0]).astype(BF16) for t in range(MF_SUB)]
    gu = [_dot(hh[t], wi_ref[...]) for t in range(MF_SUB)]
    ff = [(_silu(gu[t][:, :D_FF]) * gu[t][:, D_FF:]).astype(BF16) for t in range(MF_SUB)]
    ys2 = [x1[t] + g2_ref[0] * _dot(ff[t], wd_ref[...]) for t in range(MF_SUB)]
    y = jnp.concatenate(ys2, axis=0)
    if len(o_refs) == 1:
        o_refs[0][...] = y
    else:
        @pl.when(pl.program_id(0) < CTX_TILES)
        def _():
            o_refs[0][...] = y

        @pl.when(pl.program_id(0) >= CTX_TILES)
        def _():
            o_refs[1][...] = y


def _merge_ffn(ya, of, ob, hgrn_norm_l, yc, u, x, modt, w_branch_l, w_out_l, norm2_l, w_ffn_in_l, w_ffn_out_l, l,
               split_out):
    row = lambda w: pl.BlockSpec((TM, w), lambda i: (i, 0))
    gate0 = C_GT * CB // D
    gate = lambda k: pl.BlockSpec((TM, D), lambda i: (i, gate0 + k))
    resident = lambda shape: pl.BlockSpec(shape, lambda i: (0, 0), pipeline_mode=pl.Buffered(1))
    x_specs, xs = _x_specs(x)
    if split_out:
        out_specs, _ = _x_specs((None, None))
        out_shape = [jax.ShapeDtypeStruct((BATCH * SEQ, D), F32), jax.ShapeDtypeStruct((DEC_BATCH * DEC_SEQ, D), F32)]
    else:
        out_specs = row(D)
        out_shape = jax.ShapeDtypeStruct((T, D), F32)
    return pl.pallas_call(
        functools.partial(_merge_ffn_kernel, len(xs)),
        grid=(NT,),
        in_specs=x_specs + [row(D_A), row(D_B), row(D_B), pl.BlockSpec((TM, CB), lambda i: (i, C_HG)),
                            pl.BlockSpec((1, DK_B), lambda i: (0, 0)), row(D_C),
                            gate(0), gate(1), gate(2), _mod_spec(l, 2),
                            resident((D_A + D_B + D_C, D)), resident((D, D)),
                            pl.BlockSpec((1, D), lambda i: (0, 0)),
                            _mod_spec(l, 4), _mod_spec(l, 3), _mod_spec(l, 5),
                            resident((D, 2 * D_FF)), resident((D_FF, D))],
        out_specs=out_specs,
        out_shape=out_shape,
        compiler_params=_cparams(("arbitrary",)),
        name="merge_ffn",
    )(*xs, ya, of, ob, u, hgrn_norm_l.reshape(1, DK_B), yc, u, u, u, modt, w_branch_l, w_out_l,
      norm2_l.reshape(1, D), modt, modt, modt, w_ffn_in_l, w_ffn_out_l)


def _layer(l, x, modt, cache_k, cache_v, state_hgrn, tabs, p, outs):
    (norm1, norm2, w_in, conv_w, conv_b, conv_ln_g, conv_ln_b, hgrn_lb, hgrn_norm, q_norm, k_norm, lambda_qk,
     subln, w_branch, w_out, w_ffn_in, w_ffn_out) = p
    cos_t, sin_t, bd, maps_f, maps_b = tabs
    new_k, new_v, new_s = outs
    u, qt, kb, vt, new_k, new_v = _in_proj(x, modt, norm1[l], w_in[l].astype(BF16), l, q_norm[l], k_norm[l],
                                           cos_t, sin_t, bd, new_k, new_v)
    ya = _conv_branch(u, conv_w[l], conv_b[l], conv_ln_g[l], conv_ln_b[l])
    lbp = hgrn_lb.reshape(DEPTH * 2, D_B)
    ob, new_s = _hgrn_pass(True, l, u, lbp, state_hgrn[:, l, 1], maps_b, new_s)
    of, new_s = _hgrn_pass(False, l, u, lbp, state_hgrn[:, l, 0], maps_f, new_s)
    ck = cache_k[:, l].reshape(DEC_BATCH * PAST, D_C).astype(BF16)
    cvt = cache_v[:, l].reshape(DEC_BATCH, PAST, D_C).astype(BF16).transpose(0, 2, 1)
    yc = _attention(l, qt, kb, vt, ck, cvt, lambda_qk[l], subln[l])
    x = _merge_ffn(ya, of, ob, hgrn_norm[l], yc, u, x, modt, w_branch[l].astype(BF16), w_out[l].astype(BF16), norm2[l],
                   w_ffn_in[l].astype(BF16), w_ffn_out[l].astype(BF16), l, split_out=l == DEPTH - 1)
    return x, (new_k, new_v, new_s)


def kernel(x_prompt, x_sample, cache_k, cache_v, state_hgrn, c, c_ctx, w_mod, b_mod, norm1, norm2, w_in, conv_w,
           conv_b, conv_ln_g, conv_ln_b, hgrn_lb, hgrn_norm, q_norm, k_norm, lambda_qk, subln, w_branch, w_out,
           w_ffn_in, w_ffn_out):
    x = (x_prompt.reshape(BATCH * SEQ, D), x_sample.reshape(DEC_BATCH * DEC_SEQ, D))
    cv8 = jnp.concatenate([c_ctx[None, :], c, jnp.zeros((8 - 1 - DEC_BATCH, D), F32)], axis=0)
    mod = _modulation(cv8, w_mod, b_mod)
    modt = mod.reshape(DEPTH, 8, 6, D).transpose(0, 2, 1, 3).reshape(DEPTH * 6 * 8, 1, D)
    cos_t, sin_t = _rope_tables()
    seg = np.arange(D_C) // DH_C
    bd = jnp.asarray(seg[:, None] == seg[None, :], BF16)
    maps = lambda rev: (jnp.asarray(_level_map(rev)), jnp.asarray(_low_map(rev)))
    tabs = (cos_t, sin_t, bd, maps(False), maps(True))
    p = (norm1, norm2, w_in, conv_w, conv_b, conv_ln_g, conv_ln_b, hgrn_lb, hgrn_norm, q_norm, k_norm, lambda_qk,
         subln, w_branch, w_out, w_ffn_in, w_ffn_out)
    outs = (jnp.zeros((BATCH, DEPTH, SEQ, D_C), F32), jnp.zeros((BATCH, DEPTH, SEQ, D_C), F32),
            jnp.zeros((BATCH, DEPTH, 2, H_B, DK_B, DK_B), F32))
    for l in range(DEPTH):
        x, outs = _layer(l, x, modt, cache_k, cache_v, state_hgrn, tabs, p, outs)
    new_k, new_v, new_s = outs
    y_prompt = x[0].reshape(BATCH, SEQ, D)
    y_sample = x[1].reshape(DEC_BATCH, DEC_SEQ, D)
    return (y_prompt, y_sample, new_k.reshape(BATCH, DEPTH, SEQ, H_C, 2, DH_C),
            new_v.reshape(BATCH, DEPTH, SEQ, H_C, 2 * DH_C), new_s)
```
